```python
import math
import jax
import jax.numpy as jnp
from jax import lax
import numpy as np

D_MODEL = 2048
BATCH = 4
SEQ = 2048
DEPTH = 4

CHUNK = 64
EPS = 1e-6
N_BRANCH = 4
BRANCH_WIDTH = D_MODEL // N_BRANCH

DIFF_HEADS = 4
DIFF_QK_DIM = BRANCH_WIDTH // (2 * DIFF_HEADS)
DIFF_V_DIM = 2 * DIFF_QK_DIM
QUERY_BLOCK = 128

GLA_HEADS = 4
GLA_KEY_WIDTH = BRANCH_WIDTH // 2
GLA_DK = GLA_KEY_WIDTH // GLA_HEADS
GLA_DV = BRANCH_WIDTH // GLA_HEADS
GLA_GATE_RANK = 16
GLA_GATE_NORMALIZER = 16.0

CONV_CHANNELS = BRANCH_WIDTH
CONV_WIDTH = 31

SSD_HEAD_DIM = 64
SSD_HEADS = BRANCH_WIDTH // SSD_HEAD_DIM
SSD_GROUPS = 2
SSD_STATE = 128
SSD_CONV = 4
SSD_XBC = BRANCH_WIDTH + 2 * SSD_GROUPS * SSD_STATE

D_FF = 5632
FFN_CONV = 3

IN_SPLITS = (
    BRANCH_WIDTH, BRANCH_WIDTH, BRANCH_WIDTH,
    GLA_KEY_WIDTH, GLA_KEY_WIDTH, BRANCH_WIDTH, BRANCH_WIDTH, GLA_GATE_RANK,
    2 * CONV_CHANNELS,
    BRANCH_WIDTH, SSD_XBC, SSD_HEADS,
)
IN_COLS = sum(IN_SPLITS)

kernel_name = 'hybrid_gated_parallel_mixer_encoder'


def split_cols(u, sizes):
    offsets = np.cumsum(sizes)[:-1].tolist()
    return jnp.split(u, offsets, axis=-1)


def rms_norm(x, w, eps=EPS):
    xf = x.astype(jnp.float32)
    y = xf * lax.rsqrt(jnp.mean(xf * xf, axis=-1, keepdims=True) + eps)
    return (y * w.astype(jnp.float32)).astype(x.dtype)


def layer_norm(x, w, b, eps=EPS):
    xf = x.astype(jnp.float32)
    mu = jnp.mean(xf, axis=-1, keepdims=True)
    var = jnp.mean(jnp.square(xf - mu), axis=-1, keepdims=True)
    y = (xf - mu) * lax.rsqrt(var + eps)
    return (y * w.astype(jnp.float32) + b.astype(jnp.float32)).astype(x.dtype)


def causal_dwconv(x, w, b):
    width = w.shape[0]
    y = lax.conv_general_dilated(
        x, w[:, None, :].astype(x.dtype), window_strides=(1,), padding=[(width - 1, 0)],
        dimension_numbers=('NWC', 'WIO', 'NWC'), feature_group_count=x.shape[-1])
    return y + b


def diff_attention_branch(q, k, v, qk_gain, lam, subln, lambda_init):
    b, s, _ = q.shape
    q = rms_norm(q.reshape(b, s, 2 * DIFF_HEADS, DIFF_QK_DIM), qk_gain[0]) * (DIFF_QK_DIM ** -0.5)
    k = rms_norm(k.reshape(b, s, 2 * DIFF_HEADS, DIFF_QK_DIM), qk_gain[1])
    v = v.reshape(b, s, DIFF_HEADS, DIFF_V_DIM)
    lam = lam.astype(jnp.float32)
    lam_full = jnp.exp(jnp.sum(lam[0] * lam[1])) - jnp.exp(jnp.sum(lam[2] * lam[3])) + lambda_init
    outs = []
    for i in range(s // QUERY_BLOCK):
        q0 = i * QUERY_BLOCK
        kv_len = q0 + QUERY_BLOCK
        scores = jnp.einsum('bqhd,bkhd->bhqk', q[:, q0:kv_len], k[:, :kv_len]).astype(jnp.float32)
        q_chunk = (q0 + jnp.arange(QUERY_BLOCK)) // CHUNK
        k_chunk = jnp.arange(kv_len) // CHUNK
        scores = jnp.where(k_chunk[None, :] <= q_chunk[:, None], scores, -jnp.inf)
        probs = jax.nn.softmax(scores, axis=-1).reshape(b, DIFF_HEADS, 2, QUERY_BLOCK, kv_len)
        attn = (probs[:, :, 0] - lam_full * probs[:, :, 1]).astype(v.dtype)
        outs.append(jnp.einsum('bhqk,bkhd->bqhd', attn, v[:, :kv_len]))
    o = jnp.concatenate(outs, axis=1)
    o = rms_norm(o, subln) * (1.0 - lambda_init)
    return o.reshape(b, s, BRANCH_WIDTH)


def gla_chunked(q, k, v, gk):
    b, s, h, dk = q.shape
    dv = v.shape[-1]
    nc = s // CHUNK
    dtype = q.dtype

    def to_chunks(t):
        return t.reshape(b, nc, CHUNK, h, t.shape[-1]).transpose(1, 0, 3, 2, 4)

    causal = jnp.tril(jnp.ones((CHUNK, CHUNK), dtype=bool))

    def step(state, inp):
        q_c, k_c, v_c, g_c = inp
        cum = jnp.cumsum(g_c.astype(jnp.float32), axis=2)
        diff = cum[:, :, :, None, :] - cum[:, :, None, :, :]
        decay = jnp.exp(jnp.where(causal[:, :, None], diff, -jnp.inf)).astype(dtype)
        scores = jnp.einsum('bhtd,bhsd,bhtsd->bhts', q_c, k_c, decay)
        o = (jnp.einsum('bhts,bhse->bhte', scores, v_c)
             + jnp.einsum('bhtd,bhde->bhte', q_c * jnp.exp(cum).astype(dtype), state))
        k_dec = k_c * jnp.exp(cum[:, :, -1:] - cum).astype(dtype)
        state = (jnp.exp(cum[:, :, -1]).astype(dtype)[..., None] * state
                 + jnp.einsum('bhsd,bhse->bhde', k_dec, v_c))
        return state, o

    state0 = jnp.zeros((b, h, dk, dv), dtype)
    _, o = lax.scan(step, state0, (to_chunks(q), to_chunks(k), to_chunks(v), to_chunks(gk)))
    return o.transpose(1, 0, 3, 2, 4).reshape(b, s, h, dv)


def gla_branch(q, k, v, out_gate, gate_low, gk_w2, gk_b, norm_w):
    b, s, _ = q.shape
    gk = jax.nn.log_sigmoid((gate_low @ gk_w2 + gk_b).astype(jnp.float32)) / GLA_GATE_NORMALIZER
    o = gla_chunked(
        q.reshape(b, s, GLA_HEADS, GLA_DK) * (GLA_DK ** -0.5),
        k.reshape(b, s, GLA_HEADS, GLA_DK),
        v.reshape(b, s, GLA_HEADS, GLA_DV),
        gk.reshape(b, s, GLA_HEADS, GLA_DK))
    o = rms_norm(o, norm_w).reshape(b, s, BRANCH_WIDTH)
    return o * jax.nn.silu(out_gate)


def conformer_conv_branch(u, dw_w, dw_b, ln_w, ln_b):
    a, gate = jnp.split(u, 2, axis=-1)
    c = causal_dwconv(a * jax.nn.sigmoid(gate), dw_w, dw_b)
    return jax.nn.silu(layer_norm(c, ln_w, ln_b))


def ssd_chunked(x, dt, a, bm, cm):
    b, s, h, p = x.shape
    g, n = bm.shape[-2:]
    e = h // g
    nc = s // CHUNK
    dtype = x.dtype
    xc = x.reshape(b, nc, CHUNK, g, e, p)
    dtc = dt.reshape(b, nc, CHUNK, g, e)
    bc = bm.reshape(b, nc, CHUNK, g, n)
    cc = cm.reshape(b, nc, CHUNK, g, n)
    log_a = (dtc.astype(jnp.float32) * a.reshape(g, e)).transpose(0, 3, 4, 1, 2)
    cum = jnp.cumsum(log_a, axis=-1)
    causal = jnp.tril(jnp.ones((CHUNK, CHUNK), dtype=bool))
    seg = cum[..., :, None] - cum[..., None, :]
    decay_in = jnp.exp(jnp.where(causal, seg, -jnp.inf)).astype(dtype)
    xdt = xc * dtc[..., None].astype(dtype)
    cb = jnp.einsum('bclgn,bcsgn->bgcls', cc, bc)
    y_diag = jnp.einsum('bgcls,bgecls,bcsgep->bclgep', cb, decay_in, xdt)
    decay_to_end = jnp.exp(cum[..., -1:] - cum).astype(dtype)
    chunk_states = jnp.einsum('bcsgn,bgecs,bcsgep->cbgepn', bc, decay_to_end, xdt)
    chunk_decay = jnp.exp(cum[..., -1]).astype(dtype).transpose(3, 0, 1, 2)

    def step(state, inp):
        st, dec = inp
        return dec[..., None, None] * state + st, state

    _, h_prev = lax.scan(step, jnp.zeros((b, g, e, p, n), dtype), (chunk_states, chunk_decay))
    y_off = jnp.einsum('bclgn,bgecl,cbgepn->bclgep', cc, jnp.exp(cum).astype(dtype), h_prev)
    return (y_diag + y_off).reshape(b, s, h, p)


def ssd_branch(z, xbc, dt_raw, conv_w, conv_b, dt_bias, a_log, d_skip, norm_w):
    b, s, _ = z.shape
    xbc = jax.nn.silu(causal_dwconv(xbc, conv_w, conv_b))
    xs, bm, cm = split_cols(xbc, (BRANCH_WIDTH, SSD_GROUPS * SSD_STATE, SSD_GROUPS * SSD_STATE))
    xs = xs.reshape(b, s, SSD_HEADS, SSD_HEAD_DIM)
    dt = jax.nn.softplus(dt_raw + dt_bias)
    a = -jnp.exp(a_log.astype(jnp.float32))
    y = ssd_chunked(xs, dt, a, bm.reshape(b, s, SSD_GROUPS, SSD_STATE), cm.reshape(b, s, SSD_GROUPS, SSD_STATE))
    y = y + d_skip[:, None] * xs
    y = (y.reshape(b, s, BRANCH_WIDTH) * jax.nn.silu(z)).reshape(b, s, SSD_GROUPS, BRANCH_WIDTH // SSD_GROUPS)
    return rms_norm(y, norm_w.reshape(SSD_GROUPS, BRANCH_WIDTH // SSD_GROUPS)).reshape(b, s, BRANCH_WIDTH)


def setup_inputs(seed: int = 0) -> dict:
    key = jax.random.key(seed)
    ks = jax.random.split(key, 32)

    def nrm(i, shape, scale):
        return scale * jax.random.normal(ks[i], shape, jnp.float32)

    def gain(i, shape):
        return 1.0 + nrm(i, shape, 0.02)

    L = DEPTH
    dt0 = jnp.exp(jax.random.uniform(ks[15], (L, SSD_HEADS), jnp.float32, math.log(1e-3), math.log(1e-1)))
    return {
        'x': nrm(0, (BATCH, SEQ, D_MODEL), 1.0),
        'mix_norm': gain(1, (L, D_MODEL)),
        'w_in': nrm(2, (L, D_MODEL, IN_COLS), D_MODEL ** -0.5),
        'diff_qk_norm': gain(3, (L, 2, DIFF_QK_DIM)),
        'diff_lambda': nrm(4, (L, 4, DIFF_QK_DIM), 0.1),
        'diff_subln': gain(5, (L, DIFF_V_DIM)),
        'gla_gk_w2': nrm(6, (L, GLA_GATE_RANK, GLA_KEY_WIDTH), GLA_GATE_RANK ** -0.5),
        'gla_gk_b': nrm(7, (L, GLA_KEY_WIDTH), 0.1),
        'gla_norm': gain(8, (L, GLA_DV)),
        'conv_dw_w': nrm(9, (L, CONV_WIDTH, CONV_CHANNELS), CONV_WIDTH ** -0.5),
        'conv_dw_b': nrm(10, (L, CONV_CHANNELS), 0.02),
        'conv_ln_w': gain(11, (L, CONV_CHANNELS)),
        'conv_ln_b': nrm(12, (L, CONV_CHANNELS), 0.02),
        'ssd_conv_w': nrm(13, (L, SSD_CONV, SSD_XBC), SSD_CONV ** -0.5),
        'ssd_conv_b': nrm(14, (L, SSD_XBC), 0.02),
        'ssd_dt_bias': dt0 + jnp.log(-jnp.expm1(-dt0)),
        'ssd_a_log': jnp.log(jax.random.uniform(ks[16], (L, SSD_HEADS), jnp.float32, 1.0, 16.0)),
        'ssd_d': 1.0 + nrm(17, (L, SSD_HEADS), 0.1),
        'ssd_norm': gain(18, (L, BRANCH_WIDTH)),
        'w_branch': nrm(19, (L, N_BRANCH, BRANCH_WIDTH, D_MODEL), BRANCH_WIDTH ** -0.5),
        'w_gate': nrm(20, (L, N_BRANCH, D_MODEL, D_MODEL), D_MODEL ** -0.5),
        'b_gate': nrm(21, (L, N_BRANCH, D_MODEL), 0.1),
        'w_out': nrm(22, (L, D_MODEL, D_MODEL), D_MODEL ** -0.5),
        'ffn_norm': gain(23, (L, D_MODEL)),
        'ffn_w_up': nrm(24, (L, D_MODEL, 2 * D_FF), D_MODEL ** -0.5),
        'ffn_conv_w': nrm(25, (L, FFN_CONV, 2 * D_FF), FFN_CONV ** -0.5),
        'ffn_conv_b': nrm(26, (L, 2 * D_FF), 0.02),
        'ffn_w_down': nrm(27, (L, D_FF, D_MODEL), D_FF ** -0.5),
    }


def reference(x, mix_norm, w_in, diff_qk_norm, diff_lambda, diff_subln, gla_gk_w2, gla_gk_b, gla_norm,
              conv_dw_w, conv_dw_b, conv_ln_w, conv_ln_b, ssd_conv_w, ssd_conv_b, ssd_dt_bias, ssd_a_log,
              ssd_d, ssd_norm, w_branch, w_gate, b_gate, w_out, ffn_norm, ffn_w_up, ffn_conv_w, ffn_conv_b,
              ffn_w_down):
    for l in range(DEPTH):
        lambda_init = 0.8 - 0.6 * math.exp(-0.3 * l)
        h = rms_norm(x, mix_norm[l])
        (d_q, d_k, d_v, g_q, g_k, g_v, g_out, g_low, c_in, s_z, s_xbc, s_dt) = split_cols(h @ w_in[l], IN_SPLITS)
        branches = (
            diff_attention_branch(d_q, d_k, d_v, diff_qk_norm[l], diff_lambda[l], diff_subln[l], lambda_init),
            gla_branch(g_q, g_k, g_v, g_out, g_low, gla_gk_w2[l], gla_gk_b[l], gla_norm[l]),
            conformer_conv_branch(c_in, conv_dw_w[l], conv_dw_b[l], conv_ln_w[l], conv_ln_b[l]),
            ssd_branch(s_z, s_xbc, s_dt, ssd_conv_w[l], ssd_conv_b[l], ssd_dt_bias[l], ssd_a_log[l],
                       ssd_d[l], ssd_norm[l]),
        )
        merged = None
        for i, y in enumerate(branches):
            term = jax.nn.sigmoid(h @ w_gate[l, i] + b_gate[l, i]) * (y @ w_branch[l, i])
            merged = term if merged is None else merged + term
        x = x + merged @ w_out[l]
        h = rms_norm(x, ffn_norm[l])
        u = causal_dwconv(h @ ffn_w_up[l], ffn_conv_w[l], ffn_conv_b[l])
        gate, val = jnp.split(u, 2, axis=-1)
        x = x + (jax.nn.silu(gate) * val) @ ffn_w_down[l]
    return x
```

```python
import functools
import math

import jax
import jax.numpy as jnp
from jax import lax
from jax.experimental import pallas as pl
from jax.experimental.pallas import tpu as pltpu

F32 = jnp.float32
BF16 = jnp.bfloat16

EPS = 1e-6
D_MODEL = 2048
BRANCH = 512
D_FF = 5632
U_BIG = 5632
U_SMALL = 128
LANES = 128
SUBLANES = 8
VMEM_LIMIT = 56 * 1024 * 1024

GLA_CHUNK = 64
GLA_BLOCK = 8
SSD_CHUNK = 256
CONV_WIDTH = 31
SSD_CONV = 4
FFN_CONV = 3


def _params(*sem):
    return pltpu.CompilerParams(dimension_semantics=sem, vmem_limit_bytes=VMEM_LIMIT)


def _sigmoid(x):
    return 0.5 * jnp.tanh(0.5 * x) + 0.5


def _silu(x):
    return x * _sigmoid(x)


def _softplus(x):
    return jnp.maximum(x, 0.0) + jnp.log1p(jnp.exp(-jnp.abs(x)))


def _split3(x):
    hi = x.astype(BF16)
    r1 = x - hi.astype(F32)
    mid = r1.astype(BF16)
    lo = (r1 - mid.astype(F32)).astype(BF16)
    return hi, mid, lo


def _dot(a, b):
    return jnp.dot(a, b, preferred_element_type=F32)


def _dot_nt(a, b):
    return lax.dot_general(a, b, (((1,), (1,)), ((), ())), preferred_element_type=F32)


def _dot_tn(a, b):
    return lax.dot_general(a, b, (((0,), (0,)), ((), ())), preferred_element_type=F32)


def _select_dot(sel, x):
    hi, mid, lo = _split3(x)
    return _dot(sel, hi) + _dot(sel, mid) + _dot(sel, lo)


def _dot_select(x, sel):
    hi, mid, lo = _split3(x)
    return _dot(hi, sel) + _dot(mid, sel) + _dot(lo, sel)


def _shift_window(win, shift, rows):
    total = win.shape[0]
    if shift % SUBLANES == 0:
        return win[shift:shift + rows]
    rolled = pltpu.roll(win, total - (shift % SUBLANES), 0)
    base = shift - shift % SUBLANES
    return rolled[base:base + rows]


def _rmsnorm_kernel(x_ref, w_ref, o_ref):
    x = x_ref[...]
    ms = jnp.mean(x * x, axis=-1, keepdims=True)
    o_ref[...] = (x * lax.rsqrt(ms + EPS) * w_ref[...]).astype(o_ref.dtype)


def _rmsnorm(x, w3, l, tm=512):
    m, d = x.shape
    return pl.pallas_call(
        _rmsnorm_kernel,
        grid=(m // tm,),
        in_specs=[pl.BlockSpec((tm, d), lambda i: (i, 0)),
                  pl.BlockSpec((None, 1, d), lambda i: (l, 0, 0))],
        out_specs=pl.BlockSpec((tm, d), lambda i: (i, 0)),
        out_shape=jax.ShapeDtypeStruct((m, d), BF16),
        compiler_params=_params("arbitrary"),
        name="rmsnorm",
    )(x, w3)


def _matmul_kernel(a_ref, w_ref, o_ref):
    o_ref[...] = _dot(a_ref[...], w_ref[...]).astype(o_ref.dtype)


def _matmul_res_kernel(a_ref, w_ref, r_ref, o_ref):
    o_ref[...] = r_ref[...] + _dot(a_ref[...], w_ref[...])


def _matmul(a, w3, l, *, tm, tn, out_dtype, residual=None, name="matmul"):
    m, k = a.shape
    n = w3.shape[-1]
    in_specs = [pl.BlockSpec((tm, k), lambda j, i: (i, 0)),
                pl.BlockSpec((None, k, tn), lambda j, i: (l, 0, j))]
    args = [a, w3]
    body = _matmul_kernel
    if residual is not None:
        in_specs.append(pl.BlockSpec((tm, tn), lambda j, i: (i, j)))
        args.append(residual)
        body = _matmul_res_kernel
    return pl.pallas_call(
        body,
        grid=(n // tn, m // tm),
        in_specs=in_specs,
        out_specs=pl.BlockSpec((tm, tn), lambda j, i: (i, j)),
        out_shape=jax.ShapeDtypeStruct((m, n), out_dtype),
        compiler_params=_params("arbitrary", "arbitrary"),
        name=name,
    )(*args)


def _diff_attn_kernel(lambda_init, seq, tq, q_ref, k_ref, v_ref, gain_ref, lam_ref, subln_ref,
                      o_ref, kn_ref):
    lane = lax.broadcasted_iota(jnp.int32, (1, LANES), 1)
    first = lane < 64

    def halfnorm(t, g):
        sq = t * t
        s1 = jnp.sum(jnp.where(first, sq, 0.0), axis=-1, keepdims=True)
        s2 = jnp.sum(jnp.where(first, 0.0, sq), axis=-1, keepdims=True)
        ms = jnp.where(first, s1, s2) * (1.0 / 64)
        return t * lax.rsqrt(ms + EPS) * g

    kn_ref[...] = halfnorm(k_ref[...].astype(F32), gain_ref[1:2, :]).astype(BF16)
    lam = lam_ref[...]
    l1 = jnp.sum(lam[0:1] * lam[1:2], axis=-1, keepdims=True)
    l2 = jnp.sum(lam[2:3] * lam[3:4], axis=-1, keepdims=True)
    lam_full = jnp.exp(l1) - jnp.exp(l2) + lambda_init

    for i in range(seq // tq):
        q0 = i * tq
        kv = q0 + tq
        qn = halfnorm(q_ref[q0:q0 + tq, :].astype(F32), gain_ref[0:1, :]) * (64 ** -0.5)
        q1 = jnp.where(first, qn, 0.0).astype(BF16)
        q2 = jnp.where(first, 0.0, qn).astype(BF16)
        kk = kn_ref[0:kv, :]
        row_chunk = (q0 + lax.broadcasted_iota(jnp.int32, (tq, kv), 0)) >> 6
        col_chunk = lax.broadcasted_iota(jnp.int32, (tq, kv), 1) >> 6
        visible = col_chunk <= row_chunk

        def probs(qh):
            s = jnp.where(visible, _dot_nt(qh, kk), -jnp.inf)
            p = jnp.exp(s - jnp.max(s, axis=-1, keepdims=True))
            return p * (1.0 / jnp.sum(p, axis=-1, keepdims=True))

        attn = (probs(q1) - lam_full * probs(q2)).astype(BF16)
        o = _dot(attn, v_ref[0:kv, :])
        ms = jnp.mean(o * o, axis=-1, keepdims=True)
        o = o * lax.rsqrt(ms + EPS) * subln_ref[...] * (1.0 - lambda_init)
        o_ref[q0:q0 + tq, :] = o.astype(o_ref.dtype)


def _diff_attention(u, gain, lam, subln, l, lambda_init, batch, seq, tq=256):
    t = batch * seq
    body = functools.partial(_diff_attn_kernel, lambda_init, seq, tq)
    return pl.pallas_call(
        body,
        grid=(batch, 4),
        in_specs=[pl.BlockSpec((seq, LANES), lambda b, h: (b, h)),
                  pl.BlockSpec((seq, LANES), lambda b, h: (b, 4 + h)),
                  pl.BlockSpec((seq, LANES), lambda b, h: (b, 8 + h)),
                  pl.BlockSpec((None, 2, LANES), lambda b, h: (l, 0, 0)),
                  pl.BlockSpec((None, 4, 64), lambda b, h: (l, 0, 0)),
                  pl.BlockSpec((None, 1, LANES), lambda b, h: (l, 0, 0))],
        out_specs=pl.BlockSpec((seq, LANES), lambda b, h: (b, h)),
        out_shape=jax.ShapeDtypeStruct((t, BRANCH), BF16),
        scratch_shapes=[pltpu.VMEM((seq, LANES), BF16)],
        compiler_params=_params("arbitrary", "arbitrary"),
        name="diff_attention",
    )(u, u, u, gain, lam, subln)


def _gla_kernel(seq, q_ref, k_ref, v_ref, og_ref, small_ref, w2_ref, gkb_ref, nw_ref, o_ref,
                state_ref):
    cs, nb = GLA_CHUNK, GLA_CHUNK // GLA_BLOCK
    ri = lax.broadcasted_iota(jnp.int32, (cs, cs), 0)
    ci = lax.broadcasted_iota(jnp.int32, (cs, cs), 1)
    tri = (ci <= ri).astype(BF16)
    row = lax.broadcasted_iota(jnp.int32, (cs, LANES), 0)
    row_blk = row >> 3
    row_mod = row & 7
    sub3 = lax.broadcasted_iota(jnp.int32, (nb, GLA_BLOCK, LANES), 1)
    lane = lax.broadcasted_iota(jnp.int32, (1, LANES), 1)
    head0 = lane < 64
    r2 = lax.broadcasted_iota(jnp.int32, (2 * cs, cs), 0)
    c2 = lax.broadcasted_iota(jnp.int32, (2 * cs, cs), 1)
    same_blk = ((r2 & (cs - 1)) >> 3) == (c2 >> 3)
    sr = lax.broadcasted_iota(jnp.int32, (2 * LANES, LANES), 0)
    sc = lax.broadcasted_iota(jnp.int32, (2 * LANES, LANES), 1)
    state_mask = (sr >> 7) == (sc >> 6)
    state_ref[...] = jnp.zeros_like(state_ref)

    def chunk(c, carry):
        r0 = pl.multiple_of(c * cs, cs)
        rows = pl.ds(r0, cs)
        z = _dot(small_ref[rows, :].astype(BF16), w2_ref[...]) + gkb_ref[...]
        gk_all = (jnp.minimum(z, 0.0) - jnp.log1p(jnp.exp(-jnp.abs(z)))) * (1.0 / 16.0)
        for p in range(2):
            lanes = slice(p * LANES, (p + 1) * LANES)
            q = q_ref[rows, lanes].astype(F32) * (64 ** -0.5)
            k = k_ref[rows, lanes].astype(F32)
            cum = _select_dot(tri, gk_all[:, lanes])
            cum3 = cum.reshape(nb, GLA_BLOCK, LANES)
            q3 = q.reshape(nb, GLA_BLOCK, LANES)
            last3 = cum3[:, GLA_BLOCK - 1:GLA_BLOCK, :]
            ref3 = jnp.concatenate([jnp.zeros((1, 1, LANES), F32), last3[:nb - 1]], axis=0)
            refrow = jnp.broadcast_to(ref3, (nb, GLA_BLOCK, LANES)).reshape(cs, LANES)
            qp = q * jnp.exp(cum - refrow)
            lhs_off, rhs_off = [], []
            for blk in range(1, nb):
                lhs_off.append(jnp.where(row_blk == blk, qp, 0.0))
                n = blk * GLA_BLOCK
                kd = k[0:n] * jnp.exp(cum[n - 1:n, :] - cum[0:n])
                rhs_off.append(jnp.concatenate([kd, jnp.zeros((cs - n, LANES), F32)], axis=0))
            lhs_off = jnp.concatenate(lhs_off, axis=1)
            rhs_off = jnp.concatenate(rhs_off, axis=1).astype(BF16)
            lhs_dg, rhs_dg = [], []
            for j in range(GLA_BLOCK):
                dec = jnp.exp(jnp.minimum(cum3 - cum3[:, j:j + 1, :], 0.0))
                lhs_dg.append(jnp.where(sub3 >= j, q3 * dec, 0.0).reshape(cs, LANES))
                rhs_dg.append(jnp.where(row_mod == j, k, 0.0))
            lhs_dg = jnp.concatenate(lhs_dg, axis=1)
            rhs_dg = jnp.concatenate(rhs_dg, axis=1).astype(BF16)

            def both_heads(x):
                m0 = (lax.broadcasted_iota(jnp.int32, (1, x.shape[1]), 1) & 64) == 0
                return jnp.concatenate([jnp.where(m0, x, 0.0), jnp.where(m0, 0.0, x)],
                                       axis=0).astype(BF16)

            s_off = _dot_nt(both_heads(lhs_off), rhs_off)
            s_dg = _dot_nt(both_heads(lhs_dg), rhs_dg)
            a = (s_off + jnp.where(same_blk, s_dg, 0.0)).astype(BF16)
            st = state_ref[p]
            inter = _dot_nt((q * jnp.exp(cum)).astype(BF16), st.astype(BF16))
            cum_last = cum[cs - 1:cs, :]
            kdec = (k * jnp.exp(cum_last - cum)).astype(BF16)
            vv = v_ref[rows, p * 2 * LANES:(p + 1) * 2 * LANES]
            upd = _dot_tn(vv, kdec)
            state_ref[p] = jnp.where(state_mask, st * jnp.exp(cum_last) + upd, 0.0)
            for e in range(2):
                cols = slice((2 * p + e) * LANES, (2 * p + e + 1) * LANES)
                o = _dot(a[e * cs:(e + 1) * cs], v_ref[rows, cols]) + inter[:, e * LANES:(e + 1) * LANES]
                ms = jnp.mean(o * o, axis=-1, keepdims=True)
                o = o * lax.rsqrt(ms + EPS) * nw_ref[...]
                o_ref[rows, cols] = (o * _silu(og_ref[rows, cols].astype(F32))).astype(o_ref.dtype)
        return carry

    lax.fori_loop(0, seq // cs, chunk, 0)


def _gla(u, small, w2p, gkb, nw, l, batch, seq):
    t = batch * seq
    body = functools.partial(_gla_kernel, seq)
    return pl.pallas_call(
        body,
        grid=(batch,),
        in_specs=[pl.BlockSpec((seq, 256), lambda b: (b, 6)),
                  pl.BlockSpec((seq, 256), lambda b: (b, 7)),
                  pl.BlockSpec((seq, 512), lambda b: (b, 4)),
                  pl.BlockSpec((seq, 512), lambda b: (b, 5)),
                  pl.BlockSpec((seq, U_SMALL), lambda b: (b, 0)),
                  pl.BlockSpec((None, U_SMALL, 256), lambda b: (l, 0, 0)),
                  pl.BlockSpec((None, 1, 256), lambda b: (l, 0, 0)),
                  pl.BlockSpec((None, 1, LANES), lambda b: (l, 0, 0))],
        out_specs=pl.BlockSpec((seq, BRANCH), lambda b: (b, 0)),
        out_shape=jax.ShapeDtypeStruct((t, BRANCH), BF16),
        scratch_shapes=[pltpu.VMEM((2, 2 * LANES, LANES), F32)],
        compiler_params=_params("arbitrary"),
        name="gla",
    )(u, u, u, u, small, w2p, gkb, nw)


def _conformer_kernel(seq, tr, a_ref, g_ref, w_ref, b_ref, lnw_ref, lnb_ref, o_ref, gbuf, cbuf):
    halo = 32
    gbuf[0:halo, :] = jnp.zeros((halo, BRANCH), F32)
    for r in range(0, seq, 256):
        gbuf[halo + r:halo + r + 256, :] = (a_ref[r:r + 256, :].astype(F32)
                                            * _sigmoid(g_ref[r:r + 256, :].astype(F32)))

    def conv_tile(i, carry):
        r0 = pl.multiple_of(i * tr, tr)
        for c0 in range(0, BRANCH, LANES):
            win = gbuf[pl.ds(r0, tr + halo), c0:c0 + LANES]
            acc = jnp.broadcast_to(b_ref[:, c0:c0 + LANES], (tr, LANES))
            for j in range(CONV_WIDTH):
                acc = acc + w_ref[j:j + 1, c0:c0 + LANES] * _shift_window(
                    win, halo - (CONV_WIDTH - 1) + j, tr)
            cbuf[pl.ds(r0, tr), c0:c0 + LANES] = acc
        return carry

    lax.fori_loop(0, seq // tr, conv_tile, 0)

    def norm_tile(i, carry):
        r0 = pl.multiple_of(i * tr, tr)
        c = cbuf[pl.ds(r0, tr), :]
        mu = jnp.mean(c, axis=-1, keepdims=True)
        d = c - mu
        var = jnp.mean(d * d, axis=-1, keepdims=True)
        y = d * lax.rsqrt(var + EPS) * lnw_ref[...] + lnb_ref[...]
        o_ref[pl.ds(r0, tr), :] = _silu(y).astype(o_ref.dtype)
        return carry

    lax.fori_loop(0, seq // tr, norm_tile, 0)


def _conformer(u, w, b, lnw, lnb, l, batch, seq, tr=128):
    t = batch * seq
    body = functools.partial(_conformer_kernel, seq, tr)
    return pl.pallas_call(
        body,
        grid=(batch,),
        in_specs=[pl.BlockSpec((seq, BRANCH), lambda i: (i, 6)),
                  pl.BlockSpec((seq, BRANCH), lambda i: (i, 7)),
                  pl.BlockSpec((None, CONV_WIDTH, BRANCH), lambda i: (l, 0, 0)),
                  pl.BlockSpec((None, 1, BRANCH), lambda i: (l, 0, 0)),
                  pl.BlockSpec((None, 1, BRANCH), lambda i: (l, 0, 0)),
                  pl.BlockSpec((None, 1, BRANCH), lambda i: (l, 0, 0))],
        out_specs=pl.BlockSpec((seq, BRANCH), lambda i: (i, 0)),
        out_shape=jax.ShapeDtypeStruct((t, BRANCH), BF16),
        scratch_shapes=[pltpu.VMEM((seq + 32, BRANCH), F32), pltpu.VMEM((seq, BRANCH), F32)],
        compiler_params=_params("arbitrary"),
        name="conformer_conv",
    )(u, u, w, b, lnw, lnb)


def _ssd_kernel(seq, z_ref, x_ref, bc_ref, small_ref, cw_ref, cb_ref, dtb_ref, alog_ref, dsk_ref,
                nw_ref, o_ref, xbuf, state_ref):
    cs = SSD_CHUNK
    halo = SUBLANES
    xbuf[0:halo, :] = jnp.zeros((halo, 2 * BRANCH), F32)
    for r in range(0, seq, 256):
        xbuf[halo + r:halo + r + 256, 0:BRANCH] = x_ref[r:r + 256, :].astype(F32)
        xbuf[halo + r:halo + r + 256, BRANCH:2 * BRANCH] = bc_ref[r:r + 256, :].astype(F32)
    state_ref[...] = jnp.zeros_like(state_ref)

    ri = lax.broadcasted_iota(jnp.int32, (cs, cs), 0)
    ci = lax.broadcasted_iota(jnp.int32, (cs, cs), 1)
    causal = ci <= ri
    tri = causal.astype(BF16)
    er = lax.broadcasted_iota(jnp.int32, (LANES, BRANCH), 0)
    ec = lax.broadcasted_iota(jnp.int32, (LANES, BRANCH), 1)
    expand = ((ec >> 6) == er).astype(BF16)
    lane = lax.broadcasted_iota(jnp.int32, (1, LANES), 1)
    head0 = lane < 64
    a_neg = -jnp.exp(alog_ref[...])

    def chunk(c, carry):
        r0 = pl.multiple_of(c * cs, cs)
        rows = pl.ds(r0, cs)
        win = xbuf[pl.ds(r0, cs + halo), :]
        conv = jnp.broadcast_to(cb_ref[...], (cs, 2 * BRANCH))
        for j in range(SSD_CONV):
            conv = conv + cw_ref[j:j + 1, :] * _shift_window(win, halo - (SSD_CONV - 1) + j, cs)
        xc = _silu(conv)
        xs = xc[:, 0:BRANCH]
        dt = _softplus(small_ref[rows, :] + dtb_ref[...])
        cum = _select_dot(tri, dt * a_neg)
        cum_t = cum.T
        ecum = jnp.exp(cum)
        cum_last = cum[cs - 1:cs, :]
        dt_x = _dot_select(dt, expand)
        ecum_x = _dot_select(ecum, expand)
        dte_x = _dot_select(jnp.exp(cum_last - cum), expand)
        xdt = xs * dt_x
        ys = []
        for g in range(2):
            bm = xc[:, BRANCH + g * LANES:BRANCH + (g + 1) * LANES].astype(BF16)
            cm = xc[:, BRANCH + 256 + g * LANES:BRANCH + 256 + (g + 1) * LANES].astype(BF16)
            cb = _dot_nt(cm, bm)
            gl = slice(g * 256, (g + 1) * 256)
            for pr in range(2):
                pl_ = slice(g * 256 + pr * LANES, g * 256 + (pr + 1) * LANES)
                xp = xdt[:, pl_]
                y = None
                for e in range(2):
                    h = g * 4 + pr * 2 + e
                    seg = jnp.where(causal, cum[:, h:h + 1] - cum_t[h:h + 1, :], -jnp.inf)
                    m = (cb * jnp.exp(seg)).astype(BF16)
                    xh = jnp.where(head0 if e == 0 else ~head0, xp, 0.0).astype(BF16)
                    t = _dot(m, xh)
                    y = t if y is None else y + t
                ys.append(y)
            st = state_ref[g]
            y_off = _dot(cm, st.astype(BF16)) * ecum_x[:, gl]
            ys[2 * g] = ys[2 * g] + y_off[:, 0:LANES]
            ys[2 * g + 1] = ys[2 * g + 1] + y_off[:, LANES:2 * LANES]
            upd = _dot_tn(bm, (xdt[:, gl] * dte_x[:, gl]).astype(BF16))
            state_ref[g] = st * ecum_x[cs - 1:cs, gl] + upd
        y = jnp.concatenate(ys, axis=1) + dsk_ref[...] * xs
        y = y * _silu(z_ref[rows, :].astype(F32))
        outs = []
        for g in range(2):
            yg = y[:, g * 256:(g + 1) * 256]
            ms = jnp.mean(yg * yg, axis=-1, keepdims=True)
            outs.append(yg * lax.rsqrt(ms + EPS))
        o_ref[rows, :] = (jnp.concatenate(outs, axis=1) * nw_ref[...]).astype(o_ref.dtype)
        return carry

    lax.fori_loop(0, seq // cs, chunk, 0)


def _ssd(u, small, cw, cb, dtb, alog, dsk, nw, l, batch, seq):
    t = batch * seq
    body = functools.partial(_ssd_kernel, seq)
    vec = lambda n: pl.BlockSpec((None, 1, n), lambda i: (l, 0, 0))
    return pl.pallas_call(
        body,
        grid=(batch,),
        in_specs=[pl.BlockSpec((seq, BRANCH), lambda i: (i, 8)),
                  pl.BlockSpec((seq, BRANCH), lambda i: (i, 9)),
                  pl.BlockSpec((seq, BRANCH), lambda i: (i, 10)),
                  pl.BlockSpec((seq, U_SMALL), lambda i: (i, 0)),
                  pl.BlockSpec((None, SSD_CONV, 2 * BRANCH), lambda i: (l, 0, 0)),
                  vec(2 * BRANCH), vec(LANES), vec(LANES), vec(BRANCH), vec(BRANCH)],
        out_specs=pl.BlockSpec((seq, BRANCH), lambda i: (i, 0)),
        out_shape=jax.ShapeDtypeStruct((t, BRANCH), BF16),
        scratch_shapes=[pltpu.VMEM((seq + SUBLANES, 2 * BRANCH), F32),
                        pltpu.VMEM((2, LANES, 256), F32)],
        compiler_params=_params("arbitrary"),
        name="ssd",
    )(u, u, u, small, cw, cb, dtb, alog, dsk, nw)


def _merge_kernel(h_ref, ya_ref, yb_ref, yc_ref, yd_ref, wg_ref, wb_ref, bg_ref, o_ref):
    h = h_ref[...]
    acc = None
    for i, y_ref in enumerate((ya_ref, yb_ref, yc_ref, yd_ref)):
        gate = _sigmoid(_dot(h, wg_ref[i]) + bg_ref[i:i + 1, :])
        term = gate * _dot(y_ref[...], wb_ref[i])
        acc = term if acc is None else acc + term
    o_ref[...] = acc.astype(o_ref.dtype)


def _merge(h, ys, wg, wb, bg, l, tm=512, tn=512):
    m, d = h.shape
    yspec = pl.BlockSpec((tm, BRANCH), lambda j, i: (i, 0))
    return pl.pallas_call(
        _merge_kernel,
        grid=(d // tn, m // tm),
        in_specs=[pl.BlockSpec((tm, d), lambda j, i: (i, 0)), yspec, yspec, yspec, yspec,
                  pl.BlockSpec((None, 4, d, tn), lambda j, i: (l, 0, 0, j)),
                  pl.BlockSpec((None, 4, BRANCH, tn), lambda j, i: (l, 0, 0, j)),
                  pl.BlockSpec((None, 4, tn), lambda j, i: (l, 0, j))],
        out_specs=pl.BlockSpec((tm, tn), lambda j, i: (i, j)),
        out_shape=jax.ShapeDtypeStruct((m, d), BF16),
        compiler_params=_params("arbitrary", "arbitrary"),
        name="gated_merge",
    )(h, *ys, wg, wb, bg)


def _ffn_up_kernel(tiles_per_seq, tm, h_ref, wg_ref, wv_ref, cwg_ref, cwv_ref, cbg_ref, cbv_ref,
                   o_ref, gbuf, vbuf):
    halo = SUBLANES

    @pl.when(pl.program_id(1) % tiles_per_seq == 0)
    def _():
        gbuf[0:halo, :] = jnp.zeros((halo, gbuf.shape[1]), F32)
        vbuf[0:halo, :] = jnp.zeros((halo, vbuf.shape[1]), F32)

    h = h_ref[...]

    def conv(buf, w_ref, cw_ref, cb_ref):
        buf[halo:halo + tm, :] = _dot(h, w_ref[...])
        win = buf[...]
        out = jnp.broadcast_to(cb_ref[...], (tm, win.shape[1]))
        for j in range(FFN_CONV):
            out = out + cw_ref[j:j + 1, :] * _shift_window(win, halo - (FFN_CONV - 1) + j, tm)
        buf[0:halo, :] = win[tm:tm + halo]
        return out

    gate = conv(gbuf, wg_ref, cwg_ref, cbg_ref)
    val = conv(vbuf, wv_ref, cwv_ref, cbv_ref)
    o_ref[...] = (_silu(gate) * val).astype(o_ref.dtype)


def _ffn_up(h, w_up, cw, cb, l, seq, tm=512, tn=512):
    m, d = h.shape
    nt = D_FF // tn
    body = functools.partial(_ffn_up_kernel, seq // tm, tm)
    return pl.pallas_call(
        body,
        grid=(nt, m // tm),
        in_specs=[pl.BlockSpec((tm, d), lambda j, i: (i, 0)),
                  pl.BlockSpec((None, d, tn), lambda j, i: (l, 0, j)),
                  pl.BlockSpec((None, d, tn), lambda j, i: (l, 0, j + nt)),
                  pl.BlockSpec((None, FFN_CONV, tn), lambda j, i: (l, 0, j)),
                  pl.BlockSpec((None, FFN_CONV, tn), lambda j, i: (l, 0, j + nt)),
                  pl.BlockSpec((None, 1, tn), lambda j, i: (l, 0, j)),
                  pl.BlockSpec((None, 1, tn), lambda j, i: (l, 0, j + nt))],
        out_specs=pl.BlockSpec((tm, tn), lambda j, i: (i, j)),
        out_shape=jax.ShapeDtypeStruct((m, D_FF), BF16),
        scratch_shapes=[pltpu.VMEM((tm + SUBLANES, tn), F32), pltpu.VMEM((tm + SUBLANES, tn), F32)],
        compiler_params=_params("arbitrary", "arbitrary"),
        name="ffn_up_conv_gate",
    )(h, w_up, w_up, cw, cw, cb, cb)


def kernel(x, mix_norm, w_in, diff_qk_norm, diff_lambda, diff_subln, gla_gk_w2, gla_gk_b, gla_norm, conv_dw_w, conv_dw_b, conv_ln_w, conv_ln_b, ssd_conv_w, ssd_conv_b, ssd_dt_bias, ssd_a_log, ssd_d, ssd_norm, w_branch, w_gate, b_gate, w_out, ffn_norm, ffn_w_up, ffn_conv_w, ffn_conv_b, ffn_w_down):
    batch, seq, d = x.shape
    depth = w_in.shape[0]
    t = batch * seq
    xf = x.reshape(t, d)

    w_in_big = jnp.concatenate([w_in[:, :, :3072], w_in[:, :, 3088:5648]], axis=-1).astype(BF16)
    w_in_small = jnp.concatenate(
        [w_in[:, :, 5648:5656], w_in[:, :, 3072:3088],
         jnp.zeros((depth, d, U_SMALL - 24), F32)], axis=-1).astype(BF16)
    w2_pad = jnp.zeros((depth, U_SMALL, 256), F32).at[:, 8:24, :].set(gla_gk_w2).astype(BF16)
    pad8 = lambda v: jnp.pad(v, ((0, 0), (0, LANES - 8)))[:, None, :]
    row = lambda v: v[:, None, :]
    qk_gain = jnp.tile(diff_qk_norm, (1, 1, 2))
    dskip = jnp.repeat(ssd_d, 64, axis=-1)[:, None, :]
    w_gate_b = w_gate.astype(BF16)
    w_branch_b = w_branch.astype(BF16)
    w_out_b = w_out.astype(BF16)
    w_up_b = ffn_w_up.astype(BF16)
    w_down_b = ffn_w_down.astype(BF16)

    for l in range(depth):
        lambda_init = 0.8 - 0.6 * math.exp(-0.3 * l)
        h = _rmsnorm(xf, row(mix_norm), l)
        u = _matmul(h, w_in_big, l, tm=1024, tn=512, out_dtype=BF16, name="in_proj")
        small = _matmul(h, w_in_small, l, tm=1024, tn=U_SMALL, out_dtype=F32, name="in_proj_small")
        ya = _diff_attention(u, qk_gain, diff_lambda, row(diff_subln), l, lambda_init, batch, seq)
        yb = _gla(u, small, w2_pad, row(gla_gk_b), row(gla_norm), l, batch, seq)
        yc = _conformer(u, conv_dw_w, row(conv_dw_b), row(conv_ln_w), row(conv_ln_b), l, batch, seq)
        yd = _ssd(u, small, ssd_conv_w, row(ssd_conv_b), pad8(ssd_dt_bias), pad8(ssd_a_log), dskip,
                  row(ssd_norm), l, batch, seq)
        merged = _merge(h, (ya, yb, yc, yd), w_gate_b, w_branch_b, b_gate, l)
        xf = _matmul(merged, w_out_b, l, tm=1024, tn=512, out_dtype=F32, residual=xf, name="out_proj")
        h2 = _rmsnorm(xf, row(ffn_norm), l)
        act = _ffn_up(h2, w_up_b, ffn_conv_w, row(ffn_conv_b), l, seq)
        xf = _matmul(act, w_down_b, l, tm=1024, tn=512, out_dtype=F32, residual=xf, name="ffn_down")
    return xf.reshape(batch, seq, d)
```

```python
import functools
import math

import jax
import jax.numpy as jnp
from jax import lax
from jax.experimental import pallas as pl
from jax.experimental.pallas import tpu as pltpu

F32 = jnp.float32
BF16 = jnp.bfloat16

EPS = 1e-6
LOG2E = 1.4426950408889634
D_MODEL = 2048
BRANCH = 512
D_FF = 5632
U_BIG = 5632
U_SMALL = 128
LANES = 128
SUBLANES = 8
VMEM_LIMIT = 56 * 1024 * 1024

GLA_CHUNK = 64
GLA_BLOCK = 8
SSD_CHUNK = 256
CONV_WIDTH = 31
SSD_CONV = 4
FFN_CONV = 3
FFN_ROWS = 256


def _params(*sem):
    return pltpu.CompilerParams(dimension_semantics=sem, vmem_limit_bytes=VMEM_LIMIT)


def _sigmoid(x):
    return 0.5 * jnp.tanh(0.5 * x) + 0.5


def _silu(x):
    return x * _sigmoid(x)


def _softplus(x):
    return jnp.maximum(x, 0.0) + jnp.log1p(jnp.exp(-jnp.abs(x)))


def _split3(x):
    hi = x.astype(BF16)
    r1 = x - hi.astype(F32)
    mid = r1.astype(BF16)
    lo = (r1 - mid.astype(F32)).astype(BF16)
    return hi, mid, lo


def _dot(a, b):
    return jnp.dot(a, b, preferred_element_type=F32)


def _dot_nt(a, b):
    return lax.dot_general(a, b, (((1,), (1,)), ((), ())), preferred_element_type=F32)


def _dot_tn(a, b):
    return lax.dot_general(a, b, (((0,), (0,)), ((), ())), preferred_element_type=F32)


def _select_dot(sel, x):
    hi, mid, lo = _split3(x)
    return _dot(sel, hi) + _dot(sel, mid) + _dot(sel, lo)


def _dot_select(x, sel):
    hi, mid, lo = _split3(x)
    return _dot(hi, sel) + _dot(mid, sel) + _dot(lo, sel)


def _shift_window(win, shift, rows):
    total = win.shape[0]
    if shift % SUBLANES == 0:
        return win[shift:shift + rows]
    rolled = pltpu.roll(win, total - (shift % SUBLANES), 0)
    base = shift - shift % SUBLANES
    return rolled[base:base + rows]


def _rmsnorm_kernel(x_ref, w_ref, o_ref):
    x = x_ref[...]
    ms = jnp.mean(x * x, axis=-1, keepdims=True)
    o_ref[...] = (x * lax.rsqrt(ms + EPS) * w_ref[...]).astype(o_ref.dtype)


def _rmsnorm(x, w3, l, tm=512):
    m, d = x.shape
    return pl.pallas_call(
        _rmsnorm_kernel,
        grid=(m // tm,),
        in_specs=[pl.BlockSpec((tm, d), lambda i: (i, 0)),
                  pl.BlockSpec((None, 1, d), lambda i: (l, 0, 0))],
        out_specs=pl.BlockSpec((tm, d), lambda i: (i, 0)),
        out_shape=jax.ShapeDtypeStruct((m, d), BF16),
        compiler_params=_params("arbitrary"),
        name="rmsnorm",
    )(x, w3)


def _matmul_kernel(a_ref, w_ref, o_ref):
    o_ref[...] = _dot(a_ref[...], w_ref[...]).astype(o_ref.dtype)


def _matmul_res_kernel(a_ref, w_ref, r_ref, o_ref):
    o_ref[...] = r_ref[...] + _dot(a_ref[...], w_ref[...])


def _matmul(a, w3, l, *, tm, tn, out_dtype, residual=None, name="matmul"):
    m, k = a.shape
    n = w3.shape[-1]
    in_specs = [pl.BlockSpec((tm, k), lambda j, i: (i, 0)),
                pl.BlockSpec((None, k, tn), lambda j, i: (l, 0, j))]
    args = [a, w3]
    body = _matmul_kernel
    if residual is not None:
        in_specs.append(pl.BlockSpec((tm, tn), lambda j, i: (i, j)))
        args.append(residual)
        body = _matmul_res_kernel
    return pl.pallas_call(
        body,
        grid=(n // tn, m // tm),
        in_specs=in_specs,
        out_specs=pl.BlockSpec((tm, tn), lambda j, i: (i, j)),
        out_shape=jax.ShapeDtypeStruct((m, n), out_dtype),
        compiler_params=_params("arbitrary", "arbitrary"),
        name=name,
    )(*args)


def _proj_res_norm_kernel(tm, a_ref, w_ref, r_ref, nw_ref, x_ref, h_ref):
    for r in range(0, tm, FFN_ROWS):
        rows = slice(r, r + FFN_ROWS)
        xn = r_ref[rows, :] + _dot(a_ref[rows, :], w_ref[...])
        x_ref[rows, :] = xn
        ms = jnp.mean(xn * xn, axis=-1, keepdims=True)
        h_ref[rows, :] = (xn * lax.rsqrt(ms + EPS) * nw_ref[...]).astype(h_ref.dtype)


def _proj_res_norm(a, w3, l, residual, nw3, tm=512, name="proj_res_norm"):
    m, k = a.shape
    n = w3.shape[-1]
    return pl.pallas_call(
        functools.partial(_proj_res_norm_kernel, tm),
        grid=(m // tm,),
        in_specs=[pl.BlockSpec((tm, k), lambda i: (i, 0)),
                  pl.BlockSpec((None, k, n), lambda i: (l, 0, 0)),
                  pl.BlockSpec((tm, n), lambda i: (i, 0)),
                  pl.BlockSpec((None, 1, n), lambda i: (l, 0, 0))],
        out_specs=[pl.BlockSpec((tm, n), lambda i: (i, 0)),
                   pl.BlockSpec((tm, n), lambda i: (i, 0))],
        out_shape=[jax.ShapeDtypeStruct((m, n), F32), jax.ShapeDtypeStruct((m, n), BF16)],
        compiler_params=_params("arbitrary"),
        name=name,
    )(a, w3, residual, nw3)


def _diff_attn_kernel(lambda_init, seq, tq, q_ref, k_ref, v_ref, gain_ref, lam_ref, subln_ref,
                      o_ref, kn_ref):
    lane = lax.broadcasted_iota(jnp.int32, (1, LANES), 1)
    first = lane < 64

    def halfnorm(t, g):
        sq = t * t
        s1 = jnp.sum(jnp.where(first, sq, 0.0), axis=-1, keepdims=True)
        s2 = jnp.sum(jnp.where(first, 0.0, sq), axis=-1, keepdims=True)
        ms = jnp.where(first, s1, s2) * (1.0 / 64)
        return t * lax.rsqrt(ms + EPS) * g

    kn_ref[...] = halfnorm(k_ref[...].astype(F32), gain_ref[1:2, :]).astype(BF16)
    lam = lam_ref[...]
    l1 = jnp.sum(lam[0:1] * lam[1:2], axis=-1, keepdims=True)
    l2 = jnp.sum(lam[2:3] * lam[3:4], axis=-1, keepdims=True)
    lam_full = jnp.exp(l1) - jnp.exp(l2) + lambda_init

    visible = ((lax.broadcasted_iota(jnp.int32, (tq, tq), 1) >> 6)
               <= (lax.broadcasted_iota(jnp.int32, (tq, tq), 0) >> 6))

    for i in range(seq // tq):
        q0 = i * tq
        qn = halfnorm(q_ref[q0:q0 + tq, :].astype(F32), gain_ref[0:1, :]) * (64 ** -0.5 * LOG2E)
        k_dg = kn_ref[q0:q0 + tq, :]
        v_dg = v_ref[q0:q0 + tq, :]

        def softmax_pv(qh):
            s_dg = jnp.where(visible, _dot_nt(qh, k_dg), -jnp.inf)
            m = jnp.max(s_dg, axis=-1, keepdims=True)
            if q0:
                s_off = _dot_nt(qh, kn_ref[0:q0, :])
                m = jnp.maximum(m, jnp.max(s_off, axis=-1, keepdims=True))
            p_dg = jnp.exp2(s_dg - m)
            den = jnp.sum(p_dg, axis=-1, keepdims=True)
            acc = _dot(p_dg.astype(BF16), v_dg)
            if q0:
                p_off = jnp.exp2(s_off - m)
                den = den + jnp.sum(p_off, axis=-1, keepdims=True)
                acc = acc + _dot(p_off.astype(BF16), v_ref[0:q0, :])
            return acc * (1.0 / den)

        o = (softmax_pv(jnp.where(first, qn, 0.0).astype(BF16))
             - lam_full * softmax_pv(jnp.where(first, 0.0, qn).astype(BF16)))
        ms = jnp.mean(o * o, axis=-1, keepdims=True)
        o = o * lax.rsqrt(ms + EPS) * subln_ref[...] * (1.0 - lambda_init)
        o_ref[q0:q0 + tq, :] = o.astype(o_ref.dtype)


def _diff_attention(u, gain, lam, subln, l, lambda_init, batch, seq, tq=256):
    t = batch * seq
    body = functools.partial(_diff_attn_kernel, lambda_init, seq, tq)
    return pl.pallas_call(
        body,
        grid=(batch, 4),
        in_specs=[pl.BlockSpec((seq, LANES), lambda b, h: (b, h)),
                  pl.BlockSpec((seq, LANES), lambda b, h: (b, 4 + h)),
                  pl.BlockSpec((seq, LANES), lambda b, h: (b, 8 + h)),
                  pl.BlockSpec((None, 2, LANES), lambda b, h: (l, 0, 0)),
                  pl.BlockSpec((None, 4, 64), lambda b, h: (l, 0, 0)),
                  pl.BlockSpec((None, 1, LANES), lambda b, h: (l, 0, 0))],
        out_specs=pl.BlockSpec((seq, LANES), lambda b, h: (b, h)),
        out_shape=jax.ShapeDtypeStruct((t, BRANCH), BF16),
        scratch_shapes=[pltpu.VMEM((seq, LANES), BF16)],
        compiler_params=_params("arbitrary", "arbitrary"),
        name="diff_attention",
    )(u, u, u, gain, lam, subln)


def _gla_kernel(seq, q_ref, k_ref, v_ref, og_ref, small_ref, w2_ref, gkb_ref, nw_ref, o_ref,
                state_ref):
    cs, nb = GLA_CHUNK, GLA_CHUNK // GLA_BLOCK
    ri = lax.broadcasted_iota(jnp.int32, (cs, cs), 0)
    ci = lax.broadcasted_iota(jnp.int32, (cs, cs), 1)
    tri = (ci <= ri).astype(BF16)
    row = lax.broadcasted_iota(jnp.int32, (cs, LANES), 0)
    row_blk = row >> 3
    row_mod = row & 7
    sub3 = lax.broadcasted_iota(jnp.int32, (nb, GLA_BLOCK, LANES), 1)
    lane = lax.broadcasted_iota(jnp.int32, (1, LANES), 1)
    head0 = lane < 64
    r2 = lax.broadcasted_iota(jnp.int32, (2 * cs, cs), 0)
    c2 = lax.broadcasted_iota(jnp.int32, (2 * cs, cs), 1)
    same_blk = ((r2 & (cs - 1)) >> 3) == (c2 >> 3)
    sr = lax.broadcasted_iota(jnp.int32, (2 * LANES, LANES), 0)
    sc = lax.broadcasted_iota(jnp.int32, (2 * LANES, LANES), 1)
    state_mask = (sr >> 7) == (sc >> 6)
    state_ref[...] = jnp.zeros_like(state_ref)

    def chunk(c, carry):
        r0 = pl.multiple_of(c * cs, cs)
        rows = pl.ds(r0, cs)
        z = _dot(small_ref[rows, :].astype(BF16), w2_ref[...]) + gkb_ref[...]
        gk_all = (jnp.minimum(z, 0.0) - jnp.log1p(jnp.exp(-jnp.abs(z)))) * (1.0 / 16.0)
        for p in range(2):
            lanes = slice(p * LANES, (p + 1) * LANES)
            q = q_ref[rows, lanes].astype(F32) * (64 ** -0.5)
            k = k_ref[rows, lanes].astype(F32)
            cum = _select_dot(tri, gk_all[:, lanes])
            cum3 = cum.reshape(nb, GLA_BLOCK, LANES)
            q3 = q.reshape(nb, GLA_BLOCK, LANES)
            last3 = cum3[:, GLA_BLOCK - 1:GLA_BLOCK, :]
            ref3 = jnp.concatenate([jnp.zeros((1, 1, LANES), F32), last3[:nb - 1]], axis=0)
            refrow = jnp.broadcast_to(ref3, (nb, GLA_BLOCK, LANES)).reshape(cs, LANES)
            qp = q * jnp.exp(cum - refrow)
            lhs_off, rhs_off = [], []
            for blk in range(1, nb):
                lhs_off.append(jnp.where(row_blk == blk, qp, 0.0))
                n = blk * GLA_BLOCK
                kd = k[0:n] * jnp.exp(cum[n - 1:n, :] - cum[0:n])
                rhs_off.append(jnp.concatenate([kd, jnp.zeros((cs - n, LANES), F32)], axis=0))
            lhs_off = jnp.concatenate(lhs_off, axis=1)
            rhs_off = jnp.concatenate(rhs_off, axis=1).astype(BF16)
            lhs_dg, rhs_dg = [], []
            for j in range(GLA_BLOCK):
                dec = jnp.exp(jnp.minimum(cum3 - cum3[:, j:j + 1, :], 0.0))
                lhs_dg.append(jnp.where(sub3 >= j, q3 * dec, 0.0).reshape(cs, LANES))
                rhs_dg.append(jnp.where(row_mod == j, k, 0.0))
            lhs_dg = jnp.concatenate(lhs_dg, axis=1)
            rhs_dg = jnp.concatenate(rhs_dg, axis=1).astype(BF16)

            def both_heads(x):
                m0 = (lax.broadcasted_iota(jnp.int32, (1, x.shape[1]), 1) & 64) == 0
                return jnp.concatenate([jnp.where(m0, x, 0.0), jnp.where(m0, 0.0, x)],
                                       axis=0).astype(BF16)

            s_off = _dot_nt(both_heads(lhs_off), rhs_off)
            s_dg = _dot_nt(both_heads(lhs_dg), rhs_dg)
            a = (s_off + jnp.where(same_blk, s_dg, 0.0)).astype(BF16)
            st = state_ref[p]
            inter = _dot_nt((q * jnp.exp(cum)).astype(BF16), st.astype(BF16))
            cum_last = cum[cs - 1:cs, :]
            kdec = (k * jnp.exp(cum_last - cum)).astype(BF16)
            vv = v_ref[rows, p * 2 * LANES:(p + 1) * 2 * LANES]
            upd = _dot_tn(vv, kdec)
            state_ref[p] = jnp.where(state_mask, st * jnp.exp(cum_last) + upd, 0.0)
            for e in range(2):
                cols = slice((2 * p + e) * LANES, (2 * p + e + 1) * LANES)
                o = _dot(a[e * cs:(e + 1) * cs], v_ref[rows, cols]) + inter[:, e * LANES:(e + 1) * LANES]
                ms = jnp.mean(o * o, axis=-1, keepdims=True)
                o = o * lax.rsqrt(ms + EPS) * nw_ref[...]
                o_ref[rows, cols] = (o * _silu(og_ref[rows, cols].astype(F32))).astype(o_ref.dtype)
        return carry

    lax.fori_loop(0, seq // cs, chunk, 0, unroll=2)


def _gla(u, small, w2p, gkb, nw, l, batch, seq):
    t = batch * seq
    body = functools.partial(_gla_kernel, seq)
    return pl.pallas_call(
        body,
        grid=(batch,),
        in_specs=[pl.BlockSpec((seq, 256), lambda b: (b, 6)),
                  pl.BlockSpec((seq, 256), lambda b: (b, 7)),
                  pl.BlockSpec((seq, 512), lambda b: (b, 4)),
                  pl.BlockSpec((seq, 512), lambda b: (b, 5)),
                  pl.BlockSpec((seq, U_SMALL), lambda b: (b, 0)),
                  pl.BlockSpec((None, U_SMALL, 256), lambda b: (l, 0, 0)),
                  pl.BlockSpec((None, 1, 256), lambda b: (l, 0, 0)),
                  pl.BlockSpec((None, 1, LANES), lambda b: (l, 0, 0))],
        out_specs=pl.BlockSpec((seq, BRANCH), lambda b: (b, 0)),
        out_shape=jax.ShapeDtypeStruct((t, BRANCH), BF16),
        scratch_shapes=[pltpu.VMEM((2, 2 * LANES, LANES), F32)],
        compiler_params=_params("arbitrary"),
        name="gla",
    )(u, u, u, u, small, w2p, gkb, nw)


def _conformer_kernel(seq, tr, a_ref, g_ref, w_ref, b_ref, lnw_ref, lnb_ref, o_ref, gbuf, cbuf):
    halo = 32
    gbuf[0:halo, :] = jnp.zeros((halo, BRANCH), F32)
    for r in range(0, seq, 256):
        gbuf[halo + r:halo + r + 256, :] = (a_ref[r:r + 256, :].astype(F32)
                                            * _sigmoid(g_ref[r:r + 256, :].astype(F32)))

    def conv_tile(i, carry):
        r0 = pl.multiple_of(i * tr, tr)
        for c0 in range(0, BRANCH, LANES):
            win = gbuf[pl.ds(r0, tr + halo), c0:c0 + LANES]
            acc = jnp.broadcast_to(b_ref[:, c0:c0 + LANES], (tr, LANES))
            for j in range(CONV_WIDTH):
                acc = acc + w_ref[j:j + 1, c0:c0 + LANES] * _shift_window(
                    win, halo - (CONV_WIDTH - 1) + j, tr)
            cbuf[pl.ds(r0, tr), c0:c0 + LANES] = acc
        return carry

    lax.fori_loop(0, seq // tr, conv_tile, 0)

    def norm_tile(i, carry):
        r0 = pl.multiple_of(i * tr, tr)
        c = cbuf[pl.ds(r0, tr), :]
        mu = jnp.mean(c, axis=-1, keepdims=True)
        d = c - mu
        var = jnp.mean(d * d, axis=-1, keepdims=True)
        y = d * lax.rsqrt(var + EPS) * lnw_ref[...] + lnb_ref[...]
        o_ref[pl.ds(r0, tr), :] = _silu(y).astype(o_ref.dtype)
        return carry

    lax.fori_loop(0, seq // tr, norm_tile, 0)


def _conformer(u, w, b, lnw, lnb, l, batch, seq, tr=128):
    t = batch * seq
    body = functools.partial(_conformer_kernel, seq, tr)
    return pl.pallas_call(
        body,
        grid=(batch,),
        in_specs=[pl.BlockSpec((seq, BRANCH), lambda i: (i, 6)),
                  pl.BlockSpec((seq, BRANCH), lambda i: (i, 7)),
                  pl.BlockSpec((None, CONV_WIDTH, BRANCH), lambda i: (l, 0, 0)),
                  pl.BlockSpec((None, 1, BRANCH), lambda i: (l, 0, 0)),
                  pl.BlockSpec((None, 1, BRANCH), lambda i: (l, 0, 0)),
                  pl.BlockSpec((None, 1, BRANCH), lambda i: (l, 0, 0))],
        out_specs=pl.BlockSpec((seq, BRANCH), lambda i: (i, 0)),
        out_shape=jax.ShapeDtypeStruct((t, BRANCH), BF16),
        scratch_shapes=[pltpu.VMEM((seq + 32, BRANCH), F32), pltpu.VMEM((seq, BRANCH), F32)],
        compiler_params=_params("arbitrary"),
        name="conformer_conv",
    )(u, u, w, b, lnw, lnb)


def _ssd_kernel(seq, z_ref, x_ref, bc_ref, small_ref, cw_ref, cb_ref, dtb_ref, alog_ref, dsk_ref,
                nw_ref, o_ref, xbuf, state_ref):
    cs = SSD_CHUNK
    halo = SUBLANES
    xbuf[0:halo, :] = jnp.zeros((halo, 2 * BRANCH), F32)
    for r in range(0, seq, 256):
        xbuf[halo + r:halo + r + 256, 0:BRANCH] = x_ref[r:r + 256, :].astype(F32)
        xbuf[halo + r:halo + r + 256, BRANCH:2 * BRANCH] = bc_ref[r:r + 256, :].astype(F32)
    state_ref[...] = jnp.zeros_like(state_ref)

    ri = lax.broadcasted_iota(jnp.int32, (cs, cs), 0)
    ci = lax.broadcasted_iota(jnp.int32, (cs, cs), 1)
    causal = ci <= ri
    tri = causal.astype(BF16)
    er = lax.broadcasted_iota(jnp.int32, (LANES, BRANCH), 0)
    ec = lax.broadcasted_iota(jnp.int32, (LANES, BRANCH), 1)
    expand = ((ec >> 6) == er).astype(BF16)
    lane = lax.broadcasted_iota(jnp.int32, (1, LANES), 1)
    head0 = lane < 64
    a_neg = -jnp.exp(alog_ref[...])

    def chunk(c, carry):
        r0 = pl.multiple_of(c * cs, cs)
        rows = pl.ds(r0, cs)
        win = xbuf[pl.ds(r0, cs + halo), :]
        conv = jnp.broadcast_to(cb_ref[...], (cs, 2 * BRANCH))
        for j in range(SSD_CONV):
            conv = conv + cw_ref[j:j + 1, :] * _shift_window(win, halo - (SSD_CONV - 1) + j, cs)
        xc = _silu(conv)
        xs = xc[:, 0:BRANCH]
        dt = _softplus(small_ref[rows, :] + dtb_ref[...])
        cum = _select_dot(tri, dt * a_neg)
        cum_t = cum.T
        ecum = jnp.exp(cum)
        cum_last = cum[cs - 1:cs, :]
        dt_x = _dot_select(dt, expand)
        ecum_x = _dot_select(ecum, expand)
        dte_x = _dot_select(jnp.exp(cum_last - cum), expand)
        xdt = xs * dt_x
        ys = []
        for g in range(2):
            bm = xc[:, BRANCH + g * LANES:BRANCH + (g + 1) * LANES].astype(BF16)
            cm = xc[:, BRANCH + 256 + g * LANES:BRANCH + 256 + (g + 1) * LANES].astype(BF16)
            cb = _dot_nt(cm, bm)
            gl = slice(g * 256, (g + 1) * 256)
            for pr in range(2):
                pl_ = slice(g * 256 + pr * LANES, g * 256 + (pr + 1) * LANES)
                xp = xdt[:, pl_]
                y = None
                for e in range(2):
                    h = g * 4 + pr * 2 + e
                    seg = jnp.where(causal, cum[:, h:h + 1] - cum_t[h:h + 1, :], -jnp.inf)
                    m = (cb * jnp.exp(seg)).astype(BF16)
                    xh = jnp.where(head0 if e == 0 else ~head0, xp, 0.0).astype(BF16)
                    t = _dot(m, xh)
                    y = t if y is None else y + t
                ys.append(y)
            st = state_ref[g]
            y_off = _dot(cm, st.astype(BF16)) * ecum_x[:, gl]
            ys[2 * g] = ys[2 * g] + y_off[:, 0:LANES]
            ys[2 * g + 1] = ys[2 * g + 1] + y_off[:, LANES:2 * LANES]
            upd = _dot_tn(bm, (xdt[:, gl] * dte_x[:, gl]).astype(BF16))
            state_ref[g] = st * ecum_x[cs - 1:cs, gl] + upd
        y = jnp.concatenate(ys, axis=1) + dsk_ref[...] * xs
        y = y * _silu(z_ref[rows, :].astype(F32))
        outs = []
        for g in range(2):
            yg = y[:, g * 256:(g + 1) * 256]
            ms = jnp.mean(yg * yg, axis=-1, keepdims=True)
            outs.append(yg * lax.rsqrt(ms + EPS))
        o_ref[rows, :] = (jnp.concatenate(outs, axis=1) * nw_ref[...]).astype(o_ref.dtype)
        return carry

    lax.fori_loop(0, seq // cs, chunk, 0)


def _ssd(u, small, cw, cb, dtb, alog, dsk, nw, l, batch, seq):
    t = batch * seq
    body = functools.partial(_ssd_kernel, seq)
    vec = lambda n: pl.BlockSpec((None, 1, n), lambda i: (l, 0, 0))
    return pl.pallas_call(
        body,
        grid=(batch,),
        in_specs=[pl.BlockSpec((seq, BRANCH), lambda i: (i, 8)),
                  pl.BlockSpec((seq, BRANCH), lambda i: (i, 9)),
                  pl.BlockSpec((seq, BRANCH), lambda i: (i, 10)),
                  pl.BlockSpec((seq, U_SMALL), lambda i: (i, 0)),
                  pl.BlockSpec((None, SSD_CONV, 2 * BRANCH), lambda i: (l, 0, 0)),
                  vec(2 * BRANCH), vec(LANES), vec(LANES), vec(BRANCH), vec(BRANCH)],
        out_specs=pl.BlockSpec((seq, BRANCH), lambda i: (i, 0)),
        out_shape=jax.ShapeDtypeStruct((t, BRANCH), BF16),
        scratch_shapes=[pltpu.VMEM((seq + SUBLANES, 2 * BRANCH), F32),
                        pltpu.VMEM((2, LANES, 256), F32)],
        compiler_params=_params("arbitrary"),
        name="ssd",
    )(u, u, u, small, cw, cb, dtb, alog, dsk, nw)


def _merge_kernel(h_ref, ya_ref, yb_ref, yc_ref, yd_ref, wg_ref, wb_ref, bg_ref, o_ref):
    h = h_ref[...]
    acc = None
    for i, y_ref in enumerate((ya_ref, yb_ref, yc_ref, yd_ref)):
        gate = _sigmoid(_dot(h, wg_ref[i]) + bg_ref[i:i + 1, :])
        term = gate * _dot(y_ref[...], wb_ref[i])
        acc = term if acc is None else acc + term
    o_ref[...] = acc.astype(o_ref.dtype)


def _merge(h, ys, wg, wb, bg, l, tm=512, tn=512):
    m, d = h.shape
    yspec = pl.BlockSpec((tm, BRANCH), lambda j, i: (i, 0))
    return pl.pallas_call(
        _merge_kernel,
        grid=(d // tn, m // tm),
        in_specs=[pl.BlockSpec((tm, d), lambda j, i: (i, 0)), yspec, yspec, yspec, yspec,
                  pl.BlockSpec((None, 4, d, tn), lambda j, i: (l, 0, 0, j)),
                  pl.BlockSpec((None, 4, BRANCH, tn), lambda j, i: (l, 0, 0, j)),
                  pl.BlockSpec((None, 4, tn), lambda j, i: (l, 0, j))],
        out_specs=pl.BlockSpec((tm, tn), lambda j, i: (i, j)),
        out_shape=jax.ShapeDtypeStruct((m, d), BF16),
        compiler_params=_params("arbitrary", "arbitrary"),
        name="gated_merge",
    )(h, *ys, wg, wb, bg)


def _ffn_up_kernel(tiles_per_seq, tm, h_ref, wg_ref, wv_ref, cwg_ref, cwv_ref, cbg_ref, cbv_ref,
                   o_ref, ghist, vhist):
    halo = SUBLANES
    sub = FFN_ROWS

    @pl.when(pl.program_id(1) % tiles_per_seq == 0)
    def _():
        ghist[...] = jnp.zeros_like(ghist)
        vhist[...] = jnp.zeros_like(vhist)

    def conv(u, hist, cw_ref, cb_ref):
        win = jnp.concatenate([hist, u], axis=0)
        out = jnp.broadcast_to(cb_ref[...], u.shape)
        for j in range(FFN_CONV):
            out = out + cw_ref[j:j + 1, :] * _shift_window(win, halo - (FFN_CONV - 1) + j, sub)
        return out, u[sub - halo:sub]

    gh, vh = ghist[...], vhist[...]
    for r in range(0, tm, sub):
        h = h_ref[r:r + sub, :]
        gate, gh = conv(_dot(h, wg_ref[...]), gh, cwg_ref, cbg_ref)
        val, vh = conv(_dot(h, wv_ref[...]), vh, cwv_ref, cbv_ref)
        o_ref[r:r + sub, :] = (_silu(gate) * val).astype(o_ref.dtype)
    ghist[...] = gh
    vhist[...] = vh


def _ffn_up(h, w_up, cw, cb, l, seq, tm=1024, tn=512):
    m, d = h.shape
    nt = D_FF // tn
    body = functools.partial(_ffn_up_kernel, seq // tm, tm)
    return pl.pallas_call(
        body,
        grid=(nt, m // tm),
        in_specs=[pl.BlockSpec((tm, d), lambda j, i: (i, 0)),
                  pl.BlockSpec((None, d, tn), lambda j, i: (l, 0, j)),
                  pl.BlockSpec((None, d, tn), lambda j, i: (l, 0, j + nt)),
                  pl.BlockSpec((None, FFN_CONV, tn), lambda j, i: (l, 0, j)),
                  pl.BlockSpec((None, FFN_CONV, tn), lambda j, i: (l, 0, j + nt)),
                  pl.BlockSpec((None, 1, tn), lambda j, i: (l, 0, j)),
                  pl.BlockSpec((None, 1, tn), lambda j, i: (l, 0, j + nt))],
        out_specs=pl.BlockSpec((tm, tn), lambda j, i: (i, j)),
        out_shape=jax.ShapeDtypeStruct((m, D_FF), BF16),
        scratch_shapes=[pltpu.VMEM((SUBLANES, tn), F32), pltpu.VMEM((SUBLANES, tn), F32)],
        compiler_params=_params("arbitrary", "arbitrary"),
        name="ffn_up_conv_gate",
    )(h, w_up, w_up, cw, cw, cb, cb)


def kernel(x, mix_norm, w_in, diff_qk_norm, diff_lambda, diff_subln, gla_gk_w2, gla_gk_b, gla_norm, conv_dw_w, conv_dw_b, conv_ln_w, conv_ln_b, ssd_conv_w, ssd_conv_b, ssd_dt_bias, ssd_a_log, ssd_d, ssd_norm, w_branch, w_gate, b_gate, w_out, ffn_norm, ffn_w_up, ffn_conv_w, ffn_conv_b, ffn_w_down):
    batch, seq, d = x.shape
    depth = w_in.shape[0]
    t = batch * seq
    xf = x.reshape(t, d)

    w_in_big = jnp.concatenate([w_in[:, :, :3072], w_in[:, :, 3088:5648]], axis=-1).astype(BF16)
    w_in_small = jnp.concatenate(
        [w_in[:, :, 5648:5656], w_in[:, :, 3072:3088],
         jnp.zeros((depth, d, U_SMALL - 24), F32)], axis=-1).astype(BF16)
    w2_pad = jnp.zeros((depth, U_SMALL, 256), F32).at[:, 8:24, :].set(gla_gk_w2).astype(BF16)
    pad8 = lambda v: jnp.pad(v, ((0, 0), (0, LANES - 8)))[:, None, :]
    row = lambda v: v[:, None, :]
    qk_gain = jnp.tile(diff_qk_norm, (1, 1, 2))
    dskip = jnp.repeat(ssd_d, 64, axis=-1)[:, None, :]
    w_gate_b = w_gate.astype(BF16)
    w_branch_b = w_branch.astype(BF16)
    w_out_b = w_out.astype(BF16)
    w_up_b = ffn_w_up.astype(BF16)
    w_down_b = ffn_w_down.astype(BF16)

    for l in range(depth):
        lambda_init = 0.8 - 0.6 * math.exp(-0.3 * l)
        h = _rmsnorm(xf, row(mix_norm), l)
        u = _matmul(h, w_in_big, l, tm=2048, tn=512, out_dtype=BF16, name="in_proj")
        small = _matmul(h, w_in_small, l, tm=1024, tn=U_SMALL, out_dtype=F32, name="in_proj_small")
        ya = _diff_attention(u, qk_gain, diff_lambda, row(diff_subln), l, lambda_init, batch, seq)
        yb = _gla(u, small, w2_pad, row(gla_gk_b), row(gla_norm), l, batch, seq)
        yc = _conformer(u, conv_dw_w, row(conv_dw_b), row(conv_ln_w), row(conv_ln_b), l, batch, seq)
        yd = _ssd(u, small, ssd_conv_w, row(ssd_conv_b), pad8(ssd_dt_bias), pad8(ssd_a_log), dskip,
                  row(ssd_norm), l, batch, seq)
        merged = _merge(h, (ya, yb, yc, yd), w_gate_b, w_branch_b, b_gate, l)
        xf, h2 = _proj_res_norm(merged, w_out_b, l, xf, row(ffn_norm), name="out_proj_norm")
        act = _ffn_up(h2, w_up_b, ffn_conv_w, row(ffn_conv_b), l, seq)
        xf = _matmul(act, w_down_b, l, tm=1024, tn=512, out_dtype=F32, residual=xf, name="ffn_down")
    return xf.reshape(batch, seq, d)
```

```python
import functools
import math

import jax
import jax.numpy as jnp
from jax import lax
from jax.experimental import pallas as pl
from jax.experimental.pallas import tpu as pltpu

F32 = jnp.float32
BF16 = jnp.bfloat16

EPS = 1e-6
LOG2E = 1.4426950408889634
D_MODEL = 2048
BRANCH = 512
D_FF = 5632
U_BIG = 5632
U_SMALL = 128
LANES = 128
SUBLANES = 8
VMEM_LIMIT = 56 * 1024 * 1024

GLA_CHUNK = 64
GLA_BLOCK = 8
SSD_CHUNK = 256
CONV_WIDTH = 31
SSD_CONV = 4
FFN_CONV = 3
FFN_ROWS = 256
CAST_ROWS = 256


def _params(*sem):
    return pltpu.CompilerParams(dimension_semantics=sem, vmem_limit_bytes=VMEM_LIMIT)


def _sigmoid(x):
    return 0.5 * jnp.tanh(0.5 * x) + 0.5


def _silu(x):
    half = 0.5 * x
    return half * jnp.tanh(half) + half


def _softplus(x):
    return jnp.maximum(x, 0.0) + jnp.log1p(jnp.exp(-jnp.abs(x)))


def _split3(x):
    hi = x.astype(BF16)
    r1 = x - hi.astype(F32)
    mid = r1.astype(BF16)
    lo = (r1 - mid.astype(F32)).astype(BF16)
    return hi, mid, lo


def _dot(a, b):
    return jnp.dot(a, b, preferred_element_type=F32)


def _dot_nt(a, b):
    return lax.dot_general(a, b, (((1,), (1,)), ((), ())), preferred_element_type=F32)


def _dot_tn(a, b):
    return lax.dot_general(a, b, (((0,), (0,)), ((), ())), preferred_element_type=F32)


def _select_dot(sel, x):
    hi, mid, lo = _split3(x)
    return _dot(sel, hi) + _dot(sel, mid) + _dot(sel, lo)


def _dot_select(x, sel):
    hi, mid, lo = _split3(x)
    return _dot(hi, sel) + _dot(mid, sel) + _dot(lo, sel)


def _shift_window(win, shift, rows):
    total = win.shape[0]
    if shift % SUBLANES == 0:
        return win[shift:shift + rows]
    rolled = pltpu.roll(win, total - (shift % SUBLANES), 0)
    base = shift - shift % SUBLANES
    return rolled[base:base + rows]


def _rmsnorm_kernel(x_ref, w_ref, o_ref):
    x = x_ref[...]
    ms = jnp.mean(x * x, axis=-1, keepdims=True)
    o_ref[...] = (x * lax.rsqrt(ms + EPS) * w_ref[...]).astype(o_ref.dtype)


def _rmsnorm(x, w3, l, tm=512):
    m, d = x.shape
    return pl.pallas_call(
        _rmsnorm_kernel,
        grid=(m // tm,),
        in_specs=[pl.BlockSpec((tm, d), lambda i: (i, 0)),
                  pl.BlockSpec((None, 1, d), lambda i: (l, 0, 0))],
        out_specs=pl.BlockSpec((tm, d), lambda i: (i, 0)),
        out_shape=jax.ShapeDtypeStruct((m, d), BF16),
        compiler_params=_params("arbitrary"),
        name="rmsnorm",
    )(x, w3)


def _matmul_kernel(a_ref, w_ref, o_ref):
    o_ref[...] = _dot(a_ref[...], w_ref[...]).astype(o_ref.dtype)


def _matmul_res_kernel(a_ref, w_ref, r_ref, o_ref):
    o_ref[...] = r_ref[...] + _dot(a_ref[...], w_ref[...])


def _cast_rows(dst_ref, src_ref):
    for r in range(0, src_ref.shape[0], CAST_ROWS):
        dst_ref[r:r + CAST_ROWS, :] = src_ref[r:r + CAST_ROWS, :].astype(BF16)


def _matmul_cast_res_kernel(a_ref, w_ref, r_ref, o_ref, w_scr):
    @pl.when(pl.program_id(1) == 0)
    def _():
        _cast_rows(w_scr, w_ref)

    o_ref[...] = r_ref[...] + _dot(a_ref[...], w_scr[...])


def _matmul(a, w3, l, *, tm, tn, out_dtype, residual=None, name="matmul"):
    m, k = a.shape
    n = w3.shape[-1]
    in_specs = [pl.BlockSpec((tm, k), lambda j, i: (i, 0)),
                pl.BlockSpec((None, k, tn), lambda j, i: (l, 0, j))]
    args = [a, w3]
    body = _matmul_kernel
    scratch = []
    if residual is not None:
        in_specs.append(pl.BlockSpec((tm, tn), lambda j, i: (i, j)))
        args.append(residual)
        body = _matmul_res_kernel
        if w3.dtype == F32:
            body = _matmul_cast_res_kernel
            scratch = [pltpu.VMEM((k, tn), BF16)]
    return pl.pallas_call(
        body,
        grid=(n // tn, m // tm),
        in_specs=in_specs,
        out_specs=pl.BlockSpec((tm, tn), lambda j, i: (i, j)),
        out_shape=jax.ShapeDtypeStruct((m, n), out_dtype),
        scratch_shapes=scratch,
        compiler_params=_params("arbitrary", "arbitrary"),
        name=name,
    )(*args)


IN_TN = 512
IN_SHIFT_TILE = 3072 // IN_TN
IN_SHIFT = 16


def _in_proj_kernel(a_ref, wa_ref, wb_ref, o_ref, w_scr):
    j = pl.program_id(0)
    first_row_tile = pl.program_id(1) == 0

    @pl.when(first_row_tile & (j < IN_SHIFT_TILE))
    def _():
        _cast_rows(w_scr, wa_ref)

    @pl.when(first_row_tile & (j >= IN_SHIFT_TILE))
    def _():
        for r in range(0, wa_ref.shape[0], CAST_ROWS):
            wide = jnp.concatenate([wa_ref[r:r + CAST_ROWS, :], wb_ref[r:r + CAST_ROWS, :]], axis=1)
            w_scr[r:r + CAST_ROWS, :] = wide[:, IN_SHIFT:IN_SHIFT + IN_TN].astype(BF16)

    o_ref[...] = _dot(a_ref[...], w_scr[...]).astype(o_ref.dtype)


def _in_proj(a, w_in, l, tm=2048):
    m, k = a.shape
    per = IN_TN // LANES
    return pl.pallas_call(
        _in_proj_kernel,
        grid=(U_BIG // IN_TN, m // tm),
        in_specs=[pl.BlockSpec((tm, k), lambda j, i: (i, 0)),
                  pl.BlockSpec((None, k, IN_TN), lambda j, i: (l, 0, j)),
                  pl.BlockSpec((None, k, LANES),
                               lambda j, i: (l, 0, per * (jnp.maximum(j, IN_SHIFT_TILE) + 1)))],
        out_specs=pl.BlockSpec((tm, IN_TN), lambda j, i: (i, j)),
        out_shape=jax.ShapeDtypeStruct((m, U_BIG), BF16),
        scratch_shapes=[pltpu.VMEM((k, IN_TN), BF16)],
        compiler_params=_params("arbitrary", "arbitrary"),
        name="in_proj",
    )(a, w_in, w_in)


def _proj_res_norm_kernel(tm, a_ref, w_ref, r_ref, nw_ref, x_ref, h_ref):
    for r in range(0, tm, FFN_ROWS):
        rows = slice(r, r + FFN_ROWS)
        xn = r_ref[rows, :] + _dot(a_ref[rows, :], w_ref[...])
        x_ref[rows, :] = xn
        ms = jnp.mean(xn * xn, axis=-1, keepdims=True)
        h_ref[rows, :] = (xn * lax.rsqrt(ms + EPS) * nw_ref[...]).astype(h_ref.dtype)


def _proj_res_norm(a, w3, l, residual, nw3, tm=512, name="proj_res_norm"):
    m, k = a.shape
    n = w3.shape[-1]
    return pl.pallas_call(
        functools.partial(_proj_res_norm_kernel, tm),
        grid=(m // tm,),
        in_specs=[pl.BlockSpec((tm, k), lambda i: (i, 0)),
                  pl.BlockSpec((None, k, n), lambda i: (l, 0, 0)),
                  pl.BlockSpec((tm, n), lambda i: (i, 0)),
                  pl.BlockSpec((None, 1, n), lambda i: (l, 0, 0))],
        out_specs=[pl.BlockSpec((tm, n), lambda i: (i, 0)),
                   pl.BlockSpec((tm, n), lambda i: (i, 0))],
        out_shape=[jax.ShapeDtypeStruct((m, n), F32), jax.ShapeDtypeStruct((m, n), BF16)],
        compiler_params=_params("arbitrary"),
        name=name,
    )(a, w3, residual, nw3)


def _diff_attn_kernel(lambda_init, seq, tq, q_ref, k_ref, v_ref, gain_ref, lam_ref, subln_ref,
                      o_ref, kn_ref):
    lane = lax.broadcasted_iota(jnp.int32, (1, LANES), 1)
    first = lane < 64

    def halfnorm(t, g):
        sq = t * t
        s1 = jnp.sum(jnp.where(first, sq, 0.0), axis=-1, keepdims=True)
        s2 = jnp.sum(jnp.where(first, 0.0, sq), axis=-1, keepdims=True)
        ms = jnp.where(first, s1, s2) * (1.0 / 64)
        return t * lax.rsqrt(ms + EPS) * g

    kn_ref[...] = halfnorm(k_ref[...].astype(F32), gain_ref[1:2, :]).astype(BF16)
    lam = lam_ref[...]
    l1 = jnp.sum(lam[0:1] * lam[1:2], axis=-1, keepdims=True)
    l2 = jnp.sum(lam[2:3] * lam[3:4], axis=-1, keepdims=True)
    lam_full = jnp.exp(l1) - jnp.exp(l2) + lambda_init

    visible = ((lax.broadcasted_iota(jnp.int32, (2 * tq, tq), 1) >> 6)
               <= ((lax.broadcasted_iota(jnp.int32, (2 * tq, tq), 0) & (tq - 1)) >> 6))

    for i in range(seq // tq):
        q0 = i * tq
        qn = halfnorm(q_ref[q0:q0 + tq, :].astype(F32), gain_ref[0:1, :]) * (64 ** -0.5 * LOG2E)
        qs = jnp.concatenate([jnp.where(first, qn, 0.0), jnp.where(first, 0.0, qn)],
                             axis=0).astype(BF16)
        s_dg = jnp.where(visible, _dot_nt(qs, kn_ref[q0:q0 + tq, :]), -jnp.inf)
        m = jnp.max(s_dg, axis=-1, keepdims=True)
        if q0:
            s_off = _dot_nt(qs, kn_ref[0:q0, :])
            m = jnp.maximum(m, jnp.max(s_off, axis=-1, keepdims=True))
        p_dg = jnp.exp2(s_dg - m)
        den = jnp.sum(p_dg, axis=-1, keepdims=True)
        acc = _dot(p_dg.astype(BF16), v_ref[q0:q0 + tq, :])
        if q0:
            p_off = jnp.exp2(s_off - m)
            den = den + jnp.sum(p_off, axis=-1, keepdims=True)
            acc = acc + _dot(p_off.astype(BF16), v_ref[0:q0, :])
        acc = acc * (1.0 / den)
        o = acc[0:tq] - lam_full * acc[tq:2 * tq]
        ms = jnp.mean(o * o, axis=-1, keepdims=True)
        o = o * lax.rsqrt(ms + EPS) * subln_ref[...] * (1.0 - lambda_init)
        o_ref[q0:q0 + tq, :] = o.astype(o_ref.dtype)


def _diff_attention(u, gain, lam, subln, l, lambda_init, batch, seq, tq=256):
    t = batch * seq
    body = functools.partial(_diff_attn_kernel, lambda_init, seq, tq)
    return pl.pallas_call(
        body,
        grid=(batch, 4),
        in_specs=[pl.BlockSpec((seq, LANES), lambda b, h: (b, h)),
                  pl.BlockSpec((seq, LANES), lambda b, h: (b, 4 + h)),
                  pl.BlockSpec((seq, LANES), lambda b, h: (b, 8 + h)),
                  pl.BlockSpec((None, 2, LANES), lambda b, h: (l, 0, 0)),
                  pl.BlockSpec((None, 4, 64), lambda b, h: (l, 0, 0)),
                  pl.BlockSpec((None, 1, LANES), lambda b, h: (l, 0, 0))],
        out_specs=pl.BlockSpec((seq, LANES), lambda b, h: (b, h)),
        out_shape=jax.ShapeDtypeStruct((t, BRANCH), BF16),
        scratch_shapes=[pltpu.VMEM((seq, LANES), BF16)],
        compiler_params=_params("arbitrary", "arbitrary"),
        name="diff_attention",
    )(u, u, u, gain, lam, subln)


def _gla_kernel(seq, q_ref, k_ref, v_ref, og_ref, small_ref, w2_ref, gkb_ref, nw_ref, o_ref,
                state_ref):
    cs, nb = GLA_CHUNK, GLA_CHUNK // GLA_BLOCK
    ri = lax.broadcasted_iota(jnp.int32, (cs, cs), 0)
    ci = lax.broadcasted_iota(jnp.int32, (cs, cs), 1)
    tri = (ci <= ri).astype(BF16)
    row = lax.broadcasted_iota(jnp.int32, (cs, LANES), 0)
    row_blk = row >> 3
    row_mod = row & 7
    sub3 = lax.broadcasted_iota(jnp.int32, (nb, GLA_BLOCK, LANES), 1)
    lane = lax.broadcasted_iota(jnp.int32, (1, LANES), 1)
    head0 = lane < 64
    r2 = lax.broadcasted_iota(jnp.int32, (2 * cs, cs), 0)
    c2 = lax.broadcasted_iota(jnp.int32, (2 * cs, cs), 1)
    same_blk = ((r2 & (cs - 1)) >> 3) == (c2 >> 3)
    sr = lax.broadcasted_iota(jnp.int32, (2 * LANES, LANES), 0)
    sc = lax.broadcasted_iota(jnp.int32, (2 * LANES, LANES), 1)
    state_mask = (sr >> 7) == (sc >> 6)
    state_ref[...] = jnp.zeros_like(state_ref)

    def chunk(c, carry):
        r0 = pl.multiple_of(c * cs, cs)
        rows = pl.ds(r0, cs)
        z = _dot(small_ref[rows, :].astype(BF16), w2_ref[...]) + gkb_ref[...]
        gk_all = (jnp.minimum(z, 0.0) - jnp.log1p(jnp.exp(-jnp.abs(z)))) * (1.0 / 16.0)
        for p in range(2):
            lanes = slice(p * LANES, (p + 1) * LANES)
            q = q_ref[rows, lanes].astype(F32) * (64 ** -0.5)
            k = k_ref[rows, lanes].astype(F32)
            cum = _select_dot(tri, gk_all[:, lanes])
            cum3 = cum.reshape(nb, GLA_BLOCK, LANES)
            q3 = q.reshape(nb, GLA_BLOCK, LANES)
            last3 = cum3[:, GLA_BLOCK - 1:GLA_BLOCK, :]
            ref3 = jnp.concatenate([jnp.zeros((1, 1, LANES), F32), last3[:nb - 1]], axis=0)
            refrow = jnp.broadcast_to(ref3, (nb, GLA_BLOCK, LANES)).reshape(cs, LANES)
            qp = q * jnp.exp(cum - refrow)
            lhs_off, rhs_off = [], []
            for blk in range(1, nb):
                lhs_off.append(jnp.where(row_blk == blk, qp, 0.0))
                n = blk * GLA_BLOCK
                kd = k[0:n] * jnp.exp(cum[n - 1:n, :] - cum[0:n])
                rhs_off.append(jnp.concatenate([kd, jnp.zeros((cs - n, LANES), F32)], axis=0))
            lhs_off = jnp.concatenate(lhs_off, axis=1)
            rhs_off = jnp.concatenate(rhs_off, axis=1).astype(BF16)
            lhs_dg, rhs_dg = [], []
            for j in range(GLA_BLOCK):
                dec = jnp.exp(jnp.minimum(cum3 - cum3[:, j:j + 1, :], 0.0))
                lhs_dg.append(jnp.where(sub3 >= j, q3 * dec, 0.0).reshape(cs, LANES))
                rhs_dg.append(jnp.where(row_mod == j, k, 0.0))
            lhs_dg = jnp.concatenate(lhs_dg, axis=1)
            rhs_dg = jnp.concatenate(rhs_dg, axis=1).astype(BF16)

            def both_heads(x):
                m0 = (lax.broadcasted_iota(jnp.int32, (1, x.shape[1]), 1) & 64) == 0
                return jnp.concatenate([jnp.where(m0, x, 0.0), jnp.where(m0, 0.0, x)],
                                       axis=0).astype(BF16)

            s_off = _dot_nt(both_heads(lhs_off), rhs_off)
            s_dg = _dot_nt(both_heads(lhs_dg), rhs_dg)
            a = (s_off + jnp.where(same_blk, s_dg, 0.0)).astype(BF16)
            st = state_ref[p]
            inter = _dot_nt((q * jnp.exp(cum)).astype(BF16), st.astype(BF16))
            cum_last = cum[cs - 1:cs, :]
            kdec = (k * jnp.exp(cum_last - cum)).astype(BF16)
            vv = v_ref[rows, p * 2 * LANES:(p + 1) * 2 * LANES]
            upd = _dot_tn(vv, kdec)
            state_ref[p] = jnp.where(state_mask, st * jnp.exp(cum_last) + upd, 0.0)
            for e in range(2):
                cols = slice((2 * p + e) * LANES, (2 * p + e + 1) * LANES)
                o = _dot(a[e * cs:(e + 1) * cs], v_ref[rows, cols]) + inter[:, e * LANES:(e + 1) * LANES]
                ms = jnp.mean(o * o, axis=-1, keepdims=True)
                o = o * lax.rsqrt(ms + EPS) * nw_ref[...]
                o_ref[rows, cols] = (o * _silu(og_ref[rows, cols].astype(F32))).astype(o_ref.dtype)
        return carry

    lax.fori_loop(0, seq // cs, chunk, 0, unroll=2)


def _gla(u, small, w2p, gkb, nw, l, batch, seq):
    t = batch * seq
    body = functools.partial(_gla_kernel, seq)
    return pl.pallas_call(
        body,
        grid=(batch,),
        in_specs=[pl.BlockSpec((seq, 256), lambda b: (b, 6)),
                  pl.BlockSpec((seq, 256), lambda b: (b, 7)),
                  pl.BlockSpec((seq, 512), lambda b: (b, 4)),
                  pl.BlockSpec((seq, 512), lambda b: (b, 5)),
                  pl.BlockSpec((seq, U_SMALL), lambda b: (b, 0)),
                  pl.BlockSpec((None, U_SMALL, 256), lambda b: (l, 0, 0)),
                  pl.BlockSpec((None, 1, 256), lambda b: (l, 0, 0)),
                  pl.BlockSpec((None, 1, LANES), lambda b: (l, 0, 0))],
        out_specs=pl.BlockSpec((seq, BRANCH), lambda b: (b, 0)),
        out_shape=jax.ShapeDtypeStruct((t, BRANCH), BF16),
        scratch_shapes=[pltpu.VMEM((2, 2 * LANES, LANES), F32)],
        compiler_params=_params("arbitrary"),
        name="gla",
    )(u, u, u, u, small, w2p, gkb, nw)


def _conformer_kernel(seq, tr, a_ref, g_ref, w_ref, b_ref, lnw_ref, lnb_ref, o_ref, gbuf, cbuf):
    halo = 32
    gbuf[0:halo, :] = jnp.zeros((halo, BRANCH), F32)
    for r in range(0, seq, 256):
        gbuf[halo + r:halo + r + 256, :] = (a_ref[r:r + 256, :].astype(F32)
                                            * _sigmoid(g_ref[r:r + 256, :].astype(F32)))

    def conv_tile(i, carry):
        r0 = pl.multiple_of(i * tr, tr)
        for c0 in range(0, BRANCH, LANES):
            win = gbuf[pl.ds(r0, tr + halo), c0:c0 + LANES]
            acc = jnp.broadcast_to(b_ref[:, c0:c0 + LANES], (tr, LANES))
            for j in range(CONV_WIDTH):
                acc = acc + w_ref[j:j + 1, c0:c0 + LANES] * _shift_window(
                    win, halo - (CONV_WIDTH - 1) + j, tr)
            cbuf[pl.ds(r0, tr), c0:c0 + LANES] = acc
        return carry

    lax.fori_loop(0, seq // tr, conv_tile, 0)

    def norm_tile(i, carry):
        r0 = pl.multiple_of(i * tr, tr)
        c = cbuf[pl.ds(r0, tr), :]
        mu = jnp.mean(c, axis=-1, keepdims=True)
        d = c - mu
        var = jnp.mean(d * d, axis=-1, keepdims=True)
        y = d * lax.rsqrt(var + EPS) * lnw_ref[...] + lnb_ref[...]
        o_ref[pl.ds(r0, tr), :] = _silu(y).astype(o_ref.dtype)
        return carry

    lax.fori_loop(0, seq // tr, norm_tile, 0)


def _conformer(u, w, b, lnw, lnb, l, batch, seq, tr=128):
    t = batch * seq
    body = functools.partial(_conformer_kernel, seq, tr)
    return pl.pallas_call(
        body,
        grid=(batch,),
        in_specs=[pl.BlockSpec((seq, BRANCH), lambda i: (i, 6)),
                  pl.BlockSpec((seq, BRANCH), lambda i: (i, 7)),
                  pl.BlockSpec((None, CONV_WIDTH, BRANCH), lambda i: (l, 0, 0)),
                  pl.BlockSpec((None, 1, BRANCH), lambda i: (l, 0, 0)),
                  pl.BlockSpec((None, 1, BRANCH), lambda i: (l, 0, 0)),
                  pl.BlockSpec((None, 1, BRANCH), lambda i: (l, 0, 0))],
        out_specs=pl.BlockSpec((seq, BRANCH), lambda i: (i, 0)),
        out_shape=jax.ShapeDtypeStruct((t, BRANCH), BF16),
        scratch_shapes=[pltpu.VMEM((seq + 32, BRANCH), F32), pltpu.VMEM((seq, BRANCH), F32)],
        compiler_params=_params("arbitrary"),
        name="conformer_conv",
    )(u, u, w, b, lnw, lnb)


def _ssd_kernel(seq, z_ref, x_ref, bc_ref, small_ref, cw_ref, cb_ref, dtb_ref, alog_ref, dsk_ref,
                nw_ref, o_ref, xbuf, state_ref):
    cs = SSD_CHUNK
    halo = SUBLANES
    xbuf[0:halo, :] = jnp.zeros((halo, 2 * BRANCH), F32)
    for r in range(0, seq, 256):
        xbuf[halo + r:halo + r + 256, 0:BRANCH] = x_ref[r:r + 256, :].astype(F32)
        xbuf[halo + r:halo + r + 256, BRANCH:2 * BRANCH] = bc_ref[r:r + 256, :].astype(F32)
    state_ref[...] = jnp.zeros_like(state_ref)

    ri = lax.broadcasted_iota(jnp.int32, (cs, cs), 0)
    ci = lax.broadcasted_iota(jnp.int32, (cs, cs), 1)
    causal = ci <= ri
    tri = causal.astype(BF16)
    er = lax.broadcasted_iota(jnp.int32, (LANES, BRANCH), 0)
    ec = lax.broadcasted_iota(jnp.int32, (LANES, BRANCH), 1)
    expand = ((ec >> 6) == er).astype(BF16)
    lane = lax.broadcasted_iota(jnp.int32, (1, LANES), 1)
    head0 = lane < 64
    a_neg = -jnp.exp(alog_ref[...])

    def chunk(c, carry):
        r0 = pl.multiple_of(c * cs, cs)
        rows = pl.ds(r0, cs)
        win = xbuf[pl.ds(r0, cs + halo), :]
        conv = jnp.broadcast_to(cb_ref[...], (cs, 2 * BRANCH))
        for j in range(SSD_CONV):
            conv = conv + cw_ref[j:j + 1, :] * _shift_window(win, halo - (SSD_CONV - 1) + j, cs)
        xc = _silu(conv)
        xs = xc[:, 0:BRANCH]
        dt = _softplus(small_ref[rows, :] + dtb_ref[...])
        cum = _select_dot(tri, dt * a_neg)
        cum_t = cum.T
        ecum = jnp.exp(cum)
        cum_last = cum[cs - 1:cs, :]
        dt_x = _dot_select(dt, expand)
        ecum_x = _dot_select(ecum, expand)
        dte_x = _dot_select(jnp.exp(cum_last - cum), expand)
        xdt = xs * dt_x
        ys = []
        for g in range(2):
            bm = xc[:, BRANCH + g * LANES:BRANCH + (g + 1) * LANES].astype(BF16)
            cm = xc[:, BRANCH + 256 + g * LANES:BRANCH + 256 + (g + 1) * LANES].astype(BF16)
            cb = _dot_nt(cm, bm)
            gl = slice(g * 256, (g + 1) * 256)
            for pr in range(2):
                pl_ = slice(g * 256 + pr * LANES, g * 256 + (pr + 1) * LANES)
                xp = xdt[:, pl_]
                y = None
                for e in range(2):
                    h = g * 4 + pr * 2 + e
                    seg = jnp.where(causal, cum[:, h:h + 1] - cum_t[h:h + 1, :], -jnp.inf)
                    m = (cb * jnp.exp(seg)).astype(BF16)
                    xh = jnp.where(head0 if e == 0 else ~head0, xp, 0.0).astype(BF16)
                    t = _dot(m, xh)
                    y = t if y is None else y + t
                ys.append(y)
            st = state_ref[g]
            y_off = _dot(cm, st.astype(BF16)) * ecum_x[:, gl]
            ys[2 * g] = ys[2 * g] + y_off[:, 0:LANES]
            ys[2 * g + 1] = ys[2 * g + 1] + y_off[:, LANES:2 * LANES]
            upd = _dot_tn(bm, (xdt[:, gl] * dte_x[:, gl]).astype(BF16))
            state_ref[g] = st * ecum_x[cs - 1:cs, gl] + upd
        y = jnp.concatenate(ys, axis=1) + dsk_ref[...] * xs
        y = y * _silu(z_ref[rows, :].astype(F32))
        outs = []
        for g in range(2):
            yg = y[:, g * 256:(g + 1) * 256]
            ms = jnp.mean(yg * yg, axis=-1, keepdims=True)
            outs.append(yg * lax.rsqrt(ms + EPS))
        o_ref[rows, :] = (jnp.concatenate(outs, axis=1) * nw_ref[...]).astype(o_ref.dtype)
        return carry

    lax.fori_loop(0, seq // cs, chunk, 0)


def _ssd(u, small, cw, cb, dtb, alog, dsk, nw, l, batch, seq):
    t = batch * seq
    body = functools.partial(_ssd_kernel, seq)
    vec = lambda n: pl.BlockSpec((None, 1, n), lambda i: (l, 0, 0))
    return pl.pallas_call(
        body,
        grid=(batch,),
        in_specs=[pl.BlockSpec((seq, BRANCH), lambda i: (i, 8)),
                  pl.BlockSpec((seq, BRANCH), lambda i: (i, 9)),
                  pl.BlockSpec((seq, BRANCH), lambda i: (i, 10)),
                  pl.BlockSpec((seq, U_SMALL), lambda i: (i, 0)),
                  pl.BlockSpec((None, SSD_CONV, 2 * BRANCH), lambda i: (l, 0, 0)),
                  vec(2 * BRANCH), vec(LANES), vec(LANES), vec(BRANCH), vec(BRANCH)],
        out_specs=pl.BlockSpec((seq, BRANCH), lambda i: (i, 0)),
        out_shape=jax.ShapeDtypeStruct((t, BRANCH), BF16),
        scratch_shapes=[pltpu.VMEM((seq + SUBLANES, 2 * BRANCH), F32),
                        pltpu.VMEM((2, LANES, 256), F32)],
        compiler_params=_params("arbitrary"),
        name="ssd",
    )(u, u, u, small, cw, cb, dtb, alog, dsk, nw)


def _merge_kernel(h_ref, ya_ref, yb_ref, yc_ref, yd_ref, wg_ref, wb_ref, bg_ref, o_ref,
                  wg_scr, wb_scr):
    @pl.when(pl.program_id(1) == 0)
    def _():
        for i in range(4):
            _cast_rows(wg_scr.at[i], wg_ref.at[i])
            _cast_rows(wb_scr.at[i], wb_ref.at[i])

    h = h_ref[...]
    acc = None
    for i, y_ref in enumerate((ya_ref, yb_ref, yc_ref, yd_ref)):
        gate = _sigmoid(_dot(h, wg_scr[i]) + bg_ref[i:i + 1, :])
        term = gate * _dot(y_ref[...], wb_scr[i])
        acc = term if acc is None else acc + term
    o_ref[...] = acc.astype(o_ref.dtype)


def _merge(h, ys, wg, wb, bg, l, tm=1024, tn=256):
    m, d = h.shape
    yspec = pl.BlockSpec((tm, BRANCH), lambda j, i: (i, 0))
    return pl.pallas_call(
        _merge_kernel,
        grid=(d // tn, m // tm),
        in_specs=[pl.BlockSpec((tm, d), lambda j, i: (i, 0)), yspec, yspec, yspec, yspec,
                  pl.BlockSpec((None, 4, d, tn), lambda j, i: (l, 0, 0, j)),
                  pl.BlockSpec((None, 4, BRANCH, tn), lambda j, i: (l, 0, 0, j)),
                  pl.BlockSpec((None, 4, tn), lambda j, i: (l, 0, j))],
        out_specs=pl.BlockSpec((tm, tn), lambda j, i: (i, j)),
        out_shape=jax.ShapeDtypeStruct((m, d), BF16),
        scratch_shapes=[pltpu.VMEM((4, d, tn), BF16), pltpu.VMEM((4, BRANCH, tn), BF16)],
        compiler_params=_params("arbitrary", "arbitrary"),
        name="gated_merge",
    )(h, *ys, wg, wb, bg)


def _ffn_up_kernel(tiles_per_seq, tm, h_ref, wg_ref, wv_ref, cwg_ref, cwv_ref, cbg_ref, cbv_ref,
                   o_ref, ghist, vhist, wg_scr, wv_scr):
    halo = SUBLANES
    sub = FFN_ROWS

    @pl.when(pl.program_id(1) == 0)
    def _():
        _cast_rows(wg_scr, wg_ref)
        _cast_rows(wv_scr, wv_ref)

    @pl.when(pl.program_id(1) % tiles_per_seq == 0)
    def _():
        ghist[...] = jnp.zeros_like(ghist)
        vhist[...] = jnp.zeros_like(vhist)

    def conv(u, hist, cw_ref, cb_ref):
        win = jnp.concatenate([hist, u], axis=0)
        out = jnp.broadcast_to(cb_ref[...], u.shape)
        for j in range(FFN_CONV):
            out = out + cw_ref[j:j + 1, :] * _shift_window(win, halo - (FFN_CONV - 1) + j, sub)
        return out, u[sub - halo:sub]

    gh, vh = ghist[...], vhist[...]
    for r in range(0, tm, sub):
        h = h_ref[r:r + sub, :]
        gate, gh = conv(_dot(h, wg_scr[...]), gh, cwg_ref, cbg_ref)
        val, vh = conv(_dot(h, wv_scr[...]), vh, cwv_ref, cbv_ref)
        o_ref[r:r + sub, :] = (_silu(gate) * val).astype(o_ref.dtype)
    ghist[...] = gh
    vhist[...] = vh


def _ffn_up(h, w_up, cw, cb, l, seq, tm=1024, tn=512):
    m, d = h.shape
    nt = D_FF // tn
    body = functools.partial(_ffn_up_kernel, seq // tm, tm)
    return pl.pallas_call(
        body,
        grid=(nt, m // tm),
        in_specs=[pl.BlockSpec((tm, d), lambda j, i: (i, 0)),
                  pl.BlockSpec((None, d, tn), lambda j, i: (l, 0, j)),
                  pl.BlockSpec((None, d, tn), lambda j, i: (l, 0, j + nt)),
                  pl.BlockSpec((None, FFN_CONV, tn), lambda j, i: (l, 0, j)),
                  pl.BlockSpec((None, FFN_CONV, tn), lambda j, i: (l, 0, j + nt)),
                  pl.BlockSpec((None, 1, tn), lambda j, i: (l, 0, j)),
                  pl.BlockSpec((None, 1, tn), lambda j, i: (l, 0, j + nt))],
        out_specs=pl.BlockSpec((tm, tn), lambda j, i: (i, j)),
        out_shape=jax.ShapeDtypeStruct((m, D_FF), BF16),
        scratch_shapes=[pltpu.VMEM((SUBLANES, tn), F32), pltpu.VMEM((SUBLANES, tn), F32),
                        pltpu.VMEM((d, tn), BF16), pltpu.VMEM((d, tn), BF16)],
        compiler_params=_params("arbitrary", "arbitrary"),
        name="ffn_up_conv_gate",
    )(h, w_up, w_up, cw, cw, cb, cb)


def kernel(x, mix_norm, w_in, diff_qk_norm, diff_lambda, diff_subln, gla_gk_w2, gla_gk_b, gla_norm, conv_dw_w, conv_dw_b, conv_ln_w, conv_ln_b, ssd_conv_w, ssd_conv_b, ssd_dt_bias, ssd_a_log, ssd_d, ssd_norm, w_branch, w_gate, b_gate, w_out, ffn_norm, ffn_w_up, ffn_conv_w, ffn_conv_b, ffn_w_down):
    batch, seq, d = x.shape
    depth = w_in.shape[0]
    t = batch * seq
    xf = x.reshape(t, d)

    w_in_small = jnp.concatenate(
        [w_in[:, :, 5648:5656], w_in[:, :, 3072:3088],
         jnp.zeros((depth, d, U_SMALL - 24), F32)], axis=-1).astype(BF16)
    w2_pad = jnp.zeros((depth, U_SMALL, 256), F32).at[:, 8:24, :].set(gla_gk_w2).astype(BF16)
    pad8 = lambda v: jnp.pad(v, ((0, 0), (0, LANES - 8)))[:, None, :]
    row = lambda v: v[:, None, :]
    qk_gain = jnp.tile(diff_qk_norm, (1, 1, 2))
    dskip = jnp.repeat(ssd_d, 64, axis=-1)[:, None, :]
    w_out_b = w_out.astype(BF16)

    for l in range(depth):
        lambda_init = 0.8 - 0.6 * math.exp(-0.3 * l)
        h = _rmsnorm(xf, row(mix_norm), l)
        u = _in_proj(h, w_in, l)
        small = _matmul(h, w_in_small, l, tm=1024, tn=U_SMALL, out_dtype=F32, name="in_proj_small")
        ya = _diff_attention(u, qk_gain, diff_lambda, row(diff_subln), l, lambda_init, batch, seq)
        yb = _gla(u, small, w2_pad, row(gla_gk_b), row(gla_norm), l, batch, seq)
        yc = _conformer(u, conv_dw_w, row(conv_dw_b), row(conv_ln_w), row(conv_ln_b), l, batch, seq)
        yd = _ssd(u, small, ssd_conv_w, row(ssd_conv_b), pad8(ssd_dt_bias), pad8(ssd_a_log), dskip,
                  row(ssd_norm), l, batch, seq)
        merged = _merge(h, (ya, yb, yc, yd), w_gate, w_branch, b_gate, l)
        xf, h2 = _proj_res_norm(merged, w_out_b, l, xf, row(ffn_norm), name="out_proj_norm")
        act = _ffn_up(h2, ffn_w_up, ffn_conv_w, row(ffn_conv_b), l, seq)
        xf = _matmul(act, ffn_w_down, l, tm=512, tn=512, out_dtype=F32, residual=xf, name="ffn_down")
    return xf.reshape(batch, seq, d)
```

```python
import functools
import math

import jax
import jax.numpy as jnp
from jax import lax
from jax.experimental import pallas as pl
from jax.experimental.pallas import tpu as pltpu

F32 = jnp.float32
BF16 = jnp.bfloat16

EPS = 1e-6
LOG2E = 1.4426950408889634
D_MODEL = 2048
BRANCH = 512
D_FF = 5632
U_BIG = 5632
U_SMALL = 128
LANES = 128
SUBLANES = 8
VMEM_LIMIT = 56 * 1024 * 1024

GLA_CHUNK = 64
GLA_BLOCK = 8
SSD_CHUNK = 256
CONV_WIDTH = 31
SSD_CONV = 4
FFN_CONV = 3
FFN_ROWS = 128
PROJ_ROWS = 256
CAST_ROWS = 256
EP_ROWS = 64
MXU_COLS = 256


def _params(*sem):
    return pltpu.CompilerParams(dimension_semantics=sem, vmem_limit_bytes=VMEM_LIMIT)


def _sigmoid(x):
    return 0.5 * jnp.tanh(0.5 * x) + 0.5


def _silu(x):
    half = 0.5 * x
    return half * jnp.tanh(half) + half


def _softplus(x):
    return jnp.maximum(x, 0.0) + jnp.log1p(jnp.exp(-jnp.abs(x)))


def _split3(x):
    hi = x.astype(BF16)
    r1 = x - hi.astype(F32)
    mid = r1.astype(BF16)
    lo = (r1 - mid.astype(F32)).astype(BF16)
    return hi, mid, lo


def _dot(a, b):
    return jnp.dot(a, b, preferred_element_type=F32)


def _dot_nt(a, b):
    return lax.dot_general(a, b, (((1,), (1,)), ((), ())), preferred_element_type=F32)


def _dot_tn(a, b):
    return lax.dot_general(a, b, (((0,), (0,)), ((), ())), preferred_element_type=F32)


def _select_dot(sel, x):
    hi, mid, lo = _split3(x)
    return _dot(sel, hi) + _dot(sel, mid) + _dot(sel, lo)


def _dot_select(x, sel):
    hi, mid, lo = _split3(x)
    return _dot(hi, sel) + _dot(mid, sel) + _dot(lo, sel)


def _shift_window(win, shift, rows):
    total = win.shape[0]
    if shift % SUBLANES == 0:
        return win[shift:shift + rows]
    rolled = pltpu.roll(win, total - (shift % SUBLANES), 0)
    base = shift - shift % SUBLANES
    return rolled[base:base + rows]


def _rmsnorm_kernel(x_ref, w_ref, o_ref):
    x = x_ref[...]
    ms = jnp.mean(x * x, axis=-1, keepdims=True)
    o_ref[...] = (x * lax.rsqrt(ms + EPS) * w_ref[...]).astype(o_ref.dtype)


def _rmsnorm(x, w3, l, tm=512):
    m, d = x.shape
    return pl.pallas_call(
        _rmsnorm_kernel,
        grid=(m // tm,),
        in_specs=[pl.BlockSpec((tm, d), lambda i: (i, 0)),
                  pl.BlockSpec((None, 1, d), lambda i: (l, 0, 0))],
        out_specs=pl.BlockSpec((tm, d), lambda i: (i, 0)),
        out_shape=jax.ShapeDtypeStruct((m, d), BF16),
        compiler_params=_params("arbitrary"),
        name="rmsnorm",
    )(x, w3)


def _matmul_kernel(a_ref, w_ref, o_ref):
    o_ref[...] = _dot(a_ref[...], w_ref[...]).astype(o_ref.dtype)


def _matmul_res_kernel(a_ref, w_ref, r_ref, o_ref):
    o_ref[...] = r_ref[...] + _dot(a_ref[...], w_ref[...])


def _cast_rows(dst_ref, src_ref):
    for r in range(0, src_ref.shape[0], CAST_ROWS):
        dst_ref[r:r + CAST_ROWS, :] = src_ref[r:r + CAST_ROWS, :].astype(BF16)


def _matmul_cast_res_kernel(a_ref, w_ref, r_ref, o_ref, w_scr):
    @pl.when(pl.program_id(1) == 0)
    def _():
        _cast_rows(w_scr, w_ref)

    o_ref[...] = r_ref[...] + _dot(a_ref[...], w_scr[...])


def _matmul(a, w3, l, *, tm, tn, out_dtype, residual=None, name="matmul"):
    m, k = a.shape
    n = w3.shape[-1]
    in_specs = [pl.BlockSpec((tm, k), lambda j, i: (i, 0)),
                pl.BlockSpec((None, k, tn), lambda j, i: (l, 0, j))]
    args = [a, w3]
    body = _matmul_kernel
    scratch = []
    if residual is not None:
        in_specs.append(pl.BlockSpec((tm, tn), lambda j, i: (i, j)))
        args.append(residual)
        body = _matmul_res_kernel
        if w3.dtype == F32:
            body = _matmul_cast_res_kernel
            scratch = [pltpu.VMEM((k, tn), BF16)]
    return pl.pallas_call(
        body,
        grid=(n // tn, m // tm),
        in_specs=in_specs,
        out_specs=pl.BlockSpec((tm, tn), lambda j, i: (i, j)),
        out_shape=jax.ShapeDtypeStruct((m, n), out_dtype),
        scratch_shapes=scratch,
        compiler_params=_params("arbitrary", "arbitrary"),
        name=name,
    )(*args)


IN_TN = 512
IN_SHIFT_TILE = 3072 // IN_TN
IN_SHIFT = 16


def _in_proj_kernel(a_ref, wa_ref, wb_ref, o_ref, w_scr):
    j = pl.program_id(0)
    first_row_tile = pl.program_id(1) == 0

    @pl.when(first_row_tile & (j < IN_SHIFT_TILE))
    def _():
        _cast_rows(w_scr, wa_ref)

    @pl.when(first_row_tile & (j >= IN_SHIFT_TILE))
    def _():
        for r in range(0, wa_ref.shape[0], CAST_ROWS):
            wide = jnp.concatenate([wa_ref[r:r + CAST_ROWS, :], wb_ref[r:r + CAST_ROWS, :]], axis=1)
            w_scr[r:r + CAST_ROWS, :] = wide[:, IN_SHIFT:IN_SHIFT + IN_TN].astype(BF16)

    o_ref[...] = _dot(a_ref[...], w_scr[...]).astype(o_ref.dtype)


def _in_proj(a, w_in, l, tm=2048):
    m, k = a.shape
    per = IN_TN // LANES
    return pl.pallas_call(
        _in_proj_kernel,
        grid=(U_BIG // IN_TN, m // tm),
        in_specs=[pl.BlockSpec((tm, k), lambda j, i: (i, 0)),
                  pl.BlockSpec((None, k, IN_TN), lambda j, i: (l, 0, j)),
                  pl.BlockSpec((None, k, LANES),
                               lambda j, i: (l, 0, per * (jnp.maximum(j, IN_SHIFT_TILE) + 1)))],
        out_specs=pl.BlockSpec((tm, IN_TN), lambda j, i: (i, j)),
        out_shape=jax.ShapeDtypeStruct((m, U_BIG), BF16),
        scratch_shapes=[pltpu.VMEM((k, IN_TN), BF16)],
        compiler_params=_params("arbitrary", "arbitrary"),
        name="in_proj",
    )(a, w_in, w_in)


def _proj_res_norm_kernel(tm, a_ref, w_ref, r_ref, nw_ref, x_ref, h_ref):
    for r in range(0, tm, PROJ_ROWS):
        rows = slice(r, r + PROJ_ROWS)
        xn = r_ref[rows, :] + _dot(a_ref[rows, :], w_ref[...])
        x_ref[rows, :] = xn
        ms = jnp.mean(xn * xn, axis=-1, keepdims=True)
        h_ref[rows, :] = (xn * lax.rsqrt(ms + EPS) * nw_ref[...]).astype(h_ref.dtype)


def _proj_res_norm(a, w3, l, residual, nw3, tm=512, name="proj_res_norm"):
    m, k = a.shape
    n = w3.shape[-1]
    return pl.pallas_call(
        functools.partial(_proj_res_norm_kernel, tm),
        grid=(m // tm,),
        in_specs=[pl.BlockSpec((tm, k), lambda i: (i, 0)),
                  pl.BlockSpec((None, k, n), lambda i: (l, 0, 0)),
                  pl.BlockSpec((tm, n), lambda i: (i, 0)),
                  pl.BlockSpec((None, 1, n), lambda i: (l, 0, 0))],
        out_specs=[pl.BlockSpec((tm, n), lambda i: (i, 0)),
                   pl.BlockSpec((tm, n), lambda i: (i, 0))],
        out_shape=[jax.ShapeDtypeStruct((m, n), F32), jax.ShapeDtypeStruct((m, n), BF16)],
        compiler_params=_params("arbitrary"),
        name=name,
    )(a, w3, residual, nw3)


LOW_COL0 = 3072
DT_COL0 = 5648


def _in_proj_small_kernel(a_ref, wlow_ref, wdt_ref, o_ref, w_scr):
    @pl.when(pl.program_id(0) == 0)
    def _():
        lane = lax.broadcasted_iota(jnp.int32, (1, LANES), 1)
        for r in range(0, w_scr.shape[0], CAST_ROWS):
            rows = slice(r, r + CAST_ROWS)
            dt = pltpu.roll(wdt_ref[rows, :], LANES - DT_COL0 % LANES, 1)
            low = pltpu.roll(wlow_ref[rows, :], 8 - LOW_COL0 % LANES, 1)
            w_scr[rows, :] = jnp.where(lane < 8, dt, jnp.where(lane < 24, low, 0.0)).astype(BF16)

    o_ref[...] = _dot(a_ref[...], w_scr[...])


def _in_proj_small(a, w_in, l, tm=1024):
    m, k = a.shape
    return pl.pallas_call(
        _in_proj_small_kernel,
        grid=(m // tm,),
        in_specs=[pl.BlockSpec((tm, k), lambda i: (i, 0)),
                  pl.BlockSpec((None, k, LANES), lambda i: (l, 0, LOW_COL0 // LANES)),
                  pl.BlockSpec((None, k, LANES), lambda i: (l, 0, DT_COL0 // LANES))],
        out_specs=pl.BlockSpec((tm, U_SMALL), lambda i: (i, 0)),
        out_shape=jax.ShapeDtypeStruct((m, U_SMALL), F32),
        scratch_shapes=[pltpu.VMEM((k, U_SMALL), BF16)],
        compiler_params=_params("arbitrary"),
        name="in_proj_small",
    )(a, w_in, w_in)


def _diff_attn_kernel(lambda_init, seq, tq, q_ref, k_ref, v_ref, gain_ref, lam_ref, subln_ref,
                      o_ref, kn_ref):
    lane = lax.broadcasted_iota(jnp.int32, (1, LANES), 1)
    first = lane < 64

    def halfnorm(t, g):
        sq = t * t
        s1 = jnp.sum(jnp.where(first, sq, 0.0), axis=-1, keepdims=True)
        s2 = jnp.sum(jnp.where(first, 0.0, sq), axis=-1, keepdims=True)
        ms = jnp.where(first, s1, s2) * (1.0 / 64)
        return t * lax.rsqrt(ms + EPS) * g

    kn_ref[...] = halfnorm(k_ref[...].astype(F32), gain_ref[1:2, :]).astype(BF16)
    lam = lam_ref[...]
    l1 = jnp.sum(lam[0:1] * lam[1:2], axis=-1, keepdims=True)
    l2 = jnp.sum(lam[2:3] * lam[3:4], axis=-1, keepdims=True)
    lam_full = jnp.exp(l1) - jnp.exp(l2) + lambda_init

    visible = ((lax.broadcasted_iota(jnp.int32, (2 * tq, tq), 1) >> 6)
               <= ((lax.broadcasted_iota(jnp.int32, (2 * tq, tq), 0) & (tq - 1)) >> 6))

    for i in range(seq // tq):
        q0 = i * tq
        qn = halfnorm(q_ref[q0:q0 + tq, :].astype(F32), gain_ref[0:1, :]) * (64 ** -0.5 * LOG2E)
        qs = jnp.concatenate([jnp.where(first, qn, 0.0), jnp.where(first, 0.0, qn)],
                             axis=0).astype(BF16)
        s_dg = jnp.where(visible, _dot_nt(qs, kn_ref[q0:q0 + tq, :]), -jnp.inf)
        m = jnp.max(s_dg, axis=-1, keepdims=True)
        if q0:
            s_off = _dot_nt(qs, kn_ref[0:q0, :])
            m = jnp.maximum(m, jnp.max(s_off, axis=-1, keepdims=True))
        p_dg = jnp.exp2(s_dg - m)
        den = jnp.sum(p_dg, axis=-1, keepdims=True)
        acc = _dot(p_dg.astype(BF16), v_ref[q0:q0 + tq, :])
        if q0:
            p_off = jnp.exp2(s_off - m)
            den = den + jnp.sum(p_off, axis=-1, keepdims=True)
            acc = acc + _dot(p_off.astype(BF16), v_ref[0:q0, :])
        acc = acc * (1.0 / den)
        o = acc[0:tq] - lam_full * acc[tq:2 * tq]
        ms = jnp.mean(o * o, axis=-1, keepdims=True)
        o = o * lax.rsqrt(ms + EPS) * subln_ref[...] * (1.0 - lambda_init)
        o_ref[q0:q0 + tq, :] = o.astype(o_ref.dtype)


def _diff_attention(u, gain, lam, subln, l, lambda_init, batch, seq, tq=256):
    t = batch * seq
    body = functools.partial(_diff_attn_kernel, lambda_init, seq, tq)
    return pl.pallas_call(
        body,
        grid=(batch, 4),
        in_specs=[pl.BlockSpec((seq, LANES), lambda b, h: (b, h)),
                  pl.BlockSpec((seq, LANES), lambda b, h: (b, 4 + h)),
                  pl.BlockSpec((seq, LANES), lambda b, h: (b, 8 + h)),
                  pl.BlockSpec((None, 2, LANES), lambda b, h: (l, 0, 0)),
                  pl.BlockSpec((None, 4, 64), lambda b, h: (l, 0, 0)),
                  pl.BlockSpec((None, 1, LANES), lambda b, h: (l, 0, 0))],
        out_specs=pl.BlockSpec((seq, LANES), lambda b, h: (b, h)),
        out_shape=jax.ShapeDtypeStruct((t, BRANCH), BF16),
        scratch_shapes=[pltpu.VMEM((seq, LANES), BF16)],
        compiler_params=_params("arbitrary", "arbitrary"),
        name="diff_attention",
    )(u, u, u, gain, lam, subln)


def _gla_kernel(seq, q_ref, k_ref, v_ref, og_ref, small_ref, w2_ref, gkb_ref, nw_ref, o_ref,
                state_ref):
    cs, nb = GLA_CHUNK, GLA_CHUNK // GLA_BLOCK
    ri = lax.broadcasted_iota(jnp.int32, (cs, cs), 0)
    ci = lax.broadcasted_iota(jnp.int32, (cs, cs), 1)
    tri = (ci <= ri).astype(BF16)
    row = lax.broadcasted_iota(jnp.int32, (cs, LANES), 0)
    row_blk = row >> 3
    row_mod = row & 7
    sub3 = lax.broadcasted_iota(jnp.int32, (nb, GLA_BLOCK, LANES), 1)
    lane = lax.broadcasted_iota(jnp.int32, (1, LANES), 1)
    head0 = lane < 64
    r2 = lax.broadcasted_iota(jnp.int32, (2 * cs, cs), 0)
    c2 = lax.broadcasted_iota(jnp.int32, (2 * cs, cs), 1)
    same_blk = ((r2 & (cs - 1)) >> 3) == (c2 >> 3)
    sr = lax.broadcasted_iota(jnp.int32, (2 * LANES, LANES), 0)
    sc = lax.broadcasted_iota(jnp.int32, (2 * LANES, LANES), 1)
    state_mask = (sr >> 7) == (sc >> 6)
    state_ref[...] = jnp.zeros_like(state_ref)

    def chunk(c, carry):
        r0 = pl.multiple_of(c * cs, cs)
        rows = pl.ds(r0, cs)
        z = _dot(small_ref[rows, :].astype(BF16), w2_ref[...]) + gkb_ref[...]
        gk_all = (jnp.minimum(z, 0.0) - jnp.log1p(jnp.exp(-jnp.abs(z)))) * (1.0 / 16.0)
        for p in range(2):
            lanes = slice(p * LANES, (p + 1) * LANES)
            q = q_ref[rows, lanes].astype(F32) * (64 ** -0.5)
            k = k_ref[rows, lanes].astype(F32)
            cum = _select_dot(tri, gk_all[:, lanes])
            cum3 = cum.reshape(nb, GLA_BLOCK, LANES)
            q3 = q.reshape(nb, GLA_BLOCK, LANES)
            last3 = cum3[:, GLA_BLOCK - 1:GLA_BLOCK, :]
            ref3 = jnp.concatenate([jnp.zeros((1, 1, LANES), F32), last3[:nb - 1]], axis=0)
            refrow = jnp.broadcast_to(ref3, (nb, GLA_BLOCK, LANES)).reshape(cs, LANES)
            qp = q * jnp.exp(cum - refrow)
            lhs_off, rhs_off = [], []
            for blk in range(1, nb):
                lhs_off.append(jnp.where(row_blk == blk, qp, 0.0))
                n = blk * GLA_BLOCK
                kd = k[0:n] * jnp.exp(cum[n - 1:n, :] - cum[0:n])
                rhs_off.append(jnp.concatenate([kd, jnp.zeros((cs - n, LANES), F32)], axis=0))
            lhs_off = jnp.concatenate(lhs_off, axis=1)
            rhs_off = jnp.concatenate(rhs_off, axis=1).astype(BF16)
            lhs_dg, rhs_dg = [], []
            for j in range(GLA_BLOCK):
                dec = jnp.exp(jnp.minimum(cum3 - cum3[:, j:j + 1, :], 0.0))
                lhs_dg.append(jnp.where(sub3 >= j, q3 * dec, 0.0).reshape(cs, LANES))
                rhs_dg.append(jnp.where(row_mod == j, k, 0.0))
            lhs_dg = jnp.concatenate(lhs_dg, axis=1)
            rhs_dg = jnp.concatenate(rhs_dg, axis=1).astype(BF16)

            def both_heads(x):
                m0 = (lax.broadcasted_iota(jnp.int32, (1, x.shape[1]), 1) & 64) == 0
                return jnp.concatenate([jnp.where(m0, x, 0.0), jnp.where(m0, 0.0, x)],
                                       axis=0).astype(BF16)

            s_off = _dot_nt(both_heads(lhs_off), rhs_off)
            s_dg = _dot_nt(both_heads(lhs_dg), rhs_dg)
            a = (s_off + jnp.where(same_blk, s_dg, 0.0)).astype(BF16)
            st = state_ref[p]
            inter = _dot_nt((q * jnp.exp(cum)).astype(BF16), st.astype(BF16))
            cum_last = cum[cs - 1:cs, :]
            kdec = (k * jnp.exp(cum_last - cum)).astype(BF16)
            vv = v_ref[rows, p * 2 * LANES:(p + 1) * 2 * LANES]
            upd = _dot_tn(vv, kdec)
            state_ref[p] = jnp.where(state_mask, st * jnp.exp(cum_last) + upd, 0.0)
            for e in range(2):
                cols = slice((2 * p + e) * LANES, (2 * p + e + 1) * LANES)
                o = _dot(a[e * cs:(e + 1) * cs], v_ref[rows, cols]) + inter[:, e * LANES:(e + 1) * LANES]
                ms = jnp.mean(o * o, axis=-1, keepdims=True)
                o = o * lax.rsqrt(ms + EPS) * nw_ref[...]
                o_ref[rows, cols] = (o * _silu(og_ref[rows, cols].astype(F32))).astype(o_ref.dtype)
        return carry

    lax.fori_loop(0, seq // cs, chunk, 0, unroll=4)


def _gla(u, small, w2p, gkb, nw, l, batch, seq):
    t = batch * seq
    body = functools.partial(_gla_kernel, seq)
    return pl.pallas_call(
        body,
        grid=(batch,),
        in_specs=[pl.BlockSpec((seq, 256), lambda b: (b, 6)),
                  pl.BlockSpec((seq, 256), lambda b: (b, 7)),
                  pl.BlockSpec((seq, 512), lambda b: (b, 4)),
                  pl.BlockSpec((seq, 512), lambda b: (b, 5)),
                  pl.BlockSpec((seq, U_SMALL), lambda b: (b, 0)),
                  pl.BlockSpec((None, U_SMALL, 256), lambda b: (l, 0, 0)),
                  pl.BlockSpec((None, 1, 256), lambda b: (l, 0, 0)),
                  pl.BlockSpec((None, 1, LANES), lambda b: (l, 0, 0))],
        out_specs=pl.BlockSpec((seq, BRANCH), lambda b: (b, 0)),
        out_shape=jax.ShapeDtypeStruct((t, BRANCH), BF16),
        scratch_shapes=[pltpu.VMEM((2, 2 * LANES, LANES), F32)],
        compiler_params=_params("arbitrary"),
        name="gla",
    )(u, u, u, u, small, w2p, gkb, nw)


def _conformer_kernel(seq, tr, a_ref, g_ref, w_ref, b_ref, lnw_ref, lnb_ref, o_ref, gbuf, cbuf):
    halo = 32
    gbuf[0:halo, :] = jnp.zeros((halo, BRANCH), F32)
    for r in range(0, seq, 256):
        gbuf[halo + r:halo + r + 256, :] = (a_ref[r:r + 256, :].astype(F32)
                                            * _sigmoid(g_ref[r:r + 256, :].astype(F32)))

    def conv_tile(i, carry):
        r0 = pl.multiple_of(i * tr, tr)
        for c0 in range(0, BRANCH, LANES):
            win = gbuf[pl.ds(r0, tr + halo), c0:c0 + LANES]
            acc = jnp.broadcast_to(b_ref[:, c0:c0 + LANES], (tr, LANES))
            for j in range(CONV_WIDTH):
                acc = acc + w_ref[j:j + 1, c0:c0 + LANES] * _shift_window(
                    win, halo - (CONV_WIDTH - 1) + j, tr)
            cbuf[pl.ds(r0, tr), c0:c0 + LANES] = acc
        return carry

    lax.fori_loop(0, seq // tr, conv_tile, 0)

    def norm_tile(i, carry):
        r0 = pl.multiple_of(i * tr, tr)
        c = cbuf[pl.ds(r0, tr), :]
        mu = jnp.mean(c, axis=-1, keepdims=True)
        d = c - mu
        var = jnp.mean(d * d, axis=-1, keepdims=True)
        y = d * lax.rsqrt(var + EPS) * lnw_ref[...] + lnb_ref[...]
        o_ref[pl.ds(r0, tr), :] = _silu(y).astype(o_ref.dtype)
        return carry

    lax.fori_loop(0, seq // tr, norm_tile, 0)


def _conformer(u, w, b, lnw, lnb, l, batch, seq, tr=128):
    t = batch * seq
    body = functools.partial(_conformer_kernel, seq, tr)
    return pl.pallas_call(
        body,
        grid=(batch,),
        in_specs=[pl.BlockSpec((seq, BRANCH), lambda i: (i, 6)),
                  pl.BlockSpec((seq, BRANCH), lambda i: (i, 7)),
                  pl.BlockSpec((None, CONV_WIDTH, BRANCH), lambda i: (l, 0, 0)),
                  pl.BlockSpec((None, 1, BRANCH), lambda i: (l, 0, 0)),
                  pl.BlockSpec((None, 1, BRANCH), lambda i: (l, 0, 0)),
                  pl.BlockSpec((None, 1, BRANCH), lambda i: (l, 0, 0))],
        out_specs=pl.BlockSpec((seq, BRANCH), lambda i: (i, 0)),
        out_shape=jax.ShapeDtypeStruct((t, BRANCH), BF16),
        scratch_shapes=[pltpu.VMEM((seq + 32, BRANCH), F32), pltpu.VMEM((seq, BRANCH), F32)],
        compiler_params=_params("arbitrary"),
        name="conformer_conv",
    )(u, u, w, b, lnw, lnb)


def _ssd_kernel(seq, z_ref, x_ref, bc_ref, small_ref, cw_ref, cb_ref, dtb_ref, alog_ref, dsk_ref,
                nw_ref, o_ref, xbuf, state_ref):
    cs = SSD_CHUNK
    halo = SUBLANES
    xbuf[0:halo, :] = jnp.zeros((halo, 2 * BRANCH), F32)
    for r in range(0, seq, 256):
        xbuf[halo + r:halo + r + 256, 0:BRANCH] = x_ref[r:r + 256, :].astype(F32)
        xbuf[halo + r:halo + r + 256, BRANCH:2 * BRANCH] = bc_ref[r:r + 256, :].astype(F32)
    state_ref[...] = jnp.zeros_like(state_ref)

    ri = lax.broadcasted_iota(jnp.int32, (cs, cs), 0)
    ci = lax.broadcasted_iota(jnp.int32, (cs, cs), 1)
    causal = ci <= ri
    tri = causal.astype(BF16)
    er = lax.broadcasted_iota(jnp.int32, (LANES, BRANCH), 0)
    ec = lax.broadcasted_iota(jnp.int32, (LANES, BRANCH), 1)
    expand = ((ec >> 6) == er).astype(BF16)
    lane = lax.broadcasted_iota(jnp.int32, (1, LANES), 1)
    head0 = lane < 64
    a_neg = -jnp.exp(alog_ref[...])

    def chunk(c, carry):
        r0 = pl.multiple_of(c * cs, cs)
        rows = pl.ds(r0, cs)
        win = xbuf[pl.ds(r0, cs + halo), :]
        conv = jnp.broadcast_to(cb_ref[...], (cs, 2 * BRANCH))
        for j in range(SSD_CONV):
            conv = conv + cw_ref[j:j + 1, :] * _shift_window(win, halo - (SSD_CONV - 1) + j, cs)
        xc = _silu(conv)
        xs = xc[:, 0:BRANCH]
        dt = _softplus(small_ref[rows, :] + dtb_ref[...])
        cum = _select_dot(tri, dt * a_neg)
        cum_t = cum.T
        ecum = jnp.exp(cum)
        cum_last = cum[cs - 1:cs, :]
        dt_x = _dot_select(dt, expand)
        ecum_x = _dot_select(ecum, expand)
        dte_x = _dot_select(jnp.exp(cum_last - cum), expand)
        xdt = xs * dt_x
        ys = []
        for g in range(2):
            bm = xc[:, BRANCH + g * LANES:BRANCH + (g + 1) * LANES].astype(BF16)
            cm = xc[:, BRANCH + 256 + g * LANES:BRANCH + 256 + (g + 1) * LANES].astype(BF16)
            cb = _dot_nt(cm, bm)
            gl = slice(g * 256, (g + 1) * 256)
            for pr in range(2):
                pl_ = slice(g * 256 + pr * LANES, g * 256 + (pr + 1) * LANES)
                xp = xdt[:, pl_]
                y = None
                for e in range(2):
                    h = g * 4 + pr * 2 + e
                    seg = jnp.where(causal, cum[:, h:h + 1] - cum_t[h:h + 1, :], -jnp.inf)
                    m = (cb * jnp.exp(seg)).astype(BF16)
                    xh = jnp.where(head0 if e == 0 else ~head0, xp, 0.0).astype(BF16)
                    t = _dot(m, xh)
                    y = t if y is None else y + t
                ys.append(y)
            st = state_ref[g]
            y_off = _dot(cm, st.astype(BF16)) * ecum_x[:, gl]
            ys[2 * g] = ys[2 * g] + y_off[:, 0:LANES]
            ys[2 * g + 1] = ys[2 * g + 1] + y_off[:, LANES:2 * LANES]
            upd = _dot_tn(bm, (xdt[:, gl] * dte_x[:, gl]).astype(BF16))
            state_ref[g] = st * ecum_x[cs - 1:cs, gl] + upd
        y = jnp.concatenate(ys, axis=1) + dsk_ref[...] * xs
        y = y * _silu(z_ref[rows, :].astype(F32))
        outs = []
        for g in range(2):
            yg = y[:, g * 256:(g + 1) * 256]
            ms = jnp.mean(yg * yg, axis=-1, keepdims=True)
            outs.append(yg * lax.rsqrt(ms + EPS))
        o_ref[rows, :] = (jnp.concatenate(outs, axis=1) * nw_ref[...]).astype(o_ref.dtype)
        return carry

    lax.fori_loop(0, seq // cs, chunk, 0)


def _ssd(u, small, cw, cb, dtb, alog, dsk, nw, l, batch, seq):
    t = batch * seq
    body = functools.partial(_ssd_kernel, seq)
    vec = lambda n: pl.BlockSpec((None, 1, n), lambda i: (l, 0, 0))
    return pl.pallas_call(
        body,
        grid=(batch,),
        in_specs=[pl.BlockSpec((seq, BRANCH), lambda i: (i, 8)),
                  pl.BlockSpec((seq, BRANCH), lambda i: (i, 9)),
                  pl.BlockSpec((seq, BRANCH), lambda i: (i, 10)),
                  pl.BlockSpec((seq, U_SMALL), lambda i: (i, 0)),
                  pl.BlockSpec((None, SSD_CONV, 2 * BRANCH), lambda i: (l, 0, 0)),
                  vec(2 * BRANCH), vec(LANES), vec(LANES), vec(BRANCH), vec(BRANCH)],
        out_specs=pl.BlockSpec((seq, BRANCH), lambda i: (i, 0)),
        out_shape=jax.ShapeDtypeStruct((t, BRANCH), BF16),
        scratch_shapes=[pltpu.VMEM((seq + SUBLANES, 2 * BRANCH), F32),
                        pltpu.VMEM((2, LANES, 256), F32)],
        compiler_params=_params("arbitrary"),
        name="ssd",
    )(u, u, u, small, cw, cb, dtb, alog, dsk, nw)


def _merge_kernel(h_ref, ya_ref, yb_ref, yc_ref, yd_ref, wg_ref, wb_ref, bg_ref, o_ref,
                  wg_scr, wb_scr):
    @pl.when(pl.program_id(1) == 0)
    def _():
        for i in range(4):
            _cast_rows(wg_scr.at[i], wg_ref.at[i])
            _cast_rows(wb_scr.at[i], wb_ref.at[i])

    h = h_ref[...]
    acc = None
    for i, y_ref in enumerate((ya_ref, yb_ref, yc_ref, yd_ref)):
        gate = _sigmoid(_dot(h, wg_scr[i]) + bg_ref[i:i + 1, :])
        term = gate * _dot(y_ref[...], wb_scr[i])
        acc = term if acc is None else acc + term
    o_ref[...] = acc.astype(o_ref.dtype)


def _merge(h, ys, wg, wb, bg, l, tm=1024, tn=256):
    m, d = h.shape
    yspec = pl.BlockSpec((tm, BRANCH), lambda j, i: (i, 0))
    return pl.pallas_call(
        _merge_kernel,
        grid=(d // tn, m // tm),
        in_specs=[pl.BlockSpec((tm, d), lambda j, i: (i, 0)), yspec, yspec, yspec, yspec,
                  pl.BlockSpec((None, 4, d, tn), lambda j, i: (l, 0, 0, j)),
                  pl.BlockSpec((None, 4, BRANCH, tn), lambda j, i: (l, 0, 0, j)),
                  pl.BlockSpec((None, 4, tn), lambda j, i: (l, 0, j))],
        out_specs=pl.BlockSpec((tm, tn), lambda j, i: (i, j)),
        out_shape=jax.ShapeDtypeStruct((m, d), BF16),
        scratch_shapes=[pltpu.VMEM((4, d, tn), BF16), pltpu.VMEM((4, BRANCH, tn), BF16)],
        compiler_params=_params("arbitrary", "arbitrary"),
        name="gated_merge",
    )(h, *ys, wg, wb, bg)


def _ffn_up_kernel(n_tiles, row_tiles, tiles_per_seq, tm, h_ref, wg_ref, wv_ref, cwg_ref, cwv_ref,
                   cbg_ref, cbv_ref, o_ref, raw_g, raw_v, ghist, vhist, wg_scr, wv_scr):
    halo = SUBLANES
    sub = FFN_ROWS
    s = pl.program_id(0)
    cur = jnp.minimum(s, n_tiles - 1)
    prev = jnp.maximum(s - 1, 0)
    slot = s & 1

    @pl.when(s == 0)
    def _():
        raw_g[1] = jnp.zeros(raw_g.shape[1:], F32)
        raw_v[1] = jnp.zeros(raw_v.shape[1:], F32)

    @pl.when((cur % row_tiles == 0) & (s < n_tiles))
    def _():
        _cast_rows(wg_scr, wg_ref)
        _cast_rows(wv_scr, wv_ref)

    @pl.when(prev % tiles_per_seq == 0)
    def _():
        ghist[...] = jnp.zeros_like(ghist)
        vhist[...] = jnp.zeros_like(vhist)

    def conv(raw, old, hist, cw_ref, cb_ref, r, cols):
        if r == 0:
            win = jnp.concatenate([hist[:, cols], raw[old, 0:EP_ROWS, cols]], axis=0)
        else:
            win = raw[old, r - halo:r + EP_ROWS, cols]
        out = jnp.broadcast_to(cb_ref[:, cols], (EP_ROWS, LANES))
        for j in range(FFN_CONV):
            out = out + cw_ref[j:j + 1, cols] * _shift_window(win, halo - (FFN_CONV - 1) + j,
                                                              EP_ROWS)
        return out

    def step(new, old):
        tn = raw_g.shape[2]
        for r0 in range(0, tm, sub):
            h = h_ref[r0:r0 + sub, :]
            pieces = [(r, c) for r in range(r0, r0 + sub, EP_ROWS) for c in range(0, tn, LANES)]
            dots = [(raw, w, c) for c in range(0, tn, MXU_COLS)
                    for raw, w in ((raw_g, wg_scr), (raw_v, wv_scr))]
            per_dot = len(pieces) // len(dots)
            for n, (raw, w, c) in enumerate(dots):
                raw[new, r0:r0 + sub, c:c + MXU_COLS] = _dot(h, w[:, c:c + MXU_COLS])
                for r, pc in pieces[n * per_dot:(n + 1) * per_dot]:
                    cols = slice(pc, pc + LANES)
                    gate = conv(raw_g, old, ghist, cwg_ref, cbg_ref, r, cols)
                    val = conv(raw_v, old, vhist, cwv_ref, cbv_ref, r, cols)
                    o_ref[r:r + EP_ROWS, cols] = (_silu(gate) * val).astype(o_ref.dtype)
        ghist[...] = raw_g[old, tm - halo:tm, :]
        vhist[...] = raw_v[old, tm - halo:tm, :]

    @pl.when(slot == 0)
    def _():
        step(0, 1)

    @pl.when(slot == 1)
    def _():
        step(1, 0)


def _ffn_up(h, w_up, cw, cb, l, seq, tm=1024, tn=512):
    m, d = h.shape
    nt = D_FF // tn
    rt = m // tm
    n_tiles = nt * rt
    body = functools.partial(_ffn_up_kernel, n_tiles, rt, seq // tm, tm)
    cur = lambda s: jnp.minimum(s, n_tiles - 1)
    prev = lambda s: jnp.maximum(s - 1, 0)
    return pl.pallas_call(
        body,
        grid=(n_tiles + 1,),
        in_specs=[pl.BlockSpec((tm, d), lambda s: (cur(s) % rt, 0)),
                  pl.BlockSpec((None, d, tn), lambda s: (l, 0, cur(s) // rt)),
                  pl.BlockSpec((None, d, tn), lambda s: (l, 0, cur(s) // rt + nt)),
                  pl.BlockSpec((None, FFN_CONV, tn), lambda s: (l, 0, prev(s) // rt)),
                  pl.BlockSpec((None, FFN_CONV, tn), lambda s: (l, 0, prev(s) // rt + nt)),
                  pl.BlockSpec((None, 1, tn), lambda s: (l, 0, prev(s) // rt)),
                  pl.BlockSpec((None, 1, tn), lambda s: (l, 0, prev(s) // rt + nt))],
        out_specs=pl.BlockSpec((tm, tn), lambda s: (prev(s) % rt, prev(s) // rt)),
        out_shape=jax.ShapeDtypeStruct((m, D_FF), BF16),
        scratch_shapes=[pltpu.VMEM((2, tm, tn), F32), pltpu.VMEM((2, tm, tn), F32),
                        pltpu.VMEM((SUBLANES, tn), F32), pltpu.VMEM((SUBLANES, tn), F32),
                        pltpu.VMEM((d, tn), BF16), pltpu.VMEM((d, tn), BF16)],
        compiler_params=_params("arbitrary"),
        name="ffn_up_conv_gate",
    )(h, w_up, w_up, cw, cw, cb, cb)


def kernel(x, mix_norm, w_in, diff_qk_norm, diff_lambda, diff_subln, gla_gk_w2, gla_gk_b, gla_norm, conv_dw_w, conv_dw_b, conv_ln_w, conv_ln_b, ssd_conv_w, ssd_conv_b, ssd_dt_bias, ssd_a_log, ssd_d, ssd_norm, w_branch, w_gate, b_gate, w_out, ffn_norm, ffn_w_up, ffn_conv_w, ffn_conv_b, ffn_w_down):
    batch, seq, d = x.shape
    depth = w_in.shape[0]
    t = batch * seq
    xf = x.reshape(t, d)

    w2_pad = jnp.zeros((depth, U_SMALL, 256), F32).at[:, 8:24, :].set(gla_gk_w2).astype(BF16)
    pad8 = lambda v: jnp.pad(v, ((0, 0), (0, LANES - 8)))[:, None, :]
    row = lambda v: v[:, None, :]
    qk_gain = jnp.tile(diff_qk_norm, (1, 1, 2))
    dskip = jnp.repeat(ssd_d, 64, axis=-1)[:, None, :]
    w_out_b = w_out.astype(BF16)

    for l in range(depth):
        lambda_init = 0.8 - 0.6 * math.exp(-0.3 * l)
        h = _rmsnorm(xf, row(mix_norm), l)
        u = _in_proj(h, w_in, l)
        small = _in_proj_small(h, w_in, l)
        ya = _diff_attention(u, qk_gain, diff_lambda, row(diff_subln), l, lambda_init, batch, seq)
        yb = _gla(u, small, w2_pad, row(gla_gk_b), row(gla_norm), l, batch, seq)
        yc = _conformer(u, conv_dw_w, row(conv_dw_b), row(conv_ln_w), row(conv_ln_b), l, batch, seq)
        yd = _ssd(u, small, ssd_conv_w, row(ssd_conv_b), pad8(ssd_dt_bias), pad8(ssd_a_log), dskip,
                  row(ssd_norm), l, batch, seq)
        merged = _merge(h, (ya, yb, yc, yd), w_gate, w_branch, b_gate, l)
        xf, h2 = _proj_res_norm(merged, w_out_b, l, xf, row(ffn_norm), name="out_proj_norm")
        act = _ffn_up(h2, ffn_w_up, ffn_conv_w, row(ffn_conv_b), l, seq)
        xf = _matmul(act, ffn_w_down, l, tm=512, tn=512, out_dtype=F32, residual=xf, name="ffn_down")
    return xf.reshape(batch, seq, d)
```

```python
import functools
import math

import jax
import jax.numpy as jnp
from jax import lax
from jax.experimental import pallas as pl
from jax.experimental.pallas import tpu as pltpu

F32 = jnp.float32
BF16 = jnp.bfloat16

EPS = 1e-6
LOG2E = 1.4426950408889634
D_MODEL = 2048
BRANCH = 512
D_FF = 5632
U_BIG = 5632
U_SMALL = 128
LANES = 128
SUBLANES = 8
VMEM_LIMIT = 56 * 1024 * 1024

GLA_CHUNK = 64
GLA_BLOCK = 8
GLA_GROUP = 4
SSD_CHUNK = 256
CONV_WIDTH = 31
SSD_CONV = 4
FFN_CONV = 3
FFN_ROWS = 128
PROJ_ROWS = 256
CAST_ROWS = 256
EP_ROWS = 64
MXU_COLS = 256


def _params(*sem):
    return pltpu.CompilerParams(dimension_semantics=sem, vmem_limit_bytes=VMEM_LIMIT)


def _sigmoid(x):
    return 0.5 * jnp.tanh(0.5 * x) + 0.5


def _silu(x):
    half = 0.5 * x
    return half * jnp.tanh(half) + half


def _softplus(x):
    return jnp.maximum(x, 0.0) + jnp.log1p(jnp.exp(-jnp.abs(x)))


def _split3(x):
    hi = x.astype(BF16)
    r1 = x - hi.astype(F32)
    mid = r1.astype(BF16)
    lo = (r1 - mid.astype(F32)).astype(BF16)
    return hi, mid, lo


def _dot(a, b):
    return jnp.dot(a, b, preferred_element_type=F32)


def _dot_nt(a, b):
    return lax.dot_general(a, b, (((1,), (1,)), ((), ())), preferred_element_type=F32)


def _dot_tn(a, b):
    return lax.dot_general(a, b, (((0,), (0,)), ((), ())), preferred_element_type=F32)


def _select_dot(sel, x):
    hi, mid, lo = _split3(x)
    return _dot(sel, hi) + _dot(sel, mid) + _dot(sel, lo)


def _dot_select(x, sel):
    hi, mid, lo = _split3(x)
    return _dot(hi, sel) + _dot(mid, sel) + _dot(lo, sel)


def _shift_window(win, shift, rows):
    total = win.shape[0]
    if shift % SUBLANES == 0:
        return win[shift:shift + rows]
    rolled = pltpu.roll(win, total - (shift % SUBLANES), 0)
    base = shift - shift % SUBLANES
    return rolled[base:base + rows]


def _rmsnorm_kernel(x_ref, w_ref, o_ref):
    x = x_ref[...]
    ms = jnp.mean(x * x, axis=-1, keepdims=True)
    o_ref[...] = (x * lax.rsqrt(ms + EPS) * w_ref[...]).astype(o_ref.dtype)


def _rmsnorm(x, w3, l, tm=512):
    m, d = x.shape
    return pl.pallas_call(
        _rmsnorm_kernel,
        grid=(m // tm,),
        in_specs=[pl.BlockSpec((tm, d), lambda i: (i, 0)),
                  pl.BlockSpec((None, 1, d), lambda i: (l, 0, 0))],
        out_specs=pl.BlockSpec((tm, d), lambda i: (i, 0)),
        out_shape=jax.ShapeDtypeStruct((m, d), BF16),
        compiler_params=_params("arbitrary"),
        name="rmsnorm",
    )(x, w3)


def _matmul_kernel(a_ref, w_ref, o_ref):
    o_ref[...] = _dot(a_ref[...], w_ref[...]).astype(o_ref.dtype)


def _matmul_res_kernel(a_ref, w_ref, r_ref, o_ref):
    o_ref[...] = r_ref[...] + _dot(a_ref[...], w_ref[...])


def _cast_rows(dst_ref, src_ref):
    for r in range(0, src_ref.shape[0], CAST_ROWS):
        dst_ref[r:r + CAST_ROWS, :] = src_ref[r:r + CAST_ROWS, :].astype(BF16)


def _matmul_cast_res_kernel(a_ref, w_ref, r_ref, o_ref, w_scr):
    @pl.when(pl.program_id(1) == 0)
    def _():
        _cast_rows(w_scr, w_ref)

    o_ref[...] = r_ref[...] + _dot(a_ref[...], w_scr[...])


def _matmul(a, w3, l, *, tm, tn, out_dtype, residual=None, name="matmul"):
    m, k = a.shape
    n = w3.shape[-1]
    in_specs = [pl.BlockSpec((tm, k), lambda j, i: (i, 0)),
                pl.BlockSpec((None, k, tn), lambda j, i: (l, 0, j))]
    args = [a, w3]
    body = _matmul_kernel
    scratch = []
    if residual is not None:
        in_specs.append(pl.BlockSpec((tm, tn), lambda j, i: (i, j)))
        args.append(residual)
        body = _matmul_res_kernel
        if w3.dtype == F32:
            body = _matmul_cast_res_kernel
            scratch = [pltpu.VMEM((k, tn), BF16)]
    return pl.pallas_call(
        body,
        grid=(n // tn, m // tm),
        in_specs=in_specs,
        out_specs=pl.BlockSpec((tm, tn), lambda j, i: (i, j)),
        out_shape=jax.ShapeDtypeStruct((m, n), out_dtype),
        scratch_shapes=scratch,
        compiler_params=_params("arbitrary", "arbitrary"),
        name=name,
    )(*args)


IN_TN = 512
IN_SHIFT_TILE = 3072 // IN_TN
IN_SHIFT = 16


def _in_proj_kernel(a_ref, wa_ref, wb_ref, o_ref, w_scr):
    j = pl.program_id(0)
    first_row_tile = pl.program_id(1) == 0

    @pl.when(first_row_tile & (j < IN_SHIFT_TILE))
    def _():
        _cast_rows(w_scr, wa_ref)

    @pl.when(first_row_tile & (j >= IN_SHIFT_TILE))
    def _():
        for r in range(0, wa_ref.shape[0], CAST_ROWS):
            wide = jnp.concatenate([wa_ref[r:r + CAST_ROWS, :], wb_ref[r:r + CAST_ROWS, :]], axis=1)
            w_scr[r:r + CAST_ROWS, :] = wide[:, IN_SHIFT:IN_SHIFT + IN_TN].astype(BF16)

    o_ref[...] = _dot(a_ref[...], w_scr[...]).astype(o_ref.dtype)


def _in_proj(a, w_in, l, tm=2048):
    m, k = a.shape
    per = IN_TN // LANES
    return pl.pallas_call(
        _in_proj_kernel,
        grid=(U_BIG // IN_TN, m // tm),
        in_specs=[pl.BlockSpec((tm, k), lambda j, i: (i, 0)),
                  pl.BlockSpec((None, k, IN_TN), lambda j, i: (l, 0, j)),
                  pl.BlockSpec((None, k, LANES),
                               lambda j, i: (l, 0, per * (jnp.maximum(j, IN_SHIFT_TILE) + 1)))],
        out_specs=pl.BlockSpec((tm, IN_TN), lambda j, i: (i, j)),
        out_shape=jax.ShapeDtypeStruct((m, U_BIG), BF16),
        scratch_shapes=[pltpu.VMEM((k, IN_TN), BF16)],
        compiler_params=_params("arbitrary", "arbitrary"),
        name="in_proj",
    )(a, w_in, w_in)


def _proj_res_norm_kernel(tm, a_ref, w_ref, r_ref, nw_ref, x_ref, h_ref):
    for r in range(0, tm, PROJ_ROWS):
        rows = slice(r, r + PROJ_ROWS)
        xn = r_ref[rows, :] + _dot(a_ref[rows, :], w_ref[...])
        x_ref[rows, :] = xn
        ms = jnp.mean(xn * xn, axis=-1, keepdims=True)
        h_ref[rows, :] = (xn * lax.rsqrt(ms + EPS) * nw_ref[...]).astype(h_ref.dtype)


def _proj_res_norm(a, w3, l, residual, nw3, tm=512, name="proj_res_norm"):
    m, k = a.shape
    n = w3.shape[-1]
    return pl.pallas_call(
        functools.partial(_proj_res_norm_kernel, tm),
        grid=(m // tm,),
        in_specs=[pl.BlockSpec((tm, k), lambda i: (i, 0)),
                  pl.BlockSpec((None, k, n), lambda i: (l, 0, 0)),
                  pl.BlockSpec((tm, n), lambda i: (i, 0)),
                  pl.BlockSpec((None, 1, n), lambda i: (l, 0, 0))],
        out_specs=[pl.BlockSpec((tm, n), lambda i: (i, 0)),
                   pl.BlockSpec((tm, n), lambda i: (i, 0))],
        out_shape=[jax.ShapeDtypeStruct((m, n), F32), jax.ShapeDtypeStruct((m, n), BF16)],
        compiler_params=_params("arbitrary"),
        name=name,
    )(a, w3, residual, nw3)


LOW_COL0 = 3072
DT_COL0 = 5648


def _in_proj_small_kernel(a_ref, wlow_ref, wdt_ref, o_ref, w_scr):
    @pl.when(pl.program_id(0) == 0)
    def _():
        lane = lax.broadcasted_iota(jnp.int32, (1, LANES), 1)
        for r in range(0, w_scr.shape[0], CAST_ROWS):
            rows = slice(r, r + CAST_ROWS)
            dt = pltpu.roll(wdt_ref[rows, :], LANES - DT_COL0 % LANES, 1)
            low = pltpu.roll(wlow_ref[rows, :], 8 - LOW_COL0 % LANES, 1)
            w_scr[rows, :] = jnp.where(lane < 8, dt, jnp.where(lane < 24, low, 0.0)).astype(BF16)

    o_ref[...] = _dot(a_ref[...], w_scr[...])


def _in_proj_small(a, w_in, l, tm=1024):
    m, k = a.shape
    return pl.pallas_call(
        _in_proj_small_kernel,
        grid=(m // tm,),
        in_specs=[pl.BlockSpec((tm, k), lambda i: (i, 0)),
                  pl.BlockSpec((None, k, LANES), lambda i: (l, 0, LOW_COL0 // LANES)),
                  pl.BlockSpec((None, k, LANES), lambda i: (l, 0, DT_COL0 // LANES))],
        out_specs=pl.BlockSpec((tm, U_SMALL), lambda i: (i, 0)),
        out_shape=jax.ShapeDtypeStruct((m, U_SMALL), F32),
        scratch_shapes=[pltpu.VMEM((k, U_SMALL), BF16)],
        compiler_params=_params("arbitrary"),
        name="in_proj_small",
    )(a, w_in, w_in)


def _diff_attn_kernel(lambda_init, seq, tq, q_ref, k_ref, v_ref, gain_ref, lam_ref, subln_ref,
                      o_ref, kn_ref):
    lane = lax.broadcasted_iota(jnp.int32, (1, LANES), 1)
    first = lane < 64

    def halfnorm(t, g):
        sq = t * t
        s1 = jnp.sum(jnp.where(first, sq, 0.0), axis=-1, keepdims=True)
        s2 = jnp.sum(jnp.where(first, 0.0, sq), axis=-1, keepdims=True)
        ms = jnp.where(first, s1, s2) * (1.0 / 64)
        return t * lax.rsqrt(ms + EPS) * g

    kn_ref[...] = halfnorm(k_ref[...].astype(F32), gain_ref[1:2, :]).astype(BF16)
    lam = lam_ref[...]
    l1 = jnp.sum(lam[0:1] * lam[1:2], axis=-1, keepdims=True)
    l2 = jnp.sum(lam[2:3] * lam[3:4], axis=-1, keepdims=True)
    lam_full = jnp.exp(l1) - jnp.exp(l2) + lambda_init

    visible = ((lax.broadcasted_iota(jnp.int32, (2 * tq, tq), 1) >> 6)
               <= ((lax.broadcasted_iota(jnp.int32, (2 * tq, tq), 0) & (tq - 1)) >> 6))

    def scores(i):
        q0 = i * tq
        qn = halfnorm(q_ref[q0:q0 + tq, :].astype(F32), gain_ref[0:1, :]) * (64 ** -0.5 * LOG2E)
        qs = jnp.concatenate([jnp.where(first, qn, 0.0), jnp.where(first, 0.0, qn)],
                             axis=0).astype(BF16)
        s_dg = jnp.where(visible, _dot_nt(qs, kn_ref[q0:q0 + tq, :]), -jnp.inf)
        s_off = _dot_nt(qs, kn_ref[0:q0, :]) if q0 else None
        return s_dg, s_off

    n_tiles = seq // tq
    nxt = scores(0)
    for i in range(n_tiles):
        q0 = i * tq
        s_dg, s_off = nxt
        nxt = scores(i + 1) if i + 1 < n_tiles else None
        m = jnp.max(s_dg, axis=-1, keepdims=True)
        if q0:
            m = jnp.maximum(m, jnp.max(s_off, axis=-1, keepdims=True))
        p_dg = jnp.exp2(s_dg - m)
        den = jnp.sum(p_dg, axis=-1, keepdims=True)
        acc = _dot(p_dg.astype(BF16), v_ref[q0:q0 + tq, :])
        if q0:
            p_off = jnp.exp2(s_off - m)
            den = den + jnp.sum(p_off, axis=-1, keepdims=True)
            acc = acc + _dot(p_off.astype(BF16), v_ref[0:q0, :])
        acc = acc * (1.0 / den)
        o = acc[0:tq] - lam_full * acc[tq:2 * tq]
        ms = jnp.mean(o * o, axis=-1, keepdims=True)
        o = o * lax.rsqrt(ms + EPS) * subln_ref[...] * (1.0 - lambda_init)
        o_ref[q0:q0 + tq, :] = o.astype(o_ref.dtype)


def _diff_attention(u, gain, lam, subln, l, lambda_init, batch, seq, tq=256):
    t = batch * seq
    body = functools.partial(_diff_attn_kernel, lambda_init, seq, tq)
    return pl.pallas_call(
        body,
        grid=(batch, 4),
        in_specs=[pl.BlockSpec((seq, LANES), lambda b, h: (b, h)),
                  pl.BlockSpec((seq, LANES), lambda b, h: (b, 4 + h)),
                  pl.BlockSpec((seq, LANES), lambda b, h: (b, 8 + h)),
                  pl.BlockSpec((None, 2, LANES), lambda b, h: (l, 0, 0)),
                  pl.BlockSpec((None, 4, 64), lambda b, h: (l, 0, 0)),
                  pl.BlockSpec((None, 1, LANES), lambda b, h: (l, 0, 0))],
        out_specs=pl.BlockSpec((seq, LANES), lambda b, h: (b, h)),
        out_shape=jax.ShapeDtypeStruct((t, BRANCH), BF16),
        scratch_shapes=[pltpu.VMEM((seq, LANES), BF16)],
        compiler_params=_params("arbitrary", "arbitrary"),
        name="diff_attention",
    )(u, u, u, gain, lam, subln)


def _gla_kernel(seq, q_ref, k_ref, v_ref, og_ref, small_ref, w2_ref, gkb_ref, nw_ref, o_ref,
                state_ref):
    cs, nb = GLA_CHUNK, GLA_CHUNK // GLA_BLOCK
    ri = lax.broadcasted_iota(jnp.int32, (cs, cs), 0)
    ci = lax.broadcasted_iota(jnp.int32, (cs, cs), 1)
    tri = (ci <= ri).astype(BF16)
    row = lax.broadcasted_iota(jnp.int32, (cs, LANES), 0)
    row_blk = row >> 3
    row_mod = row & 7
    sub3 = lax.broadcasted_iota(jnp.int32, (nb, GLA_BLOCK, LANES), 1)
    lane = lax.broadcasted_iota(jnp.int32, (1, LANES), 1)
    head0 = lane < 64
    r2 = lax.broadcasted_iota(jnp.int32, (2 * cs, cs), 0)
    c2 = lax.broadcasted_iota(jnp.int32, (2 * cs, cs), 1)
    same_blk = ((r2 & (cs - 1)) >> 3) == (c2 >> 3)
    sr = lax.broadcasted_iota(jnp.int32, (2 * LANES, LANES), 0)
    sc = lax.broadcasted_iota(jnp.int32, (2 * LANES, LANES), 1)
    state_mask = (sr >> 7) == (sc >> 6)
    state_ref[...] = jnp.zeros_like(state_ref)

    def both_heads(x):
        m0 = (lax.broadcasted_iota(jnp.int32, (1, x.shape[1]), 1) & 64) == 0
        return jnp.concatenate([jnp.where(m0, x, 0.0), jnp.where(m0, 0.0, x)], axis=0).astype(BF16)

    def group(gi, carry):
        base = gi * (GLA_GROUP * cs)
        rows = [pl.ds(pl.multiple_of(base + n * cs, cs), cs) for n in range(GLA_GROUP)]
        probs = [(n, p) for n in range(GLA_GROUP) for p in range(2)]
        lanes = [slice(p * LANES, (p + 1) * LANES) for p in range(2)]

        gk = []
        for n in range(GLA_GROUP):
            z = _dot(small_ref[rows[n], :].astype(BF16), w2_ref[...]) + gkb_ref[...]
            gk.append((jnp.minimum(z, 0.0) - jnp.log1p(jnp.exp(-jnp.abs(z)))) * (1.0 / 16.0))
        cum = {(n, p): _select_dot(tri, gk[n][:, lanes[p]]) for n, p in probs}

        ops = {}
        for n, p in probs:
            c = cum[n, p]
            q = q_ref[rows[n], lanes[p]].astype(F32) * (64 ** -0.5)
            k = k_ref[rows[n], lanes[p]].astype(F32)
            cum3 = c.reshape(nb, GLA_BLOCK, LANES)
            q3 = q.reshape(nb, GLA_BLOCK, LANES)
            last3 = cum3[:, GLA_BLOCK - 1:GLA_BLOCK, :]
            ref3 = jnp.concatenate([jnp.zeros((1, 1, LANES), F32), last3[:nb - 1]], axis=0)
            refrow = jnp.broadcast_to(ref3, (nb, GLA_BLOCK, LANES)).reshape(cs, LANES)
            qp = q * jnp.exp(c - refrow)
            lhs_off, rhs_off = [], []
            for blk in range(1, nb):
                lhs_off.append(jnp.where(row_blk == blk, qp, 0.0))
                m = blk * GLA_BLOCK
                kd = k[0:m] * jnp.exp(c[m - 1:m, :] - c[0:m])
                rhs_off.append(jnp.concatenate([kd, jnp.zeros((cs - m, LANES), F32)], axis=0))
            lhs_dg, rhs_dg = [], []
            for j in range(GLA_BLOCK):
                dec = jnp.exp(jnp.minimum(cum3 - cum3[:, j:j + 1, :], 0.0))
                lhs_dg.append(jnp.where(sub3 >= j, q3 * dec, 0.0).reshape(cs, LANES))
                rhs_dg.append(jnp.where(row_mod == j, k, 0.0))
            c_last = c[cs - 1:cs, :]
            ops[n, p] = dict(
                lhs_off=both_heads(jnp.concatenate(lhs_off, axis=1)),
                rhs_off=jnp.concatenate(rhs_off, axis=1).astype(BF16),
                lhs_dg=both_heads(jnp.concatenate(lhs_dg, axis=1)),
                rhs_dg=jnp.concatenate(rhs_dg, axis=1).astype(BF16),
                qe=(q * jnp.exp(c)).astype(BF16),
                kdec=(k * jnp.exp(c_last - c)).astype(BF16),
                dec=jnp.exp(c_last))

        attn, upd = {}, {}
        for n, p in probs:
            o = ops[n, p]
            s_off = _dot_nt(o["lhs_off"], o["rhs_off"])
            s_dg = _dot_nt(o["lhs_dg"], o["rhs_dg"])
            attn[n, p] = (s_off + jnp.where(same_blk, s_dg, 0.0)).astype(BF16)
            vv = v_ref[rows[n], p * 2 * LANES:(p + 1) * 2 * LANES]
            upd[n, p] = _dot_tn(vv, o["kdec"])

        inter = {}
        for p in range(2):
            st = state_ref[p]
            for n in range(GLA_GROUP):
                inter[n, p] = _dot_nt(ops[n, p]["qe"], st.astype(BF16))
                st = jnp.where(state_mask, st * ops[n, p]["dec"] + upd[n, p], 0.0)
            state_ref[p] = st

        for n, p in probs:
            for e in range(2):
                cols = slice((2 * p + e) * LANES, (2 * p + e + 1) * LANES)
                o = (_dot(attn[n, p][e * cs:(e + 1) * cs], v_ref[rows[n], cols])
                     + inter[n, p][:, e * LANES:(e + 1) * LANES])
                ms = jnp.mean(o * o, axis=-1, keepdims=True)
                o = o * lax.rsqrt(ms + EPS) * nw_ref[...]
                o_ref[rows[n], cols] = (o * _silu(og_ref[rows[n], cols].astype(F32))
                                        ).astype(o_ref.dtype)
        return carry

    lax.fori_loop(0, seq // (GLA_GROUP * cs), group, 0)


def _gla(u, small, w2p, gkb, nw, l, batch, seq):
    t = batch * seq
    body = functools.partial(_gla_kernel, seq)
    return pl.pallas_call(
        body,
        grid=(batch,),
        in_specs=[pl.BlockSpec((seq, 256), lambda b: (b, 6)),
                  pl.BlockSpec((seq, 256), lambda b: (b, 7)),
                  pl.BlockSpec((seq, 512), lambda b: (b, 4)),
                  pl.BlockSpec((seq, 512), lambda b: (b, 5)),
                  pl.BlockSpec((seq, U_SMALL), lambda b: (b, 0)),
                  pl.BlockSpec((None, U_SMALL, 256), lambda b: (l, 0, 0)),
                  pl.BlockSpec((None, 1, 256), lambda b: (l, 0, 0)),
                  pl.BlockSpec((None, 1, LANES), lambda b: (l, 0, 0))],
        out_specs=pl.BlockSpec((seq, BRANCH), lambda b: (b, 0)),
        out_shape=jax.ShapeDtypeStruct((t, BRANCH), BF16),
        scratch_shapes=[pltpu.VMEM((2, 2 * LANES, LANES), F32)],
        compiler_params=_params("arbitrary"),
        name="gla",
    )(u, u, u, u, small, w2p, gkb, nw)


def _conformer_kernel(seq, tr, a_ref, g_ref, w_ref, b_ref, lnw_ref, lnb_ref, o_ref, gbuf, cbuf):
    halo = 32
    gbuf[0:halo, :] = jnp.zeros((halo, BRANCH), F32)
    for r in range(0, seq, 256):
        gbuf[halo + r:halo + r + 256, :] = (a_ref[r:r + 256, :].astype(F32)
                                            * _sigmoid(g_ref[r:r + 256, :].astype(F32)))

    def conv_tile(i, carry):
        r0 = pl.multiple_of(i * tr, tr)
        for c0 in range(0, BRANCH, LANES):
            win = gbuf[pl.ds(r0, tr + halo), c0:c0 + LANES]
            acc = jnp.broadcast_to(b_ref[:, c0:c0 + LANES], (tr, LANES))
            for j in range(CONV_WIDTH):
                acc = acc + w_ref[j:j + 1, c0:c0 + LANES] * _shift_window(
                    win, halo - (CONV_WIDTH - 1) + j, tr)
            cbuf[pl.ds(r0, tr), c0:c0 + LANES] = acc
        return carry

    lax.fori_loop(0, seq // tr, conv_tile, 0)

    def norm_tile(i, carry):
        r0 = pl.multiple_of(i * tr, tr)
        c = cbuf[pl.ds(r0, tr), :]
        mu = jnp.mean(c, axis=-1, keepdims=True)
        d = c - mu
        var = jnp.mean(d * d, axis=-1, keepdims=True)
        y = d * lax.rsqrt(var + EPS) * lnw_ref[...] + lnb_ref[...]
        o_ref[pl.ds(r0, tr), :] = _silu(y).astype(o_ref.dtype)
        return carry

    lax.fori_loop(0, seq // tr, norm_tile, 0)


def _conformer(u, w, b, lnw, lnb, l, batch, seq, tr=128):
    t = batch * seq
    body = functools.partial(_conformer_kernel, seq, tr)
    return pl.pallas_call(
        body,
        grid=(batch,),
        in_specs=[pl.BlockSpec((seq, BRANCH), lambda i: (i, 6)),
                  pl.BlockSpec((seq, BRANCH), lambda i: (i, 7)),
                  pl.BlockSpec((None, CONV_WIDTH, BRANCH), lambda i: (l, 0, 0)),
                  pl.BlockSpec((None, 1, BRANCH), lambda i: (l, 0, 0)),
                  pl.BlockSpec((None, 1, BRANCH), lambda i: (l, 0, 0)),
                  pl.BlockSpec((None, 1, BRANCH), lambda i: (l, 0, 0))],
        out_specs=pl.BlockSpec((seq, BRANCH), lambda i: (i, 0)),
        out_shape=jax.ShapeDtypeStruct((t, BRANCH), BF16),
        scratch_shapes=[pltpu.VMEM((seq + 32, BRANCH), F32), pltpu.VMEM((seq, BRANCH), F32)],
        compiler_params=_params("arbitrary"),
        name="conformer_conv",
    )(u, u, w, b, lnw, lnb)


def _ssd_kernel(seq, z_ref, x_ref, bc_ref, small_ref, cw_ref, cb_ref, dtb_ref, alog_ref, dsk_ref,
                nw_ref, o_ref, xbuf, state_ref):
    cs = SSD_CHUNK
    halo = SUBLANES
    xbuf[0:halo, :] = jnp.zeros((halo, 2 * BRANCH), F32)
    for r in range(0, seq, 256):
        xbuf[halo + r:halo + r + 256, 0:BRANCH] = x_ref[r:r + 256, :].astype(F32)
        xbuf[halo + r:halo + r + 256, BRANCH:2 * BRANCH] = bc_ref[r:r + 256, :].astype(F32)
    state_ref[...] = jnp.zeros_like(state_ref)

    ri = lax.broadcasted_iota(jnp.int32, (cs, cs), 0)
    ci = lax.broadcasted_iota(jnp.int32, (cs, cs), 1)
    causal = ci <= ri
    tri = causal.astype(BF16)
    er = lax.broadcasted_iota(jnp.int32, (LANES, BRANCH), 0)
    ec = lax.broadcasted_iota(jnp.int32, (LANES, BRANCH), 1)
    expand = ((ec >> 6) == er).astype(BF16)
    lane = lax.broadcasted_iota(jnp.int32, (1, LANES), 1)
    head0 = lane < 64
    a_neg = -jnp.exp(alog_ref[...])

    def chunk(c, carry):
        r0 = pl.multiple_of(c * cs, cs)
        rows = pl.ds(r0, cs)
        win = xbuf[pl.ds(r0, cs + halo), :]
        conv = jnp.broadcast_to(cb_ref[...], (cs, 2 * BRANCH))
        for j in range(SSD_CONV):
            conv = conv + cw_ref[j:j + 1, :] * _shift_window(win, halo - (SSD_CONV - 1) + j, cs)
        xc = _silu(conv)
        xs = xc[:, 0:BRANCH]
        dt = _softplus(small_ref[rows, :] + dtb_ref[...])
        cum = _select_dot(tri, dt * a_neg)
        cum_t = cum.T
        ecum = jnp.exp(cum)
        cum_last = cum[cs - 1:cs, :]
        dt_x = _dot_select(dt, expand)
        ecum_x = _dot_select(ecum, expand)
        dte_x = _dot_select(jnp.exp(cum_last - cum), expand)
        xdt = xs * dt_x
        ys = []
        for g in range(2):
            bm = xc[:, BRANCH + g * LANES:BRANCH + (g + 1) * LANES].astype(BF16)
            cm = xc[:, BRANCH + 256 + g * LANES:BRANCH + 256 + (g + 1) * LANES].astype(BF16)
            cb = _dot_nt(cm, bm)
            gl = slice(g * 256, (g + 1) * 256)
            for pr in range(2):
                pl_ = slice(g * 256 + pr * LANES, g * 256 + (pr + 1) * LANES)
                xp = xdt[:, pl_]
                y = None
                for e in range(2):
                    h = g * 4 + pr * 2 + e
                    seg = jnp.where(causal, cum[:, h:h + 1] - cum_t[h:h + 1, :], -jnp.inf)
                    m = (cb * jnp.exp(seg)).astype(BF16)
                    xh = jnp.where(head0 if e == 0 else ~head0, xp, 0.0).astype(BF16)
                    t = _dot(m, xh)
                    y = t if y is None else y + t
                ys.append(y)
            st = state_ref[g]
            y_off = _dot(cm, st.astype(BF16)) * ecum_x[:, gl]
            ys[2 * g] = ys[2 * g] + y_off[:, 0:LANES]
            ys[2 * g + 1] = ys[2 * g + 1] + y_off[:, LANES:2 * LANES]
            upd = _dot_tn(bm, (xdt[:, gl] * dte_x[:, gl]).astype(BF16))
            state_ref[g] = st * ecum_x[cs - 1:cs, gl] + upd
        y = jnp.concatenate(ys, axis=1) + dsk_ref[...] * xs
        y = y * _silu(z_ref[rows, :].astype(F32))
        outs = []
        for g in range(2):
            yg = y[:, g * 256:(g + 1) * 256]
            ms = jnp.mean(yg * yg, axis=-1, keepdims=True)
            outs.append(yg * lax.rsqrt(ms + EPS))
        o_ref[rows, :] = (jnp.concatenate(outs, axis=1) * nw_ref[...]).astype(o_ref.dtype)
        return carry

    lax.fori_loop(0, seq // cs, chunk, 0)


def _ssd(u, small, cw, cb, dtb, alog, dsk, nw, l, batch, seq):
    t = batch * seq
    body = functools.partial(_ssd_kernel, seq)
    vec = lambda n: pl.BlockSpec((None, 1, n), lambda i: (l, 0, 0))
    return pl.pallas_call(
        body,
        grid=(batch,),
        in_specs=[pl.BlockSpec((seq, BRANCH), lambda i: (i, 8)),
                  pl.BlockSpec((seq, BRANCH), lambda i: (i, 9)),
                  pl.BlockSpec((seq, BRANCH), lambda i: (i, 10)),
                  pl.BlockSpec((seq, U_SMALL), lambda i: (i, 0)),
                  pl.BlockSpec((None, SSD_CONV, 2 * BRANCH), lambda i: (l, 0, 0)),
                  vec(2 * BRANCH), vec(LANES), vec(LANES), vec(BRANCH), vec(BRANCH)],
        out_specs=pl.BlockSpec((seq, BRANCH), lambda i: (i, 0)),
        out_shape=jax.ShapeDtypeStruct((t, BRANCH), BF16),
        scratch_shapes=[pltpu.VMEM((seq + SUBLANES, 2 * BRANCH), F32),
                        pltpu.VMEM((2, LANES, 256), F32)],
        compiler_params=_params("arbitrary"),
        name="ssd",
    )(u, u, u, small, cw, cb, dtb, alog, dsk, nw)


def _merge_kernel(h_ref, ya_ref, yb_ref, yc_ref, yd_ref, wg_ref, wb_ref, bg_ref, o_ref,
                  wg_scr, wb_scr):
    @pl.when(pl.program_id(1) == 0)
    def _():
        for i in range(4):
            _cast_rows(wg_scr.at[i], wg_ref.at[i])
            _cast_rows(wb_scr.at[i], wb_ref.at[i])

    h = h_ref[...]
    acc = None
    for i, y_ref in enumerate((ya_ref, yb_ref, yc_ref, yd_ref)):
        gate = _sigmoid(_dot(h, wg_scr[i]) + bg_ref[i:i + 1, :])
        term = gate * _dot(y_ref[...], wb_scr[i])
        acc = term if acc is None else acc + term
    o_ref[...] = acc.astype(o_ref.dtype)


def _merge(h, ys, wg, wb, bg, l, tm=1024, tn=256):
    m, d = h.shape
    yspec = pl.BlockSpec((tm, BRANCH), lambda j, i: (i, 0))
    return pl.pallas_call(
        _merge_kernel,
        grid=(d // tn, m // tm),
        in_specs=[pl.BlockSpec((tm, d), lambda j, i: (i, 0)), yspec, yspec, yspec, yspec,
                  pl.BlockSpec((None, 4, d, tn), lambda j, i: (l, 0, 0, j)),
                  pl.BlockSpec((None, 4, BRANCH, tn), lambda j, i: (l, 0, 0, j)),
                  pl.BlockSpec((None, 4, tn), lambda j, i: (l, 0, j))],
        out_specs=pl.BlockSpec((tm, tn), lambda j, i: (i, j)),
        out_shape=jax.ShapeDtypeStruct((m, d), BF16),
        scratch_shapes=[pltpu.VMEM((4, d, tn), BF16), pltpu.VMEM((4, BRANCH, tn), BF16)],
        compiler_params=_params("arbitrary", "arbitrary"),
        name="gated_merge",
    )(h, *ys, wg, wb, bg)


def _ffn_up_kernel(n_tiles, row_tiles, tiles_per_seq, tm, h_ref, wg_ref, wv_ref, cwg_ref, cwv_ref,
                   cbg_ref, cbv_ref, o_ref, raw_g, raw_v, ghist, vhist, wg_scr, wv_scr):
    halo = SUBLANES
    sub = FFN_ROWS
    s = pl.program_id(0)
    cur = jnp.minimum(s, n_tiles - 1)
    prev = jnp.maximum(s - 1, 0)
    slot = s & 1

    @pl.when(s == 0)
    def _():
        raw_g[1] = jnp.zeros(raw_g.shape[1:], F32)
        raw_v[1] = jnp.zeros(raw_v.shape[1:], F32)

    @pl.when((cur % row_tiles == 0) & (s < n_tiles))
    def _():
        _cast_rows(wg_scr, wg_ref)
        _cast_rows(wv_scr, wv_ref)

    @pl.when(prev % tiles_per_seq == 0)
    def _():
        ghist[...] = jnp.zeros_like(ghist)
        vhist[...] = jnp.zeros_like(vhist)

    def conv(raw, old, hist, cw_ref, cb_ref, r, cols):
        if r == 0:
            win = jnp.concatenate([hist[:, cols], raw[old, 0:EP_ROWS, cols]], axis=0)
        else:
            win = raw[old, r - halo:r + EP_ROWS, cols]
        out = jnp.broadcast_to(cb_ref[:, cols], (EP_ROWS, LANES))
        for j in range(FFN_CONV):
            out = out + cw_ref[j:j + 1, cols] * _shift_window(win, halo - (FFN_CONV - 1) + j,
                                                              EP_ROWS)
        return out

    def step(new, old):
        tn = raw_g.shape[2]
        for r0 in range(0, tm, sub):
            h = h_ref[r0:r0 + sub, :]
            pieces = [(r, c) for r in range(r0, r0 + sub, EP_ROWS) for c in range(0, tn, LANES)]
            dots = [(raw, w, c) for c in range(0, tn, MXU_COLS)
                    for raw, w in ((raw_g, wg_scr), (raw_v, wv_scr))]
            per_dot = len(pieces) // len(dots)
            for n, (raw, w, c) in enumerate(dots):
                raw[new, r0:r0 + sub, c:c + MXU_COLS] = _dot(h, w[:, c:c + MXU_COLS])
                for r, pc in pieces[n * per_dot:(n + 1) * per_dot]:
                    cols = slice(pc, pc + LANES)
                    gate = conv(raw_g, old, ghist, cwg_ref, cbg_ref, r, cols)
                    val = conv(raw_v, old, vhist, cwv_ref, cbv_ref, r, cols)
                    o_ref[r:r + EP_ROWS, cols] = (_silu(gate) * val).astype(o_ref.dtype)
        ghist[...] = raw_g[old, tm - halo:tm, :]
        vhist[...] = raw_v[old, tm - halo:tm, :]

    @pl.when(slot == 0)
    def _():
        step(0, 1)

    @pl.when(slot == 1)
    def _():
        step(1, 0)


def _ffn_up(h, w_up, cw, cb, l, seq, tm=1024, tn=512):
    m, d = h.shape
    nt = D_FF // tn
    rt = m // tm
    n_tiles = nt * rt
    body = functools.partial(_ffn_up_kernel, n_tiles, rt, seq // tm, tm)
    cur = lambda s: jnp.minimum(s, n_tiles - 1)
    prev = lambda s: jnp.maximum(s - 1, 0)
    return pl.pallas_call(
        body,
        grid=(n_tiles + 1,),
        in_specs=[pl.BlockSpec((tm, d), lambda s: (cur(s) % rt, 0)),
                  pl.BlockSpec((None, d, tn), lambda s: (l, 0, cur(s) // rt)),
                  pl.BlockSpec((None, d, tn), lambda s: (l, 0, cur(s) // rt + nt)),
                  pl.BlockSpec((None, FFN_CONV, tn), lambda s: (l, 0, prev(s) // rt)),
                  pl.BlockSpec((None, FFN_CONV, tn), lambda s: (l, 0, prev(s) // rt + nt)),
                  pl.BlockSpec((None, 1, tn), lambda s: (l, 0, prev(s) // rt)),
                  pl.BlockSpec((None, 1, tn), lambda s: (l, 0, prev(s) // rt + nt))],
        out_specs=pl.BlockSpec((tm, tn), lambda s: (prev(s) % rt, prev(s) // rt)),
        out_shape=jax.ShapeDtypeStruct((m, D_FF), BF16),
        scratch_shapes=[pltpu.VMEM((2, tm, tn), F32), pltpu.VMEM((2, tm, tn), F32),
                        pltpu.VMEM((SUBLANES, tn), F32), pltpu.VMEM((SUBLANES, tn), F32),
                        pltpu.VMEM((d, tn), BF16), pltpu.VMEM((d, tn), BF16)],
        compiler_params=_params("arbitrary"),
        name="ffn_up_conv_gate",
    )(h, w_up, w_up, cw, cw, cb, cb)


def kernel(x, mix_norm, w_in, diff_qk_norm, diff_lambda, diff_subln, gla_gk_w2, gla_gk_b, gla_norm, conv_dw_w, conv_dw_b, conv_ln_w, conv_ln_b, ssd_conv_w, ssd_conv_b, ssd_dt_bias, ssd_a_log, ssd_d, ssd_norm, w_branch, w_gate, b_gate, w_out, ffn_norm, ffn_w_up, ffn_conv_w, ffn_conv_b, ffn_w_down):
    batch, seq, d = x.shape
    depth = w_in.shape[0]
    t = batch * seq
    xf = x.reshape(t, d)

    w2_pad = jnp.zeros((depth, U_SMALL, 256), F32).at[:, 8:24, :].set(gla_gk_w2).astype(BF16)
    pad8 = lambda v: jnp.pad(v, ((0, 0), (0, LANES - 8)))[:, None, :]
    row = lambda v: v[:, None, :]
    qk_gain = jnp.tile(diff_qk_norm, (1, 1, 2))
    dskip = jnp.repeat(ssd_d, 64, axis=-1)[:, None, :]
    w_out_b = w_out.astype(BF16)

    for l in range(depth):
        lambda_init = 0.8 - 0.6 * math.exp(-0.3 * l)
        h = _rmsnorm(xf, row(mix_norm), l)
        u = _in_proj(h, w_in, l)
        small = _in_proj_small(h, w_in, l)
        ya = _diff_attention(u, qk_gain, diff_lambda, row(diff_subln), l, lambda_init, batch, seq)
        yb = _gla(u, small, w2_pad, row(gla_gk_b), row(gla_norm), l, batch, seq)
        yc = _conformer(u, conv_dw_w, row(conv_dw_b), row(conv_ln_w), row(conv_ln_b), l, batch, seq)
        yd = _ssd(u, small, ssd_conv_w, row(ssd_conv_b), pad8(ssd_dt_bias), pad8(ssd_a_log), dskip,
                  row(ssd_norm), l, batch, seq)
        merged = _merge(h, (ya, yb, yc, yd), w_gate, w_branch, b_gate, l)
        xf, h2 = _proj_res_norm(merged, w_out_b, l, xf, row(ffn_norm), name="out_proj_norm")
        act = _ffn_up(h2, ffn_w_up, ffn_conv_w, row(ffn_conv_b), l, seq)
        xf = _matmul(act, ffn_w_down, l, tm=512, tn=512, out_dtype=F32, residual=xf, name="ffn_down")
    return xf.reshape(batch, seq, d)
```

```python
import functools
import math

import jax
import jax.numpy as jnp
from jax import lax
from jax.experimental import pallas as pl
from jax.experimental.pallas import tpu as pltpu

F32 = jnp.float32
BF16 = jnp.bfloat16

EPS = 1e-6
LOG2E = 1.4426950408889634
D_MODEL = 2048
BRANCH = 512
D_FF = 5632
U_BIG = 5632
U_SMALL = 128
LANES = 128
SUBLANES = 8
VMEM_LIMIT = 56 * 1024 * 1024

GLA_CHUNK = 64
GLA_BLOCK = 8
GLA_GROUP = 4
SSD_CHUNK = 256
CONV_WIDTH = 31
SSD_CONV = 4
FFN_CONV = 3
FFN_ROWS = 128
PROJ_ROWS = 256
CAST_ROWS = 256
EP_ROWS = 64
MXU_COLS = 256


def _params(*sem):
    return pltpu.CompilerParams(dimension_semantics=sem, vmem_limit_bytes=VMEM_LIMIT)


def _sigmoid(x):
    return 0.5 * jnp.tanh(0.5 * x) + 0.5


def _silu(x):
    half = 0.5 * x
    return half * jnp.tanh(half) + half


def _softplus(x):
    return jnp.maximum(x, 0.0) + jnp.log1p(jnp.exp(-jnp.abs(x)))


def _split3(x):
    hi = x.astype(BF16)
    r1 = x - hi.astype(F32)
    mid = r1.astype(BF16)
    lo = (r1 - mid.astype(F32)).astype(BF16)
    return hi, mid, lo


def _dot(a, b):
    return jnp.dot(a, b, preferred_element_type=F32)


def _dot_nt(a, b):
    return lax.dot_general(a, b, (((1,), (1,)), ((), ())), preferred_element_type=F32)


def _dot_tn(a, b):
    return lax.dot_general(a, b, (((0,), (0,)), ((), ())), preferred_element_type=F32)


def _select_dot(sel, x):
    hi, mid, lo = _split3(x)
    return _dot(sel, hi) + _dot(sel, mid) + _dot(sel, lo)


def _dot_select(x, sel):
    hi, mid, lo = _split3(x)
    return _dot(hi, sel) + _dot(mid, sel) + _dot(lo, sel)


def _shift_window(win, shift, rows):
    total = win.shape[0]
    if shift % SUBLANES == 0:
        return win[shift:shift + rows]
    rolled = pltpu.roll(win, total - (shift % SUBLANES), 0)
    base = shift - shift % SUBLANES
    return rolled[base:base + rows]


def _rmsnorm_kernel(x_ref, w_ref, o_ref):
    x = x_ref[...]
    ms = jnp.mean(x * x, axis=-1, keepdims=True)
    o_ref[...] = (x * lax.rsqrt(ms + EPS) * w_ref[...]).astype(o_ref.dtype)


def _rmsnorm(x, w3, l, tm=512):
    m, d = x.shape
    return pl.pallas_call(
        _rmsnorm_kernel,
        grid=(m // tm,),
        in_specs=[pl.BlockSpec((tm, d), lambda i: (i, 0)),
                  pl.BlockSpec((None, 1, d), lambda i: (l, 0, 0))],
        out_specs=pl.BlockSpec((tm, d), lambda i: (i, 0)),
        out_shape=jax.ShapeDtypeStruct((m, d), BF16),
        compiler_params=_params("arbitrary"),
        name="rmsnorm",
    )(x, w3)


def _matmul_kernel(a_ref, w_ref, o_ref):
    o_ref[...] = _dot(a_ref[...], w_ref[...]).astype(o_ref.dtype)


def _matmul_res_kernel(a_ref, w_ref, r_ref, o_ref):
    o_ref[...] = r_ref[...] + _dot(a_ref[...], w_ref[...])


def _cast_rows(dst_ref, src_ref):
    for r in range(0, src_ref.shape[0], CAST_ROWS):
        dst_ref[r:r + CAST_ROWS, :] = src_ref[r:r + CAST_ROWS, :].astype(BF16)


def _matmul_cast_res_kernel(a_ref, w_ref, r_ref, o_ref, w_scr):
    @pl.when(pl.program_id(1) == 0)
    def _():
        _cast_rows(w_scr, w_ref)

    o_ref[...] = r_ref[...] + _dot(a_ref[...], w_scr[...])


def _matmul(a, w3, l, *, tm, tn, out_dtype, residual=None, name="matmul"):
    m, k = a.shape
    n = w3.shape[-1]
    in_specs = [pl.BlockSpec((tm, k), lambda j, i: (i, 0)),
                pl.BlockSpec((None, k, tn), lambda j, i: (l, 0, j))]
    args = [a, w3]
    body = _matmul_kernel
    scratch = []
    if residual is not None:
        in_specs.append(pl.BlockSpec((tm, tn), lambda j, i: (i, j)))
        args.append(residual)
        body = _matmul_res_kernel
        if w3.dtype == F32:
            body = _matmul_cast_res_kernel
            scratch = [pltpu.VMEM((k, tn), BF16)]
    return pl.pallas_call(
        body,
        grid=(n // tn, m // tm),
        in_specs=in_specs,
        out_specs=pl.BlockSpec((tm, tn), lambda j, i: (i, j)),
        out_shape=jax.ShapeDtypeStruct((m, n), out_dtype),
        scratch_shapes=scratch,
        compiler_params=_params("arbitrary", "arbitrary"),
        name=name,
    )(*args)


IN_TN = 512
IN_SHIFT_TILE = 3072 // IN_TN
IN_SHIFT = 16


def _in_proj_kernel(a_ref, wa_ref, wb_ref, o_ref, w_scr):
    j = pl.program_id(0)
    first_row_tile = pl.program_id(1) == 0

    @pl.when(first_row_tile & (j < IN_SHIFT_TILE))
    def _():
        _cast_rows(w_scr, wa_ref)

    @pl.when(first_row_tile & (j >= IN_SHIFT_TILE))
    def _():
        for r in range(0, wa_ref.shape[0], CAST_ROWS):
            wide = jnp.concatenate([wa_ref[r:r + CAST_ROWS, :], wb_ref[r:r + CAST_ROWS, :]], axis=1)
            w_scr[r:r + CAST_ROWS, :] = wide[:, IN_SHIFT:IN_SHIFT + IN_TN].astype(BF16)

    o_ref[...] = _dot(a_ref[...], w_scr[...]).astype(o_ref.dtype)


def _in_proj(a, w_in, l, tm=2048):
    m, k = a.shape
    per = IN_TN // LANES
    return pl.pallas_call(
        _in_proj_kernel,
        grid=(U_BIG // IN_TN, m // tm),
        in_specs=[pl.BlockSpec((tm, k), lambda j, i: (i, 0)),
                  pl.BlockSpec((None, k, IN_TN), lambda j, i: (l, 0, j)),
                  pl.BlockSpec((None, k, LANES),
                               lambda j, i: (l, 0, per * (jnp.maximum(j, IN_SHIFT_TILE) + 1)))],
        out_specs=pl.BlockSpec((tm, IN_TN), lambda j, i: (i, j)),
        out_shape=jax.ShapeDtypeStruct((m, U_BIG), BF16),
        scratch_shapes=[pltpu.VMEM((k, IN_TN), BF16)],
        compiler_params=_params("arbitrary", "arbitrary"),
        name="in_proj",
    )(a, w_in, w_in)


def _proj_res_norm_kernel(tm, a_ref, w_ref, r_ref, nw_ref, x_ref, h_ref):
    for r in range(0, tm, PROJ_ROWS):
        rows = slice(r, r + PROJ_ROWS)
        xn = r_ref[rows, :] + _dot(a_ref[rows, :], w_ref[...])
        x_ref[rows, :] = xn
        ms = jnp.mean(xn * xn, axis=-1, keepdims=True)
        h_ref[rows, :] = (xn * lax.rsqrt(ms + EPS) * nw_ref[...]).astype(h_ref.dtype)


def _proj_res_norm(a, w3, l, residual, nw3, tm=512, name="proj_res_norm"):
    m, k = a.shape
    n = w3.shape[-1]
    return pl.pallas_call(
        functools.partial(_proj_res_norm_kernel, tm),
        grid=(m // tm,),
        in_specs=[pl.BlockSpec((tm, k), lambda i: (i, 0)),
                  pl.BlockSpec((None, k, n), lambda i: (l, 0, 0)),
                  pl.BlockSpec((tm, n), lambda i: (i, 0)),
                  pl.BlockSpec((None, 1, n), lambda i: (l, 0, 0))],
        out_specs=[pl.BlockSpec((tm, n), lambda i: (i, 0)),
                   pl.BlockSpec((tm, n), lambda i: (i, 0))],
        out_shape=[jax.ShapeDtypeStruct((m, n), F32), jax.ShapeDtypeStruct((m, n), BF16)],
        compiler_params=_params("arbitrary"),
        name=name,
    )(a, w3, residual, nw3)


LOW_COL0 = 3072
DT_COL0 = 5648


def _in_proj_small_kernel(a_ref, wlow_ref, wdt_ref, o_ref, w_scr):
    @pl.when(pl.program_id(0) == 0)
    def _():
        lane = lax.broadcasted_iota(jnp.int32, (1, LANES), 1)
        for r in range(0, w_scr.shape[0], CAST_ROWS):
            rows = slice(r, r + CAST_ROWS)
            dt = pltpu.roll(wdt_ref[rows, :], LANES - DT_COL0 % LANES, 1)
            low = pltpu.roll(wlow_ref[rows, :], 8 - LOW_COL0 % LANES, 1)
            w_scr[rows, :] = jnp.where(lane < 8, dt, jnp.where(lane < 24, low, 0.0)).astype(BF16)

    o_ref[...] = _dot(a_ref[...], w_scr[...])


def _in_proj_small(a, w_in, l, tm=1024):
    m, k = a.shape
    return pl.pallas_call(
        _in_proj_small_kernel,
        grid=(m // tm,),
        in_specs=[pl.BlockSpec((tm, k), lambda i: (i, 0)),
                  pl.BlockSpec((None, k, LANES), lambda i: (l, 0, LOW_COL0 // LANES)),
                  pl.BlockSpec((None, k, LANES), lambda i: (l, 0, DT_COL0 // LANES))],
        out_specs=pl.BlockSpec((tm, U_SMALL), lambda i: (i, 0)),
        out_shape=jax.ShapeDtypeStruct((m, U_SMALL), F32),
        scratch_shapes=[pltpu.VMEM((k, U_SMALL), BF16)],
        compiler_params=_params("arbitrary"),
        name="in_proj_small",
    )(a, w_in, w_in)


def _diff_attn_kernel(lambda_init, seq, tq, q_ref, k_ref, v_ref, gain_ref, lam_ref, subln_ref,
                      o_ref, kn_ref, va_ref):
    lane = lax.broadcasted_iota(jnp.int32, (1, LANES), 1)
    first = lane < 64

    def halfnorm(t, g):
        sq = t * t
        s1 = jnp.sum(jnp.where(first, sq, 0.0), axis=-1, keepdims=True)
        s2 = jnp.sum(jnp.where(first, 0.0, sq), axis=-1, keepdims=True)
        ms = jnp.where(first, s1, s2) * (1.0 / 64)
        return t * lax.rsqrt(ms + EPS) * g

    kn_ref[...] = halfnorm(k_ref[...].astype(F32), gain_ref[1:2, :]).astype(BF16)
    va_ref[:, 0:LANES] = v_ref[...]
    va_ref[:, LANES:2 * LANES] = jnp.ones((seq, LANES), BF16)
    lam = lam_ref[...]
    l1 = jnp.sum(lam[0:1] * lam[1:2], axis=-1, keepdims=True)
    l2 = jnp.sum(lam[2:3] * lam[3:4], axis=-1, keepdims=True)
    lam_full = jnp.exp(l1) - jnp.exp(l2) + lambda_init

    visible = ((lax.broadcasted_iota(jnp.int32, (2 * tq, tq), 1) >> 6)
               <= ((lax.broadcasted_iota(jnp.int32, (2 * tq, tq), 0) & (tq - 1)) >> 6))

    def scores(i):
        q0 = i * tq
        qn = halfnorm(q_ref[q0:q0 + tq, :].astype(F32), gain_ref[0:1, :]) * (64 ** -0.5 * LOG2E)
        qs = jnp.concatenate([jnp.where(first, qn, 0.0), jnp.where(first, 0.0, qn)],
                             axis=0).astype(BF16)
        s_dg = jnp.where(visible, _dot_nt(qs, kn_ref[q0:q0 + tq, :]), -jnp.inf)
        s_off = _dot_nt(qs, kn_ref[0:q0, :]) if q0 else None
        return s_dg, s_off

    n_tiles = seq // tq
    nxt = scores(0)
    for i in range(n_tiles):
        q0 = i * tq
        s_dg, s_off = nxt
        nxt = scores(i + 1) if i + 1 < n_tiles else None
        m = jnp.max(s_dg, axis=-1, keepdims=True)
        if q0:
            m = jnp.maximum(m, jnp.max(s_off, axis=-1, keepdims=True))
        acc = _dot(jnp.exp2(s_dg - m).astype(BF16), va_ref[q0:q0 + tq, :])
        if q0:
            acc = acc + _dot(jnp.exp2(s_off - m).astype(BF16), va_ref[0:q0, :])
        acc = acc[:, 0:LANES] * (1.0 / acc[:, LANES:2 * LANES])
        o = acc[0:tq] - lam_full * acc[tq:2 * tq]
        ms = jnp.mean(o * o, axis=-1, keepdims=True)
        o = o * lax.rsqrt(ms + EPS) * subln_ref[...] * (1.0 - lambda_init)
        o_ref[q0:q0 + tq, :] = o.astype(o_ref.dtype)


def _diff_attention(u, gain, lam, subln, l, lambda_init, batch, seq, tq=256):
    t = batch * seq
    body = functools.partial(_diff_attn_kernel, lambda_init, seq, tq)
    return pl.pallas_call(
        body,
        grid=(batch, 4),
        in_specs=[pl.BlockSpec((seq, LANES), lambda b, h: (b, h)),
                  pl.BlockSpec((seq, LANES), lambda b, h: (b, 4 + h)),
                  pl.BlockSpec((seq, LANES), lambda b, h: (b, 8 + h)),
                  pl.BlockSpec((None, 2, LANES), lambda b, h: (l, 0, 0)),
                  pl.BlockSpec((None, 4, 64), lambda b, h: (l, 0, 0)),
                  pl.BlockSpec((None, 1, LANES), lambda b, h: (l, 0, 0))],
        out_specs=pl.BlockSpec((seq, LANES), lambda b, h: (b, h)),
        out_shape=jax.ShapeDtypeStruct((t, BRANCH), BF16),
        scratch_shapes=[pltpu.VMEM((seq, LANES), BF16), pltpu.VMEM((seq, 2 * LANES), BF16)],
        compiler_params=_params("arbitrary", "arbitrary"),
        name="diff_attention",
    )(u, u, u, gain, lam, subln)


def _gla_kernel(seq, q_ref, k_ref, v_ref, og_ref, small_ref, w2_ref, gkb_ref, nw_ref, o_ref,
                state_ref):
    cs, nb = GLA_CHUNK, GLA_CHUNK // GLA_BLOCK
    ri = lax.broadcasted_iota(jnp.int32, (cs, cs), 0)
    ci = lax.broadcasted_iota(jnp.int32, (cs, cs), 1)
    tri = (ci <= ri).astype(BF16)
    row = lax.broadcasted_iota(jnp.int32, (cs, LANES), 0)
    row_blk = row >> 3
    row_mod = row & 7
    sub3 = lax.broadcasted_iota(jnp.int32, (nb, GLA_BLOCK, LANES), 1)
    lane = lax.broadcasted_iota(jnp.int32, (1, LANES), 1)
    head0 = lane < 64
    r2 = lax.broadcasted_iota(jnp.int32, (2 * cs, cs), 0)
    c2 = lax.broadcasted_iota(jnp.int32, (2 * cs, cs), 1)
    same_blk = ((r2 & (cs - 1)) >> 3) == (c2 >> 3)
    sr = lax.broadcasted_iota(jnp.int32, (2 * LANES, LANES), 0)
    sc = lax.broadcasted_iota(jnp.int32, (2 * LANES, LANES), 1)
    state_mask = (sr >> 7) == (sc >> 6)
    state_ref[...] = jnp.zeros_like(state_ref)

    def both_heads(x):
        m0 = (lax.broadcasted_iota(jnp.int32, (1, x.shape[1]), 1) & 64) == 0
        return jnp.concatenate([jnp.where(m0, x, 0.0), jnp.where(m0, 0.0, x)], axis=0).astype(BF16)

    def group(gi, carry):
        base = gi * (GLA_GROUP * cs)
        rows = [pl.ds(pl.multiple_of(base + n * cs, cs), cs) for n in range(GLA_GROUP)]
        probs = [(n, p) for n in range(GLA_GROUP) for p in range(2)]
        lanes = [slice(p * LANES, (p + 1) * LANES) for p in range(2)]

        gk = []
        for n in range(GLA_GROUP):
            z = _dot(small_ref[rows[n], :].astype(BF16), w2_ref[...]) + gkb_ref[...]
            gk.append((jnp.minimum(z, 0.0) - jnp.log1p(jnp.exp(-jnp.abs(z)))) * (LOG2E / 16.0))
        cum = {(n, p): _select_dot(tri, gk[n][:, lanes[p]]) for n, p in probs}

        ops = {}
        for n, p in probs:
            c = cum[n, p]
            q = q_ref[rows[n], lanes[p]].astype(F32) * (64 ** -0.5)
            k = k_ref[rows[n], lanes[p]].astype(F32)
            cum3 = c.reshape(nb, GLA_BLOCK, LANES)
            q3 = q.reshape(nb, GLA_BLOCK, LANES)
            last3 = cum3[:, GLA_BLOCK - 1:GLA_BLOCK, :]
            ref3 = jnp.concatenate([jnp.zeros((1, 1, LANES), F32), last3[:nb - 1]], axis=0)
            refrow = jnp.broadcast_to(ref3, (nb, GLA_BLOCK, LANES)).reshape(cs, LANES)
            qp = q * jnp.exp2(c - refrow)
            lhs_off, rhs_off = [], []
            for blk in range(1, nb):
                lhs_off.append(jnp.where(row_blk == blk, qp, 0.0))
                m = blk * GLA_BLOCK
                kd = k[0:m] * jnp.exp2(c[m - 1:m, :] - c[0:m])
                rhs_off.append(jnp.concatenate([kd, jnp.zeros((cs - m, LANES), F32)], axis=0))
            lhs_dg, rhs_dg = [], []
            for j in range(GLA_BLOCK):
                dec = jnp.exp2(jnp.minimum(cum3 - cum3[:, j:j + 1, :], 0.0))
                lhs_dg.append(jnp.where(sub3 >= j, q3 * dec, 0.0).reshape(cs, LANES))
                rhs_dg.append(jnp.where(row_mod == j, k, 0.0))
            c_last = c[cs - 1:cs, :]
            ops[n, p] = dict(
                lhs_off=both_heads(jnp.concatenate(lhs_off, axis=1)),
                rhs_off=jnp.concatenate(rhs_off, axis=1).astype(BF16),
                lhs_dg=both_heads(jnp.concatenate(lhs_dg, axis=1)),
                rhs_dg=jnp.concatenate(rhs_dg, axis=1).astype(BF16),
                qe=(q * jnp.exp2(c)).astype(BF16),
                kdec=(k * jnp.exp2(c_last - c)).astype(BF16),
                dec=jnp.exp2(c_last))

        attn, upd = {}, {}
        for n, p in probs:
            o = ops[n, p]
            s_off = _dot_nt(o["lhs_off"], o["rhs_off"])
            s_dg = _dot_nt(o["lhs_dg"], o["rhs_dg"])
            attn[n, p] = (s_off + jnp.where(same_blk, s_dg, 0.0)).astype(BF16)
            vv = v_ref[rows[n], p * 2 * LANES:(p + 1) * 2 * LANES]
            upd[n, p] = _dot_tn(vv, o["kdec"])

        inter = {}
        for p in range(2):
            st = state_ref[p]
            for n in range(GLA_GROUP):
                inter[n, p] = _dot_nt(ops[n, p]["qe"], st.astype(BF16))
                st = jnp.where(state_mask, st * ops[n, p]["dec"] + upd[n, p], 0.0)
            state_ref[p] = st

        for n, p in probs:
            for e in range(2):
                cols = slice((2 * p + e) * LANES, (2 * p + e + 1) * LANES)
                o = (_dot(attn[n, p][e * cs:(e + 1) * cs], v_ref[rows[n], cols])
                     + inter[n, p][:, e * LANES:(e + 1) * LANES])
                ms = jnp.mean(o * o, axis=-1, keepdims=True)
                o = o * lax.rsqrt(ms + EPS) * nw_ref[...]
                o_ref[rows[n], cols] = (o * _silu(og_ref[rows[n], cols].astype(F32))
                                        ).astype(o_ref.dtype)
        return carry

    lax.fori_loop(0, seq // (GLA_GROUP * cs), group, 0)


def _gla(u, small, w2p, gkb, nw, l, batch, seq):
    t = batch * seq
    body = functools.partial(_gla_kernel, seq)
    return pl.pallas_call(
        body,
        grid=(batch,),
        in_specs=[pl.BlockSpec((seq, 256), lambda b: (b, 6)),
                  pl.BlockSpec((seq, 256), lambda b: (b, 7)),
                  pl.BlockSpec((seq, 512), lambda b: (b, 4)),
                  pl.BlockSpec((seq, 512), lambda b: (b, 5)),
                  pl.BlockSpec((seq, U_SMALL), lambda b: (b, 0)),
                  pl.BlockSpec((None, U_SMALL, 256), lambda b: (l, 0, 0)),
                  pl.BlockSpec((None, 1, 256), lambda b: (l, 0, 0)),
                  pl.BlockSpec((None, 1, LANES), lambda b: (l, 0, 0))],
        out_specs=pl.BlockSpec((seq, BRANCH), lambda b: (b, 0)),
        out_shape=jax.ShapeDtypeStruct((t, BRANCH), BF16),
        scratch_shapes=[pltpu.VMEM((2, 2 * LANES, LANES), F32)],
        compiler_params=_params("arbitrary"),
        name="gla",
    )(u, u, u, u, small, w2p, gkb, nw)


def _conformer_kernel(seq, tr, a_ref, g_ref, w_ref, b_ref, lnw_ref, lnb_ref, o_ref, gbuf, cbuf):
    halo = 32
    gbuf[0:halo, :] = jnp.zeros((halo, BRANCH), F32)
    for r in range(0, seq, 256):
        gbuf[halo + r:halo + r + 256, :] = (a_ref[r:r + 256, :].astype(F32)
                                            * _sigmoid(g_ref[r:r + 256, :].astype(F32)))

    def conv_tile(i, carry):
        r0 = pl.multiple_of(i * tr, tr)
        for c0 in range(0, BRANCH, LANES):
            win = gbuf[pl.ds(r0, tr + halo), c0:c0 + LANES]
            acc = jnp.broadcast_to(b_ref[:, c0:c0 + LANES], (tr, LANES))
            for j in range(CONV_WIDTH):
                acc = acc + w_ref[j:j + 1, c0:c0 + LANES] * _shift_window(
                    win, halo - (CONV_WIDTH - 1) + j, tr)
            cbuf[pl.ds(r0, tr), c0:c0 + LANES] = acc
        return carry

    lax.fori_loop(0, seq // tr, conv_tile, 0)

    def norm_tile(i, carry):
        r0 = pl.multiple_of(i * tr, tr)
        c = cbuf[pl.ds(r0, tr), :]
        mu = jnp.mean(c, axis=-1, keepdims=True)
        d = c - mu
        var = jnp.mean(d * d, axis=-1, keepdims=True)
        y = d * lax.rsqrt(var + EPS) * lnw_ref[...] + lnb_ref[...]
        o_ref[pl.ds(r0, tr), :] = _silu(y).astype(o_ref.dtype)
        return carry

    lax.fori_loop(0, seq // tr, norm_tile, 0)


def _conformer(u, w, b, lnw, lnb, l, batch, seq, tr=128):
    t = batch * seq
    body = functools.partial(_conformer_kernel, seq, tr)
    return pl.pallas_call(
        body,
        grid=(batch,),
        in_specs=[pl.BlockSpec((seq, BRANCH), lambda i: (i, 6)),
                  pl.BlockSpec((seq, BRANCH), lambda i: (i, 7)),
                  pl.BlockSpec((None, CONV_WIDTH, BRANCH), lambda i: (l, 0, 0)),
                  pl.BlockSpec((None, 1, BRANCH), lambda i: (l, 0, 0)),
                  pl.BlockSpec((None, 1, BRANCH), lambda i: (l, 0, 0)),
                  pl.BlockSpec((None, 1, BRANCH), lambda i: (l, 0, 0))],
        out_specs=pl.BlockSpec((seq, BRANCH), lambda i: (i, 0)),
        out_shape=jax.ShapeDtypeStruct((t, BRANCH), BF16),
        scratch_shapes=[pltpu.VMEM((seq + 32, BRANCH), F32), pltpu.VMEM((seq, BRANCH), F32)],
        compiler_params=_params("arbitrary"),
        name="conformer_conv",
    )(u, u, w, b, lnw, lnb)


def _ssd_kernel(seq, z_ref, x_ref, bc_ref, small_ref, cw_ref, cb_ref, dtb_ref, alog_ref, dsk_ref,
                nw_ref, o_ref, xbuf, state_ref):
    cs = SSD_CHUNK
    halo = SUBLANES
    xbuf[0:halo, :] = jnp.zeros((halo, 2 * BRANCH), F32)
    for r in range(0, seq, 256):
        xbuf[halo + r:halo + r + 256, 0:BRANCH] = x_ref[r:r + 256, :].astype(F32)
        xbuf[halo + r:halo + r + 256, BRANCH:2 * BRANCH] = bc_ref[r:r + 256, :].astype(F32)
    state_ref[...] = jnp.zeros_like(state_ref)

    ri = lax.broadcasted_iota(jnp.int32, (cs, cs), 0)
    ci = lax.broadcasted_iota(jnp.int32, (cs, cs), 1)
    causal = ci <= ri
    tri = causal.astype(BF16)
    er = lax.broadcasted_iota(jnp.int32, (LANES, BRANCH), 0)
    ec = lax.broadcasted_iota(jnp.int32, (LANES, BRANCH), 1)
    expand = ((ec >> 6) == er).astype(BF16)
    lane = lax.broadcasted_iota(jnp.int32, (1, LANES), 1)
    head0 = lane < 64
    a_neg = -jnp.exp(alog_ref[...])

    def chunk(c, carry):
        r0 = pl.multiple_of(c * cs, cs)
        rows = pl.ds(r0, cs)
        win = xbuf[pl.ds(r0, cs + halo), :]
        conv = jnp.broadcast_to(cb_ref[...], (cs, 2 * BRANCH))
        for j in range(SSD_CONV):
            conv = conv + cw_ref[j:j + 1, :] * _shift_window(win, halo - (SSD_CONV - 1) + j, cs)
        xc = _silu(conv)
        xs = xc[:, 0:BRANCH]
        dt = _softplus(small_ref[rows, :] + dtb_ref[...])
        cum = _select_dot(tri, dt * a_neg)
        cum2 = cum * LOG2E
        cum2_t = cum2.T
        ecum = jnp.exp(cum)
        cum_last = cum[cs - 1:cs, :]
        dt_x = _dot_select(dt, expand)
        ecum_x = _dot_select(ecum, expand)
        dte_x = _dot_select(jnp.exp(cum_last - cum), expand)
        xdt = xs * dt_x
        ys = []
        for g in range(2):
            bm = xc[:, BRANCH + g * LANES:BRANCH + (g + 1) * LANES].astype(BF16)
            cm = xc[:, BRANCH + 256 + g * LANES:BRANCH + 256 + (g + 1) * LANES].astype(BF16)
            cb = _dot_nt(cm, bm)
            gl = slice(g * 256, (g + 1) * 256)
            for pr in range(2):
                pl_ = slice(g * 256 + pr * LANES, g * 256 + (pr + 1) * LANES)
                xp = xdt[:, pl_]
                y = None
                for e in range(2):
                    h = g * 4 + pr * 2 + e
                    seg = jnp.where(causal, cum2[:, h:h + 1] - cum2_t[h:h + 1, :], -jnp.inf)
                    m = (cb * jnp.exp2(seg)).astype(BF16)
                    xh = jnp.where(head0 if e == 0 else ~head0, xp, 0.0).astype(BF16)
                    t = _dot(m, xh)
                    y = t if y is None else y + t
                ys.append(y)
            st = state_ref[g]
            y_off = _dot(cm, st.astype(BF16)) * ecum_x[:, gl]
            ys[2 * g] = ys[2 * g] + y_off[:, 0:LANES]
            ys[2 * g + 1] = ys[2 * g + 1] + y_off[:, LANES:2 * LANES]
            upd = _dot_tn(bm, (xdt[:, gl] * dte_x[:, gl]).astype(BF16))
            state_ref[g] = st * ecum_x[cs - 1:cs, gl] + upd
        y = jnp.concatenate(ys, axis=1) + dsk_ref[...] * xs
        y = y * _silu(z_ref[rows, :].astype(F32))
        outs = []
        for g in range(2):
            yg = y[:, g * 256:(g + 1) * 256]
            ms = jnp.mean(yg * yg, axis=-1, keepdims=True)
            outs.append(yg * lax.rsqrt(ms + EPS))
        o_ref[rows, :] = (jnp.concatenate(outs, axis=1) * nw_ref[...]).astype(o_ref.dtype)
        return carry

    lax.fori_loop(0, seq // cs, chunk, 0)


def _ssd(u, small, cw, cb, dtb, alog, dsk, nw, l, batch, seq):
    t = batch * seq
    body = functools.partial(_ssd_kernel, seq)
    vec = lambda n: pl.BlockSpec((None, 1, n), lambda i: (l, 0, 0))
    return pl.pallas_call(
        body,
        grid=(batch,),
        in_specs=[pl.BlockSpec((seq, BRANCH), lambda i: (i, 8)),
                  pl.BlockSpec((seq, BRANCH), lambda i: (i, 9)),
                  pl.BlockSpec((seq, BRANCH), lambda i: (i, 10)),
                  pl.BlockSpec((seq, U_SMALL), lambda i: (i, 0)),
                  pl.BlockSpec((None, SSD_CONV, 2 * BRANCH), lambda i: (l, 0, 0)),
                  vec(2 * BRANCH), vec(LANES), vec(LANES), vec(BRANCH), vec(BRANCH)],
        out_specs=pl.BlockSpec((seq, BRANCH), lambda i: (i, 0)),
        out_shape=jax.ShapeDtypeStruct((t, BRANCH), BF16),
        scratch_shapes=[pltpu.VMEM((seq + SUBLANES, 2 * BRANCH), F32),
                        pltpu.VMEM((2, LANES, 256), F32)],
        compiler_params=_params("arbitrary"),
        name="ssd",
    )(u, u, u, small, cw, cb, dtb, alog, dsk, nw)


def _merge_kernel(h_ref, ya_ref, yb_ref, yc_ref, yd_ref, wg_ref, wb_ref, bg_ref, o_ref,
                  wg_scr, wb_scr):
    @pl.when(pl.program_id(1) == 0)
    def _():
        for i in range(4):
            _cast_rows(wg_scr.at[i], wg_ref.at[i])
            _cast_rows(wb_scr.at[i], wb_ref.at[i])

    h = h_ref[...]
    acc = None
    for i, y_ref in enumerate((ya_ref, yb_ref, yc_ref, yd_ref)):
        gate = _sigmoid(_dot(h, wg_scr[i]) + bg_ref[i:i + 1, :])
        term = gate * _dot(y_ref[...], wb_scr[i])
        acc = term if acc is None else acc + term
    o_ref[...] = acc.astype(o_ref.dtype)


def _merge(h, ys, wg, wb, bg, l, tm=1024, tn=256):
    m, d = h.shape
    yspec = pl.BlockSpec((tm, BRANCH), lambda j, i: (i, 0))
    return pl.pallas_call(
        _merge_kernel,
        grid=(d // tn, m // tm),
        in_specs=[pl.BlockSpec((tm, d), lambda j, i: (i, 0)), yspec, yspec, yspec, yspec,
                  pl.BlockSpec((None, 4, d, tn), lambda j, i: (l, 0, 0, j)),
                  pl.BlockSpec((None, 4, BRANCH, tn), lambda j, i: (l, 0, 0, j)),
                  pl.BlockSpec((None, 4, tn), lambda j, i: (l, 0, j))],
        out_specs=pl.BlockSpec((tm, tn), lambda j, i: (i, j)),
        out_shape=jax.ShapeDtypeStruct((m, d), BF16),
        scratch_shapes=[pltpu.VMEM((4, d, tn), BF16), pltpu.VMEM((4, BRANCH, tn), BF16)],
        compiler_params=_params("arbitrary", "arbitrary"),
        name="gated_merge",
    )(h, *ys, wg, wb, bg)


def _ffn_up_kernel(n_tiles, row_tiles, tiles_per_seq, tm, h_ref, wg_ref, wv_ref, cwg_ref, cwv_ref,
                   cbg_ref, cbv_ref, o_ref, raw_g, raw_v, ghist, vhist, wg_scr, wv_scr):
    halo = SUBLANES
    sub = FFN_ROWS
    s = pl.program_id(0)
    cur = jnp.minimum(s, n_tiles - 1)
    prev = jnp.maximum(s - 1, 0)
    slot = s & 1

    @pl.when(s == 0)
    def _():
        raw_g[1] = jnp.zeros(raw_g.shape[1:], F32)
        raw_v[1] = jnp.zeros(raw_v.shape[1:], F32)

    @pl.when((cur % row_tiles == 0) & (s < n_tiles))
    def _():
        _cast_rows(wg_scr, wg_ref)
        _cast_rows(wv_scr, wv_ref)

    @pl.when(prev % tiles_per_seq == 0)
    def _():
        ghist[...] = jnp.zeros_like(ghist)
        vhist[...] = jnp.zeros_like(vhist)

    def conv(raw, old, hist, cw_ref, cb_ref, r, cols):
        if r == 0:
            win = jnp.concatenate([hist[:, cols], raw[old, 0:EP_ROWS, cols]], axis=0)
        else:
            win = raw[old, r - halo:r + EP_ROWS, cols]
        out = jnp.broadcast_to(cb_ref[:, cols], (EP_ROWS, LANES))
        for j in range(FFN_CONV):
            out = out + cw_ref[j:j + 1, cols] * _shift_window(win, halo - (FFN_CONV - 1) + j,
                                                              EP_ROWS)
        return out

    def step(new, old):
        tn = raw_g.shape[2]
        for r0 in range(0, tm, sub):
            h = h_ref[r0:r0 + sub, :]
            pieces = [(r, c) for r in range(r0, r0 + sub, EP_ROWS) for c in range(0, tn, LANES)]
            dots = [(raw, w, c) for c in range(0, tn, MXU_COLS)
                    for raw, w in ((raw_g, wg_scr), (raw_v, wv_scr))]
            per_dot = len(pieces) // len(dots)
            for n, (raw, w, c) in enumerate(dots):
                raw[new, r0:r0 + sub, c:c + MXU_COLS] = _dot(h, w[:, c:c + MXU_COLS])
                for r, pc in pieces[n * per_dot:(n + 1) * per_dot]:
                    cols = slice(pc, pc + LANES)
                    gate = conv(raw_g, old, ghist, cwg_ref, cbg_ref, r, cols)
                    val = conv(raw_v, old, vhist, cwv_ref, cbv_ref, r, cols)
                    o_ref[r:r + EP_ROWS, cols] = (_silu(gate) * val).astype(o_ref.dtype)
        ghist[...] = raw_g[old, tm - halo:tm, :]
        vhist[...] = raw_v[old, tm - halo:tm, :]

    @pl.when(slot == 0)
    def _():
        step(0, 1)

    @pl.when(slot == 1)
    def _():
        step(1, 0)


def _ffn_up(h, w_up, cw, cb, l, seq, tm=1024, tn=512):
    m, d = h.shape
    nt = D_FF // tn
    rt = m // tm
    n_tiles = nt * rt
    body = functools.partial(_ffn_up_kernel, n_tiles, rt, seq // tm, tm)
    cur = lambda s: jnp.minimum(s, n_tiles - 1)
    prev = lambda s: jnp.maximum(s - 1, 0)
    return pl.pallas_call(
        body,
        grid=(n_tiles + 1,),
        in_specs=[pl.BlockSpec((tm, d), lambda s: (cur(s) % rt, 0)),
                  pl.BlockSpec((None, d, tn), lambda s: (l, 0, cur(s) // rt)),
                  pl.BlockSpec((None, d, tn), lambda s: (l, 0, cur(s) // rt + nt)),
                  pl.BlockSpec((None, FFN_CONV, tn), lambda s: (l, 0, prev(s) // rt)),
                  pl.BlockSpec((None, FFN_CONV, tn), lambda s: (l, 0, prev(s) // rt + nt)),
                  pl.BlockSpec((None, 1, tn), lambda s: (l, 0, prev(s) // rt)),
                  pl.BlockSpec((None, 1, tn), lambda s: (l, 0, prev(s) // rt + nt))],
        out_specs=pl.BlockSpec((tm, tn), lambda s: (prev(s) % rt, prev(s) // rt)),
        out_shape=jax.ShapeDtypeStruct((m, D_FF), BF16),
        scratch_shapes=[pltpu.VMEM((2, tm, tn), F32), pltpu.VMEM((2, tm, tn), F32),
                        pltpu.VMEM((SUBLANES, tn), F32), pltpu.VMEM((SUBLANES, tn), F32),
                        pltpu.VMEM((d, tn), BF16), pltpu.VMEM((d, tn), BF16)],
        compiler_params=_params("arbitrary"),
        name="ffn_up_conv_gate",
    )(h, w_up, w_up, cw, cw, cb, cb)


def kernel(x, mix_norm, w_in, diff_qk_norm, diff_lambda, diff_subln, gla_gk_w2, gla_gk_b, gla_norm, conv_dw_w, conv_dw_b, conv_ln_w, conv_ln_b, ssd_conv_w, ssd_conv_b, ssd_dt_bias, ssd_a_log, ssd_d, ssd_norm, w_branch, w_gate, b_gate, w_out, ffn_norm, ffn_w_up, ffn_conv_w, ffn_conv_b, ffn_w_down):
    batch, seq, d = x.shape
    depth = w_in.shape[0]
    t = batch * seq
    xf = x.reshape(t, d)

    w2_pad = jnp.zeros((depth, U_SMALL, 256), F32).at[:, 8:24, :].set(gla_gk_w2).astype(BF16)
    pad8 = lambda v: jnp.pad(v, ((0, 0), (0, LANES - 8)))[:, None, :]
    row = lambda v: v[:, None, :]
    qk_gain = jnp.tile(diff_qk_norm, (1, 1, 2))
    dskip = jnp.repeat(ssd_d, 64, axis=-1)[:, None, :]
    w_out_b = w_out.astype(BF16)

    for l in range(depth):
        lambda_init = 0.8 - 0.6 * math.exp(-0.3 * l)
        h = _rmsnorm(xf, row(mix_norm), l)
        u = _in_proj(h, w_in, l)
        small = _in_proj_small(h, w_in, l)
        ya = _diff_attention(u, qk_gain, diff_lambda, row(diff_subln), l, lambda_init, batch, seq)
        yb = _gla(u, small, w2_pad, row(gla_gk_b), row(gla_norm), l, batch, seq)
        yc = _conformer(u, conv_dw_w, row(conv_dw_b), row(conv_ln_w), row(conv_ln_b), l, batch, seq)
        yd = _ssd(u, small, ssd_conv_w, row(ssd_conv_b), pad8(ssd_dt_bias), pad8(ssd_a_log), dskip,
                  row(ssd_norm), l, batch, seq)
        merged = _merge(h, (ya, yb, yc, yd), w_gate, w_branch, b_gate, l)
        xf, h2 = _proj_res_norm(merged, w_out_b, l, xf, row(ffn_norm), name="out_proj_norm")
        act = _ffn_up(h2, ffn_w_up, ffn_conv_w, row(ffn_conv_b), l, seq)
        xf = _matmul(act, ffn_w_down, l, tm=512, tn=512, out_dtype=F32, residual=xf, name="ffn_down")
    return xf.reshape(batch, seq, d)
```

```python
import functools
import math

import jax
import jax.numpy as jnp
from jax import lax
from jax.experimental import pallas as pl
from jax.experimental.pallas import tpu as pltpu

F32 = jnp.float32
BF16 = jnp.bfloat16

EPS = 1e-6
LOG2E = 1.4426950408889634
D_MODEL = 2048
BRANCH = 512
D_FF = 5632
U_BIG = 5632
U_SMALL = 128
LANES = 128
SUBLANES = 8
VMEM_LIMIT = 56 * 1024 * 1024

GLA_CHUNK = 64
GLA_BLOCK = 8
GLA_GROUP = 4
SSD_CHUNK = 256
CONV_WIDTH = 31
SSD_CONV = 4
FFN_CONV = 3
FFN_ROWS = 128
PROJ_ROWS = 256
CAST_ROWS = 256
EP_ROWS = 64
MXU_COLS = 256


def _params(*sem):
    return pltpu.CompilerParams(dimension_semantics=sem, vmem_limit_bytes=VMEM_LIMIT)


def _sigmoid(x):
    return 0.5 * jnp.tanh(0.5 * x) + 0.5


def _silu(x):
    half = 0.5 * x
    return half * jnp.tanh(half) + half


def _softplus(x):
    return jnp.maximum(x, 0.0) + jnp.log1p(jnp.exp(-jnp.abs(x)))


def _split3(x):
    hi = x.astype(BF16)
    r1 = x - hi.astype(F32)
    mid = r1.astype(BF16)
    lo = (r1 - mid.astype(F32)).astype(BF16)
    return hi, mid, lo


def _dot(a, b):
    return jnp.dot(a, b, preferred_element_type=F32)


def _dot_nt(a, b):
    return lax.dot_general(a, b, (((1,), (1,)), ((), ())), preferred_element_type=F32)


def _dot_tn(a, b):
    return lax.dot_general(a, b, (((0,), (0,)), ((), ())), preferred_element_type=F32)


def _select_dot(sel, x):
    hi, mid, lo = _split3(x)
    return _dot(sel, hi) + _dot(sel, mid) + _dot(sel, lo)


def _dot_select(x, sel):
    hi, mid, lo = _split3(x)
    return _dot(hi, sel) + _dot(mid, sel) + _dot(lo, sel)


def _shift_window(win, shift, rows):
    total = win.shape[0]
    if shift % SUBLANES == 0:
        return win[shift:shift + rows]
    rolled = pltpu.roll(win, total - (shift % SUBLANES), 0)
    base = shift - shift % SUBLANES
    return rolled[base:base + rows]


LOW_COL0 = 3072
DT_COL0 = 5648


def _rmsnorm_small_kernel(x_ref, nw_ref, wlow_ref, wdt_ref, h_ref, s_ref, w_scr):
    @pl.when(pl.program_id(0) == 0)
    def _():
        lane = lax.broadcasted_iota(jnp.int32, (1, LANES), 1)
        for r in range(0, w_scr.shape[0], CAST_ROWS):
            rows = slice(r, r + CAST_ROWS)
            dt = pltpu.roll(wdt_ref[rows, :], LANES - DT_COL0 % LANES, 1)
            low = pltpu.roll(wlow_ref[rows, :], 8 - LOW_COL0 % LANES, 1)
            w_scr[rows, :] = jnp.where(lane < 8, dt, jnp.where(lane < 24, low, 0.0)).astype(BF16)

    x = x_ref[...]
    ms = jnp.mean(x * x, axis=-1, keepdims=True)
    h = (x * lax.rsqrt(ms + EPS) * nw_ref[...]).astype(BF16)
    h_ref[...] = h
    s_ref[...] = _dot(h, w_scr[...])


def _rmsnorm_small(x, nw3, w_in, l, tm=512):
    m, d = x.shape
    return pl.pallas_call(
        _rmsnorm_small_kernel,
        grid=(m // tm,),
        in_specs=[pl.BlockSpec((tm, d), lambda i: (i, 0)),
                  pl.BlockSpec((None, 1, d), lambda i: (l, 0, 0)),
                  pl.BlockSpec((None, d, LANES), lambda i: (l, 0, LOW_COL0 // LANES)),
                  pl.BlockSpec((None, d, LANES), lambda i: (l, 0, DT_COL0 // LANES))],
        out_specs=[pl.BlockSpec((tm, d), lambda i: (i, 0)),
                   pl.BlockSpec((tm, U_SMALL), lambda i: (i, 0))],
        out_shape=[jax.ShapeDtypeStruct((m, d), BF16), jax.ShapeDtypeStruct((m, U_SMALL), F32)],
        scratch_shapes=[pltpu.VMEM((d, U_SMALL), BF16)],
        compiler_params=_params("arbitrary"),
        name="rmsnorm_small_proj",
    )(x, nw3, w_in, w_in)


def _cast_rows(dst_ref, src_ref):
    for r in range(0, src_ref.shape[0], CAST_ROWS):
        dst_ref[r:r + CAST_ROWS, :] = src_ref[r:r + CAST_ROWS, :].astype(BF16)


def _matmul_cast_res_kernel(a_ref, w_ref, r_ref, o_ref, w_scr):
    @pl.when(pl.program_id(1) == 0)
    def _():
        _cast_rows(w_scr, w_ref)

    o_ref[...] = r_ref[...] + _dot(a_ref[...], w_scr[...])


def _matmul_res(a, w3, l, residual, *, tm, tn, name):
    m, k = a.shape
    n = w3.shape[-1]
    return pl.pallas_call(
        _matmul_cast_res_kernel,
        grid=(n // tn, m // tm),
        in_specs=[pl.BlockSpec((tm, k), lambda j, i: (i, 0)),
                  pl.BlockSpec((None, k, tn), lambda j, i: (l, 0, j)),
                  pl.BlockSpec((tm, tn), lambda j, i: (i, j))],
        out_specs=pl.BlockSpec((tm, tn), lambda j, i: (i, j)),
        out_shape=jax.ShapeDtypeStruct((m, n), F32),
        scratch_shapes=[pltpu.VMEM((k, tn), BF16)],
        compiler_params=_params("arbitrary", "arbitrary"),
        name=name,
    )(a, w3, residual)


IN_TN = 512
IN_SHIFT_TILE = 3072 // IN_TN
IN_SHIFT = 16


def _in_proj_kernel(a_ref, wa_ref, wb_ref, o_ref, w_scr):
    j = pl.program_id(0)
    first_row_tile = pl.program_id(1) == 0

    @pl.when(first_row_tile & (j < IN_SHIFT_TILE))
    def _():
        _cast_rows(w_scr, wa_ref)

    @pl.when(first_row_tile & (j >= IN_SHIFT_TILE))
    def _():
        for r in range(0, wa_ref.shape[0], CAST_ROWS):
            wide = jnp.concatenate([wa_ref[r:r + CAST_ROWS, :], wb_ref[r:r + CAST_ROWS, :]], axis=1)
            w_scr[r:r + CAST_ROWS, :] = wide[:, IN_SHIFT:IN_SHIFT + IN_TN].astype(BF16)

    o_ref[...] = _dot(a_ref[...], w_scr[...]).astype(o_ref.dtype)


def _in_proj(a, w_in, l, tm=2048):
    m, k = a.shape
    per = IN_TN // LANES
    return pl.pallas_call(
        _in_proj_kernel,
        grid=(U_BIG // IN_TN, m // tm),
        in_specs=[pl.BlockSpec((tm, k), lambda j, i: (i, 0)),
                  pl.BlockSpec((None, k, IN_TN), lambda j, i: (l, 0, j)),
                  pl.BlockSpec((None, k, LANES),
                               lambda j, i: (l, 0, per * (jnp.maximum(j, IN_SHIFT_TILE) + 1)))],
        out_specs=pl.BlockSpec((tm, IN_TN), lambda j, i: (i, j)),
        out_shape=jax.ShapeDtypeStruct((m, U_BIG), BF16),
        scratch_shapes=[pltpu.VMEM((k, IN_TN), BF16)],
        compiler_params=_params("arbitrary", "arbitrary"),
        name="in_proj",
    )(a, w_in, w_in)


def _proj_res_norm_kernel(tm, a_ref, w_ref, r_ref, nw_ref, x_ref, h_ref):
    for r in range(0, tm, PROJ_ROWS):
        rows = slice(r, r + PROJ_ROWS)
        xn = r_ref[rows, :] + _dot(a_ref[rows, :], w_ref[...])
        x_ref[rows, :] = xn
        ms = jnp.mean(xn * xn, axis=-1, keepdims=True)
        h_ref[rows, :] = (xn * lax.rsqrt(ms + EPS) * nw_ref[...]).astype(h_ref.dtype)


def _proj_res_norm(a, w3, l, residual, nw3, tm=512, name="proj_res_norm"):
    m, k = a.shape
    n = w3.shape[-1]
    return pl.pallas_call(
        functools.partial(_proj_res_norm_kernel, tm),
        grid=(m // tm,),
        in_specs=[pl.BlockSpec((tm, k), lambda i: (i, 0)),
                  pl.BlockSpec((None, k, n), lambda i: (l, 0, 0)),
                  pl.BlockSpec((tm, n), lambda i: (i, 0)),
                  pl.BlockSpec((None, 1, n), lambda i: (l, 0, 0))],
        out_specs=[pl.BlockSpec((tm, n), lambda i: (i, 0)),
                   pl.BlockSpec((tm, n), lambda i: (i, 0))],
        out_shape=[jax.ShapeDtypeStruct((m, n), F32), jax.ShapeDtypeStruct((m, n), BF16)],
        compiler_params=_params("arbitrary"),
        name=name,
    )(a, w3, residual, nw3)


def _diff_attn_kernel(lambda_init, seq, tq, q_ref, k_ref, v_ref, gain_ref, lam_ref, subln_ref,
                      o_ref, kn_ref, va_ref):
    lane = lax.broadcasted_iota(jnp.int32, (1, LANES), 1)
    first = lane < 64

    def halfnorm(t, g):
        sq = t * t
        s1 = jnp.sum(jnp.where(first, sq, 0.0), axis=-1, keepdims=True)
        s2 = jnp.sum(jnp.where(first, 0.0, sq), axis=-1, keepdims=True)
        ms = jnp.where(first, s1, s2) * (1.0 / 64)
        return t * lax.rsqrt(ms + EPS) * g

    kn_ref[...] = halfnorm(k_ref[...].astype(F32), gain_ref[1:2, :]).astype(BF16)
    va_ref[:, 0:LANES] = v_ref[...]
    va_ref[:, LANES:2 * LANES] = jnp.ones((seq, LANES), BF16)
    lam = lam_ref[...]
    l1 = jnp.sum(lam[0:1] * lam[1:2], axis=-1, keepdims=True)
    l2 = jnp.sum(lam[2:3] * lam[3:4], axis=-1, keepdims=True)
    lam_full = jnp.exp(l1) - jnp.exp(l2) + lambda_init

    visible = ((lax.broadcasted_iota(jnp.int32, (2 * tq, tq), 1) >> 6)
               <= ((lax.broadcasted_iota(jnp.int32, (2 * tq, tq), 0) & (tq - 1)) >> 6))

    def scores(i):
        q0 = i * tq
        qn = halfnorm(q_ref[q0:q0 + tq, :].astype(F32), gain_ref[0:1, :]) * (64 ** -0.5 * LOG2E)
        qs = jnp.concatenate([jnp.where(first, qn, 0.0), jnp.where(first, 0.0, qn)],
                             axis=0).astype(BF16)
        s_dg = jnp.where(visible, _dot_nt(qs, kn_ref[q0:q0 + tq, :]), -jnp.inf)
        s_off = _dot_nt(qs, kn_ref[0:q0, :]) if q0 else None
        return s_dg, s_off

    n_tiles = seq // tq
    nxt = scores(0)
    for i in range(n_tiles):
        q0 = i * tq
        s_dg, s_off = nxt
        nxt = scores(i + 1) if i + 1 < n_tiles else None
        m = jnp.max(s_dg, axis=-1, keepdims=True)
        if q0:
            m = jnp.maximum(m, jnp.max(s_off, axis=-1, keepdims=True))
        acc = _dot(jnp.exp2(s_dg - m).astype(BF16), va_ref[q0:q0 + tq, :])
        if q0:
            acc = acc + _dot(jnp.exp2(s_off - m).astype(BF16), va_ref[0:q0, :])
        acc = acc[:, 0:LANES] * (1.0 / acc[:, LANES:2 * LANES])
        o = acc[0:tq] - lam_full * acc[tq:2 * tq]
        ms = jnp.mean(o * o, axis=-1, keepdims=True)
        o = o * lax.rsqrt(ms + EPS) * subln_ref[...] * (1.0 - lambda_init)
        o_ref[q0:q0 + tq, :] = o.astype(o_ref.dtype)


def _diff_attention(u, gain, lam, subln, l, lambda_init, batch, seq, tq=256):
    t = batch * seq
    body = functools.partial(_diff_attn_kernel, lambda_init, seq, tq)
    return pl.pallas_call(
        body,
        grid=(batch, 4),
        in_specs=[pl.BlockSpec((seq, LANES), lambda b, h: (b, h)),
                  pl.BlockSpec((seq, LANES), lambda b, h: (b, 4 + h)),
                  pl.BlockSpec((seq, LANES), lambda b, h: (b, 8 + h)),
                  pl.BlockSpec((None, 2, LANES), lambda b, h: (l, 0, 0)),
                  pl.BlockSpec((None, 4, 64), lambda b, h: (l, 0, 0)),
                  pl.BlockSpec((None, 1, LANES), lambda b, h: (l, 0, 0))],
        out_specs=pl.BlockSpec((seq, LANES), lambda b, h: (b, h)),
        out_shape=jax.ShapeDtypeStruct((t, BRANCH), BF16),
        scratch_shapes=[pltpu.VMEM((seq, LANES), BF16), pltpu.VMEM((seq, 2 * LANES), BF16)],
        compiler_params=_params("arbitrary", "arbitrary"),
        name="diff_attention",
    )(u, u, u, gain, lam, subln)


def _gla_kernel(seq, q_ref, k_ref, v_ref, og_ref, small_ref, w2_ref, gkb_ref, nw_ref, o_ref,
                state_ref):
    cs, nb = GLA_CHUNK, GLA_CHUNK // GLA_BLOCK
    ri = lax.broadcasted_iota(jnp.int32, (cs, cs), 0)
    ci = lax.broadcasted_iota(jnp.int32, (cs, cs), 1)
    tri = (ci <= ri).astype(BF16)
    row = lax.broadcasted_iota(jnp.int32, (cs, LANES), 0)
    row_blk = row >> 3
    row_mod = row & 7
    sub3 = lax.broadcasted_iota(jnp.int32, (nb, GLA_BLOCK, LANES), 1)
    lane = lax.broadcasted_iota(jnp.int32, (1, LANES), 1)
    head0 = lane < 64
    r2 = lax.broadcasted_iota(jnp.int32, (2 * cs, cs), 0)
    c2 = lax.broadcasted_iota(jnp.int32, (2 * cs, cs), 1)
    same_blk = ((r2 & (cs - 1)) >> 3) == (c2 >> 3)
    sr = lax.broadcasted_iota(jnp.int32, (2 * LANES, LANES), 0)
    sc = lax.broadcasted_iota(jnp.int32, (2 * LANES, LANES), 1)
    state_mask = (sr >> 7) == (sc >> 6)
    state_ref[...] = jnp.zeros_like(state_ref)

    def both_heads(x):
        m0 = (lax.broadcasted_iota(jnp.int32, (1, x.shape[1]), 1) & 64) == 0
        return jnp.concatenate([jnp.where(m0, x, 0.0), jnp.where(m0, 0.0, x)], axis=0).astype(BF16)

    def group(gi, carry):
        base = gi * (GLA_GROUP * cs)
        rows = [pl.ds(pl.multiple_of(base + n * cs, cs), cs) for n in range(GLA_GROUP)]
        probs = [(n, p) for n in range(GLA_GROUP) for p in range(2)]
        lanes = [slice(p * LANES, (p + 1) * LANES) for p in range(2)]

        gk = []
        for n in range(GLA_GROUP):
            z = _dot(small_ref[rows[n], :].astype(BF16), w2_ref[...]) + gkb_ref[...]
            gk.append((jnp.minimum(z, 0.0) - jnp.log1p(jnp.exp(-jnp.abs(z)))) * (LOG2E / 16.0))
        cum = {(n, p): _select_dot(tri, gk[n][:, lanes[p]]) for n, p in probs}

        ops = {}
        for n, p in probs:
            c = cum[n, p]
            q = q_ref[rows[n], lanes[p]].astype(F32) * (64 ** -0.5)
            k = k_ref[rows[n], lanes[p]].astype(F32)
            cum3 = c.reshape(nb, GLA_BLOCK, LANES)
            q3 = q.reshape(nb, GLA_BLOCK, LANES)
            last3 = cum3[:, GLA_BLOCK - 1:GLA_BLOCK, :]
            ref3 = jnp.concatenate([jnp.zeros((1, 1, LANES), F32), last3[:nb - 1]], axis=0)
            refrow = jnp.broadcast_to(ref3, (nb, GLA_BLOCK, LANES)).reshape(cs, LANES)
            qp = q * jnp.exp2(c - refrow)
            lhs_off, rhs_off = [], []
            for blk in range(1, nb):
                lhs_off.append(jnp.where(row_blk == blk, qp, 0.0))
                m = blk * GLA_BLOCK
                kd = k[0:m] * jnp.exp2(c[m - 1:m, :] - c[0:m])
                rhs_off.append(jnp.concatenate([kd, jnp.zeros((cs - m, LANES), F32)], axis=0))
            lhs_dg, rhs_dg = [], []
            for j in range(GLA_BLOCK):
                dec = jnp.exp2(jnp.minimum(cum3 - cum3[:, j:j + 1, :], 0.0))
                lhs_dg.append(jnp.where(sub3 >= j, q3 * dec, 0.0).reshape(cs, LANES))
                rhs_dg.append(jnp.where(row_mod == j, k, 0.0))
            c_last = c[cs - 1:cs, :]
            ops[n, p] = dict(
                lhs_off=both_heads(jnp.concatenate(lhs_off, axis=1)),
                rhs_off=jnp.concatenate(rhs_off, axis=1).astype(BF16),
                lhs_dg=both_heads(jnp.concatenate(lhs_dg, axis=1)),
                rhs_dg=jnp.concatenate(rhs_dg, axis=1).astype(BF16),
                qe=(q * jnp.exp2(c)).astype(BF16),
                kdec=(k * jnp.exp2(c_last - c)).astype(BF16),
                dec=jnp.exp2(c_last))

        attn, upd = {}, {}
        for n, p in probs:
            o = ops[n, p]
            s_off = _dot_nt(o["lhs_off"], o["rhs_off"])
            s_dg = _dot_nt(o["lhs_dg"], o["rhs_dg"])
            attn[n, p] = (s_off + jnp.where(same_blk, s_dg, 0.0)).astype(BF16)
            vv = v_ref[rows[n], p * 2 * LANES:(p + 1) * 2 * LANES]
            upd[n, p] = _dot_tn(vv, o["kdec"])

        inter = {}
        for p in range(2):
            st = state_ref[p]
            for n in range(GLA_GROUP):
                inter[n, p] = _dot_nt(ops[n, p]["qe"], st.astype(BF16))
                st = jnp.where(state_mask, st * ops[n, p]["dec"] + upd[n, p], 0.0)
            state_ref[p] = st

        for n, p in probs:
            for e in range(2):
                cols = slice((2 * p + e) * LANES, (2 * p + e + 1) * LANES)
                o = (_dot(attn[n, p][e * cs:(e + 1) * cs], v_ref[rows[n], cols])
                     + inter[n, p][:, e * LANES:(e + 1) * LANES])
                ms = jnp.mean(o * o, axis=-1, keepdims=True)
                o = o * lax.rsqrt(ms + EPS) * nw_ref[...]
                o_ref[rows[n], cols] = (o * _silu(og_ref[rows[n], cols].astype(F32))
                                        ).astype(o_ref.dtype)
        return carry

    lax.fori_loop(0, seq // (GLA_GROUP * cs), group, 0)


def _gla(u, small, w2p, gkb, nw, l, batch, seq):
    t = batch * seq
    body = functools.partial(_gla_kernel, seq)
    return pl.pallas_call(
        body,
        grid=(batch,),
        in_specs=[pl.BlockSpec((seq, 256), lambda b: (b, 6)),
                  pl.BlockSpec((seq, 256), lambda b: (b, 7)),
                  pl.BlockSpec((seq, 512), lambda b: (b, 4)),
                  pl.BlockSpec((seq, 512), lambda b: (b, 5)),
                  pl.BlockSpec((seq, U_SMALL), lambda b: (b, 0)),
                  pl.BlockSpec((None, U_SMALL, 256), lambda b: (l, 0, 0)),
                  pl.BlockSpec((None, 1, 256), lambda b: (l, 0, 0)),
                  pl.BlockSpec((None, 1, LANES), lambda b: (l, 0, 0))],
        out_specs=pl.BlockSpec((seq, BRANCH), lambda b: (b, 0)),
        out_shape=jax.ShapeDtypeStruct((t, BRANCH), BF16),
        scratch_shapes=[pltpu.VMEM((2, 2 * LANES, LANES), F32)],
        compiler_params=_params("arbitrary"),
        name="gla",
    )(u, u, u, u, small, w2p, gkb, nw)


def _conformer_kernel(seq, tr, a_ref, g_ref, w_ref, b_ref, lnw_ref, lnb_ref, o_ref, gbuf, cbuf):
    halo = 32
    gbuf[0:halo, :] = jnp.zeros((halo, BRANCH), F32)
    for r in range(0, seq, 256):
        gbuf[halo + r:halo + r + 256, :] = (a_ref[r:r + 256, :].astype(F32)
                                            * _sigmoid(g_ref[r:r + 256, :].astype(F32)))

    def conv_tile(i, carry):
        r0 = pl.multiple_of(i * tr, tr)
        for c0 in range(0, BRANCH, LANES):
            win = gbuf[pl.ds(r0, tr + halo), c0:c0 + LANES]
            acc = jnp.broadcast_to(b_ref[:, c0:c0 + LANES], (tr, LANES))
            for j in range(CONV_WIDTH):
                acc = acc + w_ref[j:j + 1, c0:c0 + LANES] * _shift_window(
                    win, halo - (CONV_WIDTH - 1) + j, tr)
            cbuf[pl.ds(r0, tr), c0:c0 + LANES] = acc
        return carry

    lax.fori_loop(0, seq // tr, conv_tile, 0)

    def norm_tile(i, carry):
        r0 = pl.multiple_of(i * tr, tr)
        c = cbuf[pl.ds(r0, tr), :]
        mu = jnp.mean(c, axis=-1, keepdims=True)
        d = c - mu
        var = jnp.mean(d * d, axis=-1, keepdims=True)
        y = d * lax.rsqrt(var + EPS) * lnw_ref[...] + lnb_ref[...]
        o_ref[pl.ds(r0, tr), :] = _silu(y).astype(o_ref.dtype)
        return carry

    lax.fori_loop(0, seq // tr, norm_tile, 0)


def _conformer(u, w, b, lnw, lnb, l, batch, seq, tr=128):
    t = batch * seq
    body = functools.partial(_conformer_kernel, seq, tr)
    return pl.pallas_call(
        body,
        grid=(batch,),
        in_specs=[pl.BlockSpec((seq, BRANCH), lambda i: (i, 6)),
                  pl.BlockSpec((seq, BRANCH), lambda i: (i, 7)),
                  pl.BlockSpec((None, CONV_WIDTH, BRANCH), lambda i: (l, 0, 0)),
                  pl.BlockSpec((None, 1, BRANCH), lambda i: (l, 0, 0)),
                  pl.BlockSpec((None, 1, BRANCH), lambda i: (l, 0, 0)),
                  pl.BlockSpec((None, 1, BRANCH), lambda i: (l, 0, 0))],
        out_specs=pl.BlockSpec((seq, BRANCH), lambda i: (i, 0)),
        out_shape=jax.ShapeDtypeStruct((t, BRANCH), BF16),
        scratch_shapes=[pltpu.VMEM((seq + 32, BRANCH), F32), pltpu.VMEM((seq, BRANCH), F32)],
        compiler_params=_params("arbitrary"),
        name="conformer_conv",
    )(u, u, w, b, lnw, lnb)


def _ssd_kernel(seq, z_ref, x_ref, bc_ref, small_ref, cw_ref, cb_ref, dtb_ref, alog_ref, dsk_ref,
                nw_ref, o_ref, xbuf, state_ref):
    cs = SSD_CHUNK
    halo = SUBLANES
    xbuf[0:halo, :] = jnp.zeros((halo, 2 * BRANCH), F32)
    for r in range(0, seq, 256):
        xbuf[halo + r:halo + r + 256, 0:BRANCH] = x_ref[r:r + 256, :].astype(F32)
        xbuf[halo + r:halo + r + 256, BRANCH:2 * BRANCH] = bc_ref[r:r + 256, :].astype(F32)
    state_ref[...] = jnp.zeros_like(state_ref)

    ri = lax.broadcasted_iota(jnp.int32, (cs, cs), 0)
    ci = lax.broadcasted_iota(jnp.int32, (cs, cs), 1)
    causal = ci <= ri
    tri = causal.astype(BF16)
    er = lax.broadcasted_iota(jnp.int32, (LANES, BRANCH), 0)
    ec = lax.broadcasted_iota(jnp.int32, (LANES, BRANCH), 1)
    expand = ((ec >> 6) == er).astype(BF16)
    lane = lax.broadcasted_iota(jnp.int32, (1, LANES), 1)
    head0 = lane < 64
    a_neg = -jnp.exp(alog_ref[...])

    def chunk(c, carry):
        r0 = pl.multiple_of(c * cs, cs)
        rows = pl.ds(r0, cs)
        win = xbuf[pl.ds(r0, cs + halo), :]
        conv = jnp.broadcast_to(cb_ref[...], (cs, 2 * BRANCH))
        for j in range(SSD_CONV):
            conv = conv + cw_ref[j:j + 1, :] * _shift_window(win, halo - (SSD_CONV - 1) + j, cs)
        xc = _silu(conv)
        xs = xc[:, 0:BRANCH]
        dt = _softplus(small_ref[rows, :] + dtb_ref[...])
        cum = _select_dot(tri, dt * a_neg)
        cum2 = cum * LOG2E
        cum2_t = cum2.T
        ecum = jnp.exp(cum)
        cum_last = cum[cs - 1:cs, :]
        dt_x = _dot_select(dt, expand)
        ecum_x = _dot_select(ecum, expand)
        dte_x = _dot_select(jnp.exp(cum_last - cum), expand)
        xdt = xs * dt_x
        ys = []
        for g in range(2):
            bm = xc[:, BRANCH + g * LANES:BRANCH + (g + 1) * LANES].astype(BF16)
            cm = xc[:, BRANCH + 256 + g * LANES:BRANCH + 256 + (g + 1) * LANES].astype(BF16)
            cb = _dot_nt(cm, bm)
            gl = slice(g * 256, (g + 1) * 256)
            for pr in range(2):
                pl_ = slice(g * 256 + pr * LANES, g * 256 + (pr + 1) * LANES)
                xp = xdt[:, pl_]
                y = None
                for e in range(2):
                    h = g * 4 + pr * 2 + e
                    seg = jnp.where(causal, cum2[:, h:h + 1] - cum2_t[h:h + 1, :], -jnp.inf)
                    m = (cb * jnp.exp2(seg)).astype(BF16)
                    xh = jnp.where(head0 if e == 0 else ~head0, xp, 0.0).astype(BF16)
                    t = _dot(m, xh)
                    y = t if y is None else y + t
                ys.append(y)
            st = state_ref[g]
            y_off = _dot(cm, st.astype(BF16)) * ecum_x[:, gl]
            ys[2 * g] = ys[2 * g] + y_off[:, 0:LANES]
            ys[2 * g + 1] = ys[2 * g + 1] + y_off[:, LANES:2 * LANES]
            upd = _dot_tn(bm, (xdt[:, gl] * dte_x[:, gl]).astype(BF16))
            state_ref[g] = st * ecum_x[cs - 1:cs, gl] + upd
        y = jnp.concatenate(ys, axis=1) + dsk_ref[...] * xs
        y = y * _silu(z_ref[rows, :].astype(F32))
        outs = []
        for g in range(2):
            yg = y[:, g * 256:(g + 1) * 256]
            ms = jnp.mean(yg * yg, axis=-1, keepdims=True)
            outs.append(yg * lax.rsqrt(ms + EPS))
        o_ref[rows, :] = (jnp.concatenate(outs, axis=1) * nw_ref[...]).astype(o_ref.dtype)
        return carry

    lax.fori_loop(0, seq // cs, chunk, 0)


def _ssd(u, small, cw, cb, dtb, alog, dsk, nw, l, batch, seq):
    t = batch * seq
    body = functools.partial(_ssd_kernel, seq)
    vec = lambda n: pl.BlockSpec((None, 1, n), lambda i: (l, 0, 0))
    return pl.pallas_call(
        body,
        grid=(batch,),
        in_specs=[pl.BlockSpec((seq, BRANCH), lambda i: (i, 8)),
                  pl.BlockSpec((seq, BRANCH), lambda i: (i, 9)),
                  pl.BlockSpec((seq, BRANCH), lambda i: (i, 10)),
                  pl.BlockSpec((seq, U_SMALL), lambda i: (i, 0)),
                  pl.BlockSpec((None, SSD_CONV, 2 * BRANCH), lambda i: (l, 0, 0)),
                  vec(2 * BRANCH), vec(LANES), vec(LANES), vec(BRANCH), vec(BRANCH)],
        out_specs=pl.BlockSpec((seq, BRANCH), lambda i: (i, 0)),
        out_shape=jax.ShapeDtypeStruct((t, BRANCH), BF16),
        scratch_shapes=[pltpu.VMEM((seq + SUBLANES, 2 * BRANCH), F32),
                        pltpu.VMEM((2, LANES, 256), F32)],
        compiler_params=_params("arbitrary"),
        name="ssd",
    )(u, u, u, small, cw, cb, dtb, alog, dsk, nw)


def _merge_kernel(h_ref, ya_ref, yb_ref, yc_ref, yd_ref, wg_ref, wb_ref, bg_ref, o_ref,
                  wg_scr, wb_scr):
    @pl.when(pl.program_id(1) == 0)
    def _():
        for i in range(4):
            _cast_rows(wg_scr.at[i], wg_ref.at[i])
            _cast_rows(wb_scr.at[i], wb_ref.at[i])

    h = h_ref[...]
    acc = None
    for i, y_ref in enumerate((ya_ref, yb_ref, yc_ref, yd_ref)):
        gate = _sigmoid(_dot(h, wg_scr[i]) + bg_ref[i:i + 1, :])
        term = gate * _dot(y_ref[...], wb_scr[i])
        acc = term if acc is None else acc + term
    o_ref[...] = acc.astype(o_ref.dtype)


def _merge(h, ys, wg, wb, bg, l, tm=1024, tn=256):
    m, d = h.shape
    yspec = pl.BlockSpec((tm, BRANCH), lambda j, i: (i, 0))
    return pl.pallas_call(
        _merge_kernel,
        grid=(d // tn, m // tm),
        in_specs=[pl.BlockSpec((tm, d), lambda j, i: (i, 0)), yspec, yspec, yspec, yspec,
                  pl.BlockSpec((None, 4, d, tn), lambda j, i: (l, 0, 0, j)),
                  pl.BlockSpec((None, 4, BRANCH, tn), lambda j, i: (l, 0, 0, j)),
                  pl.BlockSpec((None, 4, tn), lambda j, i: (l, 0, j))],
        out_specs=pl.BlockSpec((tm, tn), lambda j, i: (i, j)),
        out_shape=jax.ShapeDtypeStruct((m, d), BF16),
        scratch_shapes=[pltpu.VMEM((4, d, tn), BF16), pltpu.VMEM((4, BRANCH, tn), BF16)],
        compiler_params=_params("arbitrary", "arbitrary"),
        name="gated_merge",
    )(h, *ys, wg, wb, bg)


def _ffn_up_kernel(n_tiles, row_tiles, tiles_per_seq, tm, h_ref, wg_ref, wv_ref, cwg_ref, cwv_ref,
                   cbg_ref, cbv_ref, o_ref, raw_g, raw_v, ghist, vhist, wg_scr, wv_scr):
    halo = SUBLANES
    sub = FFN_ROWS
    s = pl.program_id(0)
    cur = jnp.minimum(s, n_tiles - 1)
    prev = jnp.maximum(s - 1, 0)
    slot = s & 1

    @pl.when(s == 0)
    def _():
        raw_g[1] = jnp.zeros(raw_g.shape[1:], F32)
        raw_v[1] = jnp.zeros(raw_v.shape[1:], F32)

    @pl.when((cur % row_tiles == 0) & (s < n_tiles))
    def _():
        _cast_rows(wg_scr, wg_ref)
        _cast_rows(wv_scr, wv_ref)

    @pl.when(prev % tiles_per_seq == 0)
    def _():
        ghist[...] = jnp.zeros_like(ghist)
        vhist[...] = jnp.zeros_like(vhist)

    def conv(raw, old, hist, cw_ref, cb_ref, r, cols):
        if r == 0:
            win = jnp.concatenate([hist[:, cols], raw[old, 0:EP_ROWS, cols]], axis=0)
        else:
            win = raw[old, r - halo:r + EP_ROWS, cols]
        out = jnp.broadcast_to(cb_ref[:, cols], (EP_ROWS, LANES))
        for j in range(FFN_CONV):
            out = out + cw_ref[j:j + 1, cols] * _shift_window(win, halo - (FFN_CONV - 1) + j,
                                                              EP_ROWS)
        return out

    def step(new, old):
        tn = raw_g.shape[2]
        for r0 in range(0, tm, sub):
            h = h_ref[r0:r0 + sub, :]
            pieces = [(r, c) for r in range(r0, r0 + sub, EP_ROWS) for c in range(0, tn, LANES)]
            dots = [(raw, w, c) for c in range(0, tn, MXU_COLS)
                    for raw, w in ((raw_g, wg_scr), (raw_v, wv_scr))]
            per_dot = len(pieces) // len(dots)
            for n, (raw, w, c) in enumerate(dots):
                raw[new, r0:r0 + sub, c:c + MXU_COLS] = _dot(h, w[:, c:c + MXU_COLS])
                for r, pc in pieces[n * per_dot:(n + 1) * per_dot]:
                    cols = slice(pc, pc + LANES)
                    gate = conv(raw_g, old, ghist, cwg_ref, cbg_ref, r, cols)
                    val = conv(raw_v, old, vhist, cwv_ref, cbv_ref, r, cols)
                    o_ref[r:r + EP_ROWS, cols] = (_silu(gate) * val).astype(o_ref.dtype)
        ghist[...] = raw_g[old, tm - halo:tm, :]
        vhist[...] = raw_v[old, tm - halo:tm, :]

    @pl.when(slot == 0)
    def _():
        step(0, 1)

    @pl.when(slot == 1)
    def _():
        step(1, 0)


def _ffn_up(h, w_up, cw, cb, l, seq, tm=1024, tn=512):
    m, d = h.shape
    nt = D_FF // tn
    rt = m // tm
    n_tiles = nt * rt
    body = functools.partial(_ffn_up_kernel, n_tiles, rt, seq // tm, tm)
    cur = lambda s: jnp.minimum(s, n_tiles - 1)
    prev = lambda s: jnp.maximum(s - 1, 0)
    return pl.pallas_call(
        body,
        grid=(n_tiles + 1,),
        in_specs=[pl.BlockSpec((tm, d), lambda s: (cur(s) % rt, 0)),
                  pl.BlockSpec((None, d, tn), lambda s: (l, 0, cur(s) // rt)),
                  pl.BlockSpec((None, d, tn), lambda s: (l, 0, cur(s) // rt + nt)),
                  pl.BlockSpec((None, FFN_CONV, tn), lambda s: (l, 0, prev(s) // rt)),
                  pl.BlockSpec((None, FFN_CONV, tn), lambda s: (l, 0, prev(s) // rt + nt)),
                  pl.BlockSpec((None, 1, tn), lambda s: (l, 0, prev(s) // rt)),
                  pl.BlockSpec((None, 1, tn), lambda s: (l, 0, prev(s) // rt + nt))],
        out_specs=pl.BlockSpec((tm, tn), lambda s: (prev(s) % rt, prev(s) // rt)),
        out_shape=jax.ShapeDtypeStruct((m, D_FF), BF16),
        scratch_shapes=[pltpu.VMEM((2, tm, tn), F32), pltpu.VMEM((2, tm, tn), F32),
                        pltpu.VMEM((SUBLANES, tn), F32), pltpu.VMEM((SUBLANES, tn), F32),
                        pltpu.VMEM((d, tn), BF16), pltpu.VMEM((d, tn), BF16)],
        compiler_params=_params("arbitrary"),
        name="ffn_up_conv_gate",
    )(h, w_up, w_up, cw, cw, cb, cb)


def kernel(x, mix_norm, w_in, diff_qk_norm, diff_lambda, diff_subln, gla_gk_w2, gla_gk_b, gla_norm, conv_dw_w, conv_dw_b, conv_ln_w, conv_ln_b, ssd_conv_w, ssd_conv_b, ssd_dt_bias, ssd_a_log, ssd_d, ssd_norm, w_branch, w_gate, b_gate, w_out, ffn_norm, ffn_w_up, ffn_conv_w, ffn_conv_b, ffn_w_down):
    batch, seq, d = x.shape
    depth = w_in.shape[0]
    t = batch * seq
    xf = x.reshape(t, d)

    w2_pad = jnp.zeros((depth, U_SMALL, 256), F32).at[:, 8:24, :].set(gla_gk_w2).astype(BF16)
    pad8 = lambda v: jnp.pad(v, ((0, 0), (0, LANES - 8)))[:, None, :]
    row = lambda v: v[:, None, :]
    qk_gain = jnp.tile(diff_qk_norm, (1, 1, 2))
    dskip = jnp.repeat(ssd_d, 64, axis=-1)[:, None, :]
    w_out_b = w_out.astype(BF16)

    for l in range(depth):
        lambda_init = 0.8 - 0.6 * math.exp(-0.3 * l)
        h, small = _rmsnorm_small(xf, row(mix_norm), w_in, l)
        u = _in_proj(h, w_in, l)
        ya = _diff_attention(u, qk_gain, diff_lambda, row(diff_subln), l, lambda_init, batch, seq)
        yb = _gla(u, small, w2_pad, row(gla_gk_b), row(gla_norm), l, batch, seq)
        yc = _conformer(u, conv_dw_w, row(conv_dw_b), row(conv_ln_w), row(conv_ln_b), l, batch, seq)
        yd = _ssd(u, small, ssd_conv_w, row(ssd_conv_b), pad8(ssd_dt_bias), pad8(ssd_a_log), dskip,
                  row(ssd_norm), l, batch, seq)
        merged = _merge(h, (ya, yb, yc, yd), w_gate, w_branch, b_gate, l)
        xf, h2 = _proj_res_norm(merged, w_out_b, l, xf, row(ffn_norm), name="out_proj_norm")
        act = _ffn_up(h2, ffn_w_up, ffn_conv_w, row(ffn_conv_b), l, seq)
        xf = _matmul_res(act, ffn_w_down, l, xf, tm=512, tn=512, name="ffn_down")
    return xf.reshape(batch, seq, d)
```

```python
import functools
import math

import jax
import jax.numpy as jnp
from jax import lax
from jax.experimental import pallas as pl
from jax.experimental.pallas import tpu as pltpu

F32 = jnp.float32
BF16 = jnp.bfloat16

EPS = 1e-6
LOG2E = 1.4426950408889634
D_MODEL = 2048
BRANCH = 512
D_FF = 5632
U_BIG = 5632
U_SMALL = 128
LANES = 128
SUBLANES = 8
VMEM_LIMIT = 56 * 1024 * 1024

GLA_CHUNK = 64
GLA_BLOCK = 8
GLA_GROUP = 4
SSD_CHUNK = 256
CONV_WIDTH = 31
SSD_CONV = 4
FFN_CONV = 3
FFN_ROWS = 128
PROJ_ROWS = 256
CAST_ROWS = 256
EP_ROWS = 64
MXU_COLS = 256


def _params(*sem):
    return pltpu.CompilerParams(dimension_semantics=sem, vmem_limit_bytes=VMEM_LIMIT)


def _sigmoid(x):
    return 0.5 * jnp.tanh(0.5 * x) + 0.5


def _silu(x):
    half = 0.5 * x
    return half * jnp.tanh(half) + half


def _softplus(x):
    return jnp.maximum(x, 0.0) + jnp.log1p(jnp.exp(-jnp.abs(x)))


def _split3(x):
    hi = x.astype(BF16)
    r1 = x - hi.astype(F32)
    mid = r1.astype(BF16)
    lo = (r1 - mid.astype(F32)).astype(BF16)
    return hi, mid, lo


def _dot(a, b):
    return jnp.dot(a, b, preferred_element_type=F32)


def _dot_nt(a, b):
    return lax.dot_general(a, b, (((1,), (1,)), ((), ())), preferred_element_type=F32)


def _dot_tn(a, b):
    return lax.dot_general(a, b, (((0,), (0,)), ((), ())), preferred_element_type=F32)


def _select_dot(sel, x):
    hi, mid, lo = _split3(x)
    return _dot(sel, hi) + _dot(sel, mid) + _dot(sel, lo)


def _dot_select(x, sel):
    hi, mid, lo = _split3(x)
    return _dot(hi, sel) + _dot(mid, sel) + _dot(lo, sel)


def _shift_window(win, shift, rows):
    total = win.shape[0]
    if shift % SUBLANES == 0:
        return win[shift:shift + rows]
    rolled = pltpu.roll(win, total - (shift % SUBLANES), 0)
    base = shift - shift % SUBLANES
    return rolled[base:base + rows]


LOW_COL0 = 3072
DT_COL0 = 5648


def _rmsnorm_small_kernel(x_ref, nw_ref, wlow_ref, wdt_ref, h_ref, s_ref, w_scr):
    @pl.when(pl.program_id(0) == 0)
    def _():
        dt0 = DT_COL0 % LANES
        low0 = LOW_COL0 % LANES
        w_scr[...] = jnp.zeros_like(w_scr)
        w_scr[0:8, :] = wdt_ref[dt0:dt0 + 8, :]
        w_scr[8:24, :] = wlow_ref[low0:low0 + 16, :]

    x = x_ref[...]
    ms = jnp.mean(x * x, axis=-1, keepdims=True)
    h = (x * lax.rsqrt(ms + EPS) * nw_ref[...]).astype(BF16)
    h_ref[...] = h
    s_ref[...] = _dot_nt(h, w_scr[...].astype(BF16))


def _rmsnorm_small(x, nw3, w_in_t, l, tm=512):
    m, d = x.shape
    return pl.pallas_call(
        _rmsnorm_small_kernel,
        grid=(m // tm,),
        in_specs=[pl.BlockSpec((tm, d), lambda i: (i, 0)),
                  pl.BlockSpec((None, 1, d), lambda i: (l, 0, 0)),
                  pl.BlockSpec((None, LANES, d), lambda i: (l, LOW_COL0 // LANES, 0)),
                  pl.BlockSpec((None, LANES, d), lambda i: (l, DT_COL0 // LANES, 0))],
        out_specs=[pl.BlockSpec((tm, d), lambda i: (i, 0)),
                   pl.BlockSpec((tm, U_SMALL), lambda i: (i, 0))],
        out_shape=[jax.ShapeDtypeStruct((m, d), BF16), jax.ShapeDtypeStruct((m, U_SMALL), F32)],
        scratch_shapes=[pltpu.VMEM((U_SMALL, d), F32)],
        compiler_params=_params("arbitrary"),
        name="rmsnorm_small_proj",
    )(x, nw3, w_in_t, w_in_t)


def _cast_rows(dst_ref, src_ref):
    for r in range(0, src_ref.shape[0], CAST_ROWS):
        dst_ref[r:r + CAST_ROWS, :] = src_ref[r:r + CAST_ROWS, :].astype(BF16)


def _matmul_cast_res_kernel(a_ref, w_ref, r_ref, o_ref, w_scr):
    @pl.when(pl.program_id(1) == 0)
    def _():
        _cast_rows(w_scr, w_ref)

    o_ref[...] = r_ref[...] + _dot(a_ref[...], w_scr[...])


def _matmul_res(a, w3, l, residual, *, tm, tn, name):
    m, k = a.shape
    n = w3.shape[-1]
    return pl.pallas_call(
        _matmul_cast_res_kernel,
        grid=(n // tn, m // tm),
        in_specs=[pl.BlockSpec((tm, k), lambda j, i: (i, 0)),
                  pl.BlockSpec((None, k, tn), lambda j, i: (l, 0, j)),
                  pl.BlockSpec((tm, tn), lambda j, i: (i, j))],
        out_specs=pl.BlockSpec((tm, tn), lambda j, i: (i, j)),
        out_shape=jax.ShapeDtypeStruct((m, n), F32),
        scratch_shapes=[pltpu.VMEM((k, tn), BF16)],
        compiler_params=_params("arbitrary", "arbitrary"),
        name=name,
    )(a, w3, residual)


IN_TN = 512
IN_SHIFT_TILE = 3072 // IN_TN
IN_SHIFT = 16


def _in_proj_kernel(a_ref, wa_ref, wb_ref, o_ref, w_scr):
    j = pl.program_id(0)
    first_row_tile = pl.program_id(1) == 0

    @pl.when(first_row_tile & (j < IN_SHIFT_TILE))
    def _():
        _cast_rows(w_scr, wa_ref)

    @pl.when(first_row_tile & (j >= IN_SHIFT_TILE))
    def _():
        w_scr[0:IN_TN - IN_SHIFT, :] = wa_ref[IN_SHIFT:IN_TN, :].astype(BF16)
        w_scr[IN_TN - IN_SHIFT:IN_TN, :] = wb_ref[0:IN_SHIFT, :].astype(BF16)

    o_ref[...] = _dot_nt(a_ref[...], w_scr[...]).astype(o_ref.dtype)


def _in_proj(a, w_in_t, l, tm=2048):
    m, k = a.shape
    per = IN_TN // LANES
    return pl.pallas_call(
        _in_proj_kernel,
        grid=(U_BIG // IN_TN, m // tm),
        in_specs=[pl.BlockSpec((tm, k), lambda j, i: (i, 0)),
                  pl.BlockSpec((None, IN_TN, k), lambda j, i: (l, j, 0)),
                  pl.BlockSpec((None, LANES, k),
                               lambda j, i: (l, per * (jnp.maximum(j, IN_SHIFT_TILE) + 1), 0))],
        out_specs=pl.BlockSpec((tm, IN_TN), lambda j, i: (i, j)),
        out_shape=jax.ShapeDtypeStruct((m, U_BIG), BF16),
        scratch_shapes=[pltpu.VMEM((IN_TN, k), BF16)],
        compiler_params=_params("arbitrary", "arbitrary"),
        name="in_proj",
    )(a, w_in_t, w_in_t)


def _proj_res_norm_kernel(tm, a_ref, w_ref, r_ref, nw_ref, x_ref, h_ref):
    for r in range(0, tm, PROJ_ROWS):
        rows = slice(r, r + PROJ_ROWS)
        xn = r_ref[rows, :] + _dot(a_ref[rows, :], w_ref[...])
        x_ref[rows, :] = xn
        ms = jnp.mean(xn * xn, axis=-1, keepdims=True)
        h_ref[rows, :] = (xn * lax.rsqrt(ms + EPS) * nw_ref[...]).astype(h_ref.dtype)


def _proj_res_norm(a, w3, l, residual, nw3, tm=512, name="proj_res_norm"):
    m, k = a.shape
    n = w3.shape[-1]
    return pl.pallas_call(
        functools.partial(_proj_res_norm_kernel, tm),
        grid=(m // tm,),
        in_specs=[pl.BlockSpec((tm, k), lambda i: (i, 0)),
                  pl.BlockSpec((None, k, n), lambda i: (l, 0, 0)),
                  pl.BlockSpec((tm, n), lambda i: (i, 0)),
                  pl.BlockSpec((None, 1, n), lambda i: (l, 0, 0))],
        out_specs=[pl.BlockSpec((tm, n), lambda i: (i, 0)),
                   pl.BlockSpec((tm, n), lambda i: (i, 0))],
        out_shape=[jax.ShapeDtypeStruct((m, n), F32), jax.ShapeDtypeStruct((m, n), BF16)],
        compiler_params=_params("arbitrary"),
        name=name,
    )(a, w3, residual, nw3)


def _diff_attn_kernel(lambda_init, seq, tq, q_ref, k_ref, v_ref, gain_ref, lam_ref, subln_ref,
                      o_ref, kn_ref, va_ref):
    lane = lax.broadcasted_iota(jnp.int32, (1, LANES), 1)
    first = lane < 64

    def halfnorm(t, g):
        sq = t * t
        s1 = jnp.sum(jnp.where(first, sq, 0.0), axis=-1, keepdims=True)
        s2 = jnp.sum(jnp.where(first, 0.0, sq), axis=-1, keepdims=True)
        ms = jnp.where(first, s1, s2) * (1.0 / 64)
        return t * lax.rsqrt(ms + EPS) * g

    kn_ref[...] = halfnorm(k_ref[...].astype(F32), gain_ref[1:2, :]).astype(BF16)
    va_ref[:, 0:LANES] = v_ref[...]
    va_ref[:, LANES:2 * LANES] = jnp.ones((seq, LANES), BF16)
    lam = lam_ref[...]
    l1 = jnp.sum(lam[0:1] * lam[1:2], axis=-1, keepdims=True)
    l2 = jnp.sum(lam[2:3] * lam[3:4], axis=-1, keepdims=True)
    lam_full = jnp.exp(l1) - jnp.exp(l2) + lambda_init

    visible = ((lax.broadcasted_iota(jnp.int32, (2 * tq, tq), 1) >> 6)
               <= ((lax.broadcasted_iota(jnp.int32, (2 * tq, tq), 0) & (tq - 1)) >> 6))

    def scores(i):
        q0 = i * tq
        qn = halfnorm(q_ref[q0:q0 + tq, :].astype(F32), gain_ref[0:1, :]) * (64 ** -0.5 * LOG2E)
        qs = jnp.concatenate([jnp.where(first, qn, 0.0), jnp.where(first, 0.0, qn)],
                             axis=0).astype(BF16)
        s_dg = jnp.where(visible, _dot_nt(qs, kn_ref[q0:q0 + tq, :]), -jnp.inf)
        s_off = _dot_nt(qs, kn_ref[0:q0, :]) if q0 else None
        return s_dg, s_off

    n_tiles = seq // tq
    nxt = scores(0)
    for i in range(n_tiles):
        q0 = i * tq
        s_dg, s_off = nxt
        nxt = scores(i + 1) if i + 1 < n_tiles else None
        m = jnp.max(s_dg, axis=-1, keepdims=True)
        if q0:
            m = jnp.maximum(m, jnp.max(s_off, axis=-1, keepdims=True))
        acc = _dot(jnp.exp2(s_dg - m).astype(BF16), va_ref[q0:q0 + tq, :])
        if q0:
            acc = acc + _dot(jnp.exp2(s_off - m).astype(BF16), va_ref[0:q0, :])
        acc = acc[:, 0:LANES] * (1.0 / acc[:, LANES:2 * LANES])
        o = acc[0:tq] - lam_full * acc[tq:2 * tq]
        ms = jnp.mean(o * o, axis=-1, keepdims=True)
        o = o * lax.rsqrt(ms + EPS) * subln_ref[...] * (1.0 - lambda_init)
        o_ref[q0:q0 + tq, :] = o.astype(o_ref.dtype)


def _diff_attention(u, gain, lam, subln, l, lambda_init, batch, seq, tq=256):
    t = batch * seq
    body = functools.partial(_diff_attn_kernel, lambda_init, seq, tq)
    return pl.pallas_call(
        body,
        grid=(batch, 4),
        in_specs=[pl.BlockSpec((seq, LANES), lambda b, h: (b, h)),
                  pl.BlockSpec((seq, LANES), lambda b, h: (b, 4 + h)),
                  pl.BlockSpec((seq, LANES), lambda b, h: (b, 8 + h)),
                  pl.BlockSpec((None, 2, LANES), lambda b, h: (l, 0, 0)),
                  pl.BlockSpec((None, 4, 64), lambda b, h: (l, 0, 0)),
                  pl.BlockSpec((None, 1, LANES), lambda b, h: (l, 0, 0))],
        out_specs=pl.BlockSpec((seq, LANES), lambda b, h: (b, h)),
        out_shape=jax.ShapeDtypeStruct((t, BRANCH), BF16),
        scratch_shapes=[pltpu.VMEM((seq, LANES), BF16), pltpu.VMEM((seq, 2 * LANES), BF16)],
        compiler_params=_params("arbitrary", "arbitrary"),
        name="diff_attention",
    )(u, u, u, gain, lam, subln)


def _gla_kernel(seq, q_ref, k_ref, v_ref, og_ref, small_ref, w2_ref, gkb_ref, nw_ref, o_ref,
                state_ref):
    cs, nb = GLA_CHUNK, GLA_CHUNK // GLA_BLOCK
    ri = lax.broadcasted_iota(jnp.int32, (cs, cs), 0)
    ci = lax.broadcasted_iota(jnp.int32, (cs, cs), 1)
    tri = (ci <= ri).astype(BF16)
    row = lax.broadcasted_iota(jnp.int32, (cs, LANES), 0)
    row_blk = row >> 3
    row_mod = row & 7
    sub3 = lax.broadcasted_iota(jnp.int32, (nb, GLA_BLOCK, LANES), 1)
    lane = lax.broadcasted_iota(jnp.int32, (1, LANES), 1)
    head0 = lane < 64
    r2 = lax.broadcasted_iota(jnp.int32, (2 * cs, cs), 0)
    c2 = lax.broadcasted_iota(jnp.int32, (2 * cs, cs), 1)
    same_blk = ((r2 & (cs - 1)) >> 3) == (c2 >> 3)
    sr = lax.broadcasted_iota(jnp.int32, (2 * LANES, LANES), 0)
    sc = lax.broadcasted_iota(jnp.int32, (2 * LANES, LANES), 1)
    state_mask = (sr >> 7) == (sc >> 6)
    state_ref[...] = jnp.zeros_like(state_ref)

    def both_heads(x):
        m0 = (lax.broadcasted_iota(jnp.int32, (1, x.shape[1]), 1) & 64) == 0
        return jnp.concatenate([jnp.where(m0, x, 0.0), jnp.where(m0, 0.0, x)], axis=0).astype(BF16)

    def group(gi, carry):
        base = gi * (GLA_GROUP * cs)
        rows = [pl.ds(pl.multiple_of(base + n * cs, cs), cs) for n in range(GLA_GROUP)]
        probs = [(n, p) for n in range(GLA_GROUP) for p in range(2)]
        lanes = [slice(p * LANES, (p + 1) * LANES) for p in range(2)]

        gk = []
        for n in range(GLA_GROUP):
            z = _dot(small_ref[rows[n], :].astype(BF16), w2_ref[...]) + gkb_ref[...]
            gk.append((jnp.minimum(z, 0.0) - jnp.log1p(jnp.exp(-jnp.abs(z)))) * (LOG2E / 16.0))
        cum = {(n, p): _select_dot(tri, gk[n][:, lanes[p]]) for n, p in probs}

        ops = {}
        for n, p in probs:
            c = cum[n, p]
            q = q_ref[rows[n], lanes[p]].astype(F32) * (64 ** -0.5)
            k = k_ref[rows[n], lanes[p]].astype(F32)
            cum3 = c.reshape(nb, GLA_BLOCK, LANES)
            q3 = q.reshape(nb, GLA_BLOCK, LANES)
            last3 = cum3[:, GLA_BLOCK - 1:GLA_BLOCK, :]
            ref3 = jnp.concatenate([jnp.zeros((1, 1, LANES), F32), last3[:nb - 1]], axis=0)
            refrow = jnp.broadcast_to(ref3, (nb, GLA_BLOCK, LANES)).reshape(cs, LANES)
            qp = q * jnp.exp2(c - refrow)
            lhs_off, rhs_off = [], []
            for blk in range(1, nb):
                lhs_off.append(jnp.where(row_blk == blk, qp, 0.0))
                m = blk * GLA_BLOCK
                kd = k[0:m] * jnp.exp2(c[m - 1:m, :] - c[0:m])
                rhs_off.append(jnp.concatenate([kd, jnp.zeros((cs - m, LANES), F32)], axis=0))
            lhs_dg, rhs_dg = [], []
            for j in range(GLA_BLOCK):
                dec = jnp.exp2(jnp.minimum(cum3 - cum3[:, j:j + 1, :], 0.0))
                lhs_dg.append(jnp.where(sub3 >= j, q3 * dec, 0.0).reshape(cs, LANES))
                rhs_dg.append(jnp.where(row_mod == j, k, 0.0))
            c_last = c[cs - 1:cs, :]
            ops[n, p] = dict(
                lhs_off=both_heads(jnp.concatenate(lhs_off, axis=1)),
                rhs_off=jnp.concatenate(rhs_off, axis=1).astype(BF16),
                lhs_dg=both_heads(jnp.concatenate(lhs_dg, axis=1)),
                rhs_dg=jnp.concatenate(rhs_dg, axis=1).astype(BF16),
                qe=(q * jnp.exp2(c)).astype(BF16),
                kdec=(k * jnp.exp2(c_last - c)).astype(BF16),
                dec=jnp.exp2(c_last))

        attn, upd = {}, {}
        for n, p in probs:
            o = ops[n, p]
            s_off = _dot_nt(o["lhs_off"], o["rhs_off"])
            s_dg = _dot_nt(o["lhs_dg"], o["rhs_dg"])
            attn[n, p] = (s_off + jnp.where(same_blk, s_dg, 0.0)).astype(BF16)
            vv = v_ref[rows[n], p * 2 * LANES:(p + 1) * 2 * LANES]
            upd[n, p] = _dot_tn(vv, o["kdec"])

        inter = {}
        for p in range(2):
            st = state_ref[p]
            for n in range(GLA_GROUP):
                inter[n, p] = _dot_nt(ops[n, p]["qe"], st.astype(BF16))
                st = jnp.where(state_mask, st * ops[n, p]["dec"] + upd[n, p], 0.0)
            state_ref[p] = st

        for n, p in probs:
            for e in range(2):
                cols = slice((2 * p + e) * LANES, (2 * p + e + 1) * LANES)
                o = (_dot(attn[n, p][e * cs:(e + 1) * cs], v_ref[rows[n], cols])
                     + inter[n, p][:, e * LANES:(e + 1) * LANES])
                ms = jnp.mean(o * o, axis=-1, keepdims=True)
                o = o * lax.rsqrt(ms + EPS) * nw_ref[...]
                o_ref[rows[n], cols] = (o * _silu(og_ref[rows[n], cols].astype(F32))
                                        ).astype(o_ref.dtype)
        return carry

    lax.fori_loop(0, seq // (GLA_GROUP * cs), group, 0)


def _gla(u, small, w2p, gkb, nw, l, batch, seq):
    t = batch * seq
    body = functools.partial(_gla_kernel, seq)
    return pl.pallas_call(
        body,
        grid=(batch,),
        in_specs=[pl.BlockSpec((seq, 256), lambda b: (b, 6)),
                  pl.BlockSpec((seq, 256), lambda b: (b, 7)),
                  pl.BlockSpec((seq, 512), lambda b: (b, 4)),
                  pl.BlockSpec((seq, 512), lambda b: (b, 5)),
                  pl.BlockSpec((seq, U_SMALL), lambda b: (b, 0)),
                  pl.BlockSpec((None, U_SMALL, 256), lambda b: (l, 0, 0)),
                  pl.BlockSpec((None, 1, 256), lambda b: (l, 0, 0)),
                  pl.BlockSpec((None, 1, LANES), lambda b: (l, 0, 0))],
        out_specs=pl.BlockSpec((seq, BRANCH), lambda b: (b, 0)),
        out_shape=jax.ShapeDtypeStruct((t, BRANCH), BF16),
        scratch_shapes=[pltpu.VMEM((2, 2 * LANES, LANES), F32)],
        compiler_params=_params("arbitrary"),
        name="gla",
    )(u, u, u, u, small, w2p, gkb, nw)


def _conformer_kernel(seq, tr, a_ref, g_ref, w_ref, b_ref, lnw_ref, lnb_ref, o_ref, gbuf, cbuf):
    halo = 32
    gbuf[0:halo, :] = jnp.zeros((halo, BRANCH), F32)
    for r in range(0, seq, 256):
        gbuf[halo + r:halo + r + 256, :] = (a_ref[r:r + 256, :].astype(F32)
                                            * _sigmoid(g_ref[r:r + 256, :].astype(F32)))

    def conv_tile(i, carry):
        r0 = pl.multiple_of(i * tr, tr)
        for c0 in range(0, BRANCH, LANES):
            win = gbuf[pl.ds(r0, tr + halo), c0:c0 + LANES]
            acc = jnp.broadcast_to(b_ref[:, c0:c0 + LANES], (tr, LANES))
            for j in range(CONV_WIDTH):
                acc = acc + w_ref[j:j + 1, c0:c0 + LANES] * _shift_window(
                    win, halo - (CONV_WIDTH - 1) + j, tr)
            cbuf[pl.ds(r0, tr), c0:c0 + LANES] = acc
        return carry

    lax.fori_loop(0, seq // tr, conv_tile, 0)

    def norm_tile(i, carry):
        r0 = pl.multiple_of(i * tr, tr)
        c = cbuf[pl.ds(r0, tr), :]
        mu = jnp.mean(c, axis=-1, keepdims=True)
        d = c - mu
        var = jnp.mean(d * d, axis=-1, keepdims=True)
        y = d * lax.rsqrt(var + EPS) * lnw_ref[...] + lnb_ref[...]
        o_ref[pl.ds(r0, tr), :] = _silu(y).astype(o_ref.dtype)
        return carry

    lax.fori_loop(0, seq // tr, norm_tile, 0)


def _conformer(u, w, b, lnw, lnb, l, batch, seq, tr=128):
    t = batch * seq
    body = functools.partial(_conformer_kernel, seq, tr)
    return pl.pallas_call(
        body,
        grid=(batch,),
        in_specs=[pl.BlockSpec((seq, BRANCH), lambda i: (i, 6)),
                  pl.BlockSpec((seq, BRANCH), lambda i: (i, 7)),
                  pl.BlockSpec((None, CONV_WIDTH, BRANCH), lambda i: (l, 0, 0)),
                  pl.BlockSpec((None, 1, BRANCH), lambda i: (l, 0, 0)),
                  pl.BlockSpec((None, 1, BRANCH), lambda i: (l, 0, 0)),
                  pl.BlockSpec((None, 1, BRANCH), lambda i: (l, 0, 0))],
        out_specs=pl.BlockSpec((seq, BRANCH), lambda i: (i, 0)),
        out_shape=jax.ShapeDtypeStruct((t, BRANCH), BF16),
        scratch_shapes=[pltpu.VMEM((seq + 32, BRANCH), F32), pltpu.VMEM((seq, BRANCH), F32)],
        compiler_params=_params("arbitrary"),
        name="conformer_conv",
    )(u, u, w, b, lnw, lnb)


def _ssd_kernel(seq, z_ref, x_ref, bc_ref, small_ref, cw_ref, cb_ref, dtb_ref, alog_ref, dsk_ref,
                nw_ref, o_ref, xbuf, state_ref):
    cs = SSD_CHUNK
    halo = SUBLANES
    xbuf[0:halo, :] = jnp.zeros((halo, 2 * BRANCH), F32)
    for r in range(0, seq, 256):
        xbuf[halo + r:halo + r + 256, 0:BRANCH] = x_ref[r:r + 256, :].astype(F32)
        xbuf[halo + r:halo + r + 256, BRANCH:2 * BRANCH] = bc_ref[r:r + 256, :].astype(F32)
    state_ref[...] = jnp.zeros_like(state_ref)

    ri = lax.broadcasted_iota(jnp.int32, (cs, cs), 0)
    ci = lax.broadcasted_iota(jnp.int32, (cs, cs), 1)
    causal = ci <= ri
    tri = causal.astype(BF16)
    er = lax.broadcasted_iota(jnp.int32, (LANES, BRANCH), 0)
    ec = lax.broadcasted_iota(jnp.int32, (LANES, BRANCH), 1)
    expand = ((ec >> 6) == er).astype(BF16)
    lane = lax.broadcasted_iota(jnp.int32, (1, LANES), 1)
    head0 = lane < 64
    a_neg = -jnp.exp(alog_ref[...])

    def chunk(c, carry):
        r0 = pl.multiple_of(c * cs, cs)
        rows = pl.ds(r0, cs)
        win = xbuf[pl.ds(r0, cs + halo), :]
        conv = jnp.broadcast_to(cb_ref[...], (cs, 2 * BRANCH))
        for j in range(SSD_CONV):
            conv = conv + cw_ref[j:j + 1, :] * _shift_window(win, halo - (SSD_CONV - 1) + j, cs)
        xc = _silu(conv)
        xs = xc[:, 0:BRANCH]
        dt = _softplus(small_ref[rows, :] + dtb_ref[...])
        cum = _select_dot(tri, dt * a_neg)
        cum2 = cum * LOG2E
        cum2_t = cum2.T
        ecum = jnp.exp(cum)
        cum_last = cum[cs - 1:cs, :]
        dt_x = _dot_select(dt, expand)
        ecum_x = _dot_select(ecum, expand)
        dte_x = _dot_select(jnp.exp(cum_last - cum), expand)
        xdt = xs * dt_x
        ys = []
        for g in range(2):
            bm = xc[:, BRANCH + g * LANES:BRANCH + (g + 1) * LANES].astype(BF16)
            cm = xc[:, BRANCH + 256 + g * LANES:BRANCH + 256 + (g + 1) * LANES].astype(BF16)
            cb = _dot_nt(cm, bm)
            gl = slice(g * 256, (g + 1) * 256)
            for pr in range(2):
                pl_ = slice(g * 256 + pr * LANES, g * 256 + (pr + 1) * LANES)
                xp = xdt[:, pl_]
                y = None
                for e in range(2):
                    h = g * 4 + pr * 2 + e
                    seg = jnp.where(causal, cum2[:, h:h + 1] - cum2_t[h:h + 1, :], -jnp.inf)
                    m = (cb * jnp.exp2(seg)).astype(BF16)
                    xh = jnp.where(head0 if e == 0 else ~head0, xp, 0.0).astype(BF16)
                    t = _dot(m, xh)
                    y = t if y is None else y + t
                ys.append(y)
            st = state_ref[g]
            y_off = _dot(cm, st.astype(BF16)) * ecum_x[:, gl]
            ys[2 * g] = ys[2 * g] + y_off[:, 0:LANES]
            ys[2 * g + 1] = ys[2 * g + 1] + y_off[:, LANES:2 * LANES]
            upd = _dot_tn(bm, (xdt[:, gl] * dte_x[:, gl]).astype(BF16))
            state_ref[g] = st * ecum_x[cs - 1:cs, gl] + upd
        y = jnp.concatenate(ys, axis=1) + dsk_ref[...] * xs
        y = y * _silu(z_ref[rows, :].astype(F32))
        outs = []
        for g in range(2):
            yg = y[:, g * 256:(g + 1) * 256]
            ms = jnp.mean(yg * yg, axis=-1, keepdims=True)
            outs.append(yg * lax.rsqrt(ms + EPS))
        o_ref[rows, :] = (jnp.concatenate(outs, axis=1) * nw_ref[...]).astype(o_ref.dtype)
        return carry

    lax.fori_loop(0, seq // cs, chunk, 0)


def _ssd(u, small, cw, cb, dtb, alog, dsk, nw, l, batch, seq):
    t = batch * seq
    body = functools.partial(_ssd_kernel, seq)
    vec = lambda n: pl.BlockSpec((None, 1, n), lambda i: (l, 0, 0))
    return pl.pallas_call(
        body,
        grid=(batch,),
        in_specs=[pl.BlockSpec((seq, BRANCH), lambda i: (i, 8)),
                  pl.BlockSpec((seq, BRANCH), lambda i: (i, 9)),
                  pl.BlockSpec((seq, BRANCH), lambda i: (i, 10)),
                  pl.BlockSpec((seq, U_SMALL), lambda i: (i, 0)),
                  pl.BlockSpec((None, SSD_CONV, 2 * BRANCH), lambda i: (l, 0, 0)),
                  vec(2 * BRANCH), vec(LANES), vec(LANES), vec(BRANCH), vec(BRANCH)],
        out_specs=pl.BlockSpec((seq, BRANCH), lambda i: (i, 0)),
        out_shape=jax.ShapeDtypeStruct((t, BRANCH), BF16),
        scratch_shapes=[pltpu.VMEM((seq + SUBLANES, 2 * BRANCH), F32),
                        pltpu.VMEM((2, LANES, 256), F32)],
        compiler_params=_params("arbitrary"),
        name="ssd",
    )(u, u, u, small, cw, cb, dtb, alog, dsk, nw)


def _merge_kernel(h_ref, ya_ref, yb_ref, yc_ref, yd_ref, wg_ref, wb_ref, bg_ref, o_ref,
                  wg_scr, wb_scr):
    @pl.when(pl.program_id(1) == 0)
    def _():
        for i in range(4):
            _cast_rows(wg_scr.at[i], wg_ref.at[i])
            _cast_rows(wb_scr.at[i], wb_ref.at[i])

    h = h_ref[...]
    acc = None
    for i, y_ref in enumerate((ya_ref, yb_ref, yc_ref, yd_ref)):
        gate = _sigmoid(_dot(h, wg_scr[i]) + bg_ref[i:i + 1, :])
        term = gate * _dot(y_ref[...], wb_scr[i])
        acc = term if acc is None else acc + term
    o_ref[...] = acc.astype(o_ref.dtype)


def _merge(h, ys, wg, wb, bg, l, tm=1024, tn=256):
    m, d = h.shape
    yspec = pl.BlockSpec((tm, BRANCH), lambda j, i: (i, 0))
    return pl.pallas_call(
        _merge_kernel,
        grid=(d // tn, m // tm),
        in_specs=[pl.BlockSpec((tm, d), lambda j, i: (i, 0)), yspec, yspec, yspec, yspec,
                  pl.BlockSpec((None, 4, d, tn), lambda j, i: (l, 0, 0, j)),
                  pl.BlockSpec((None, 4, BRANCH, tn), lambda j, i: (l, 0, 0, j)),
                  pl.BlockSpec((None, 4, tn), lambda j, i: (l, 0, j))],
        out_specs=pl.BlockSpec((tm, tn), lambda j, i: (i, j)),
        out_shape=jax.ShapeDtypeStruct((m, d), BF16),
        scratch_shapes=[pltpu.VMEM((4, d, tn), BF16), pltpu.VMEM((4, BRANCH, tn), BF16)],
        compiler_params=_params("arbitrary", "arbitrary"),
        name="gated_merge",
    )(h, *ys, wg, wb, bg)


def _ffn_up_kernel(n_tiles, row_tiles, tiles_per_seq, tm, h_ref, wg_ref, wv_ref, cwg_ref, cwv_ref,
                   cbg_ref, cbv_ref, o_ref, raw_g, raw_v, ghist, vhist, wg_scr, wv_scr):
    halo = SUBLANES
    sub = FFN_ROWS
    s = pl.program_id(0)
    cur = jnp.minimum(s, n_tiles - 1)
    prev = jnp.maximum(s - 1, 0)
    slot = s & 1

    @pl.when(s == 0)
    def _():
        raw_g[1] = jnp.zeros(raw_g.shape[1:], F32)
        raw_v[1] = jnp.zeros(raw_v.shape[1:], F32)

    @pl.when((cur % row_tiles == 0) & (s < n_tiles))
    def _():
        _cast_rows(wg_scr, wg_ref)
        _cast_rows(wv_scr, wv_ref)

    @pl.when(prev % tiles_per_seq == 0)
    def _():
        ghist[...] = jnp.zeros_like(ghist)
        vhist[...] = jnp.zeros_like(vhist)

    def conv(raw, old, hist, cw_ref, cb_ref, r, cols):
        if r == 0:
            win = jnp.concatenate([hist[:, cols], raw[old, 0:EP_ROWS, cols]], axis=0)
        else:
            win = raw[old, r - halo:r + EP_ROWS, cols]
        out = jnp.broadcast_to(cb_ref[:, cols], (EP_ROWS, LANES))
        for j in range(FFN_CONV):
            out = out + cw_ref[j:j + 1, cols] * _shift_window(win, halo - (FFN_CONV - 1) + j,
                                                              EP_ROWS)
        return out

    def step(new, old):
        tn = raw_g.shape[2]
        for r0 in range(0, tm, sub):
            h = h_ref[r0:r0 + sub, :]
            pieces = [(r, c) for r in range(r0, r0 + sub, EP_ROWS) for c in range(0, tn, LANES)]
            dots = [(raw, w, c) for c in range(0, tn, MXU_COLS)
                    for raw, w in ((raw_g, wg_scr), (raw_v, wv_scr))]
            per_dot = len(pieces) // len(dots)
            for n, (raw, w, c) in enumerate(dots):
                raw[new, r0:r0 + sub, c:c + MXU_COLS] = _dot(h, w[:, c:c + MXU_COLS])
                for r, pc in pieces[n * per_dot:(n + 1) * per_dot]:
                    cols = slice(pc, pc + LANES)
                    gate = conv(raw_g, old, ghist, cwg_ref, cbg_ref, r, cols)
                    val = conv(raw_v, old, vhist, cwv_ref, cbv_ref, r, cols)
                    o_ref[r:r + EP_ROWS, cols] = (_silu(gate) * val).astype(o_ref.dtype)
        ghist[...] = raw_g[old, tm - halo:tm, :]
        vhist[...] = raw_v[old, tm - halo:tm, :]

    @pl.when(slot == 0)
    def _():
        step(0, 1)

    @pl.when(slot == 1)
    def _():
        step(1, 0)


def _ffn_up(h, w_up, cw, cb, l, seq, tm=1024, tn=512):
    m, d = h.shape
    nt = D_FF // tn
    rt = m // tm
    n_tiles = nt * rt
    body = functools.partial(_ffn_up_kernel, n_tiles, rt, seq // tm, tm)
    cur = lambda s: jnp.minimum(s, n_tiles - 1)
    prev = lambda s: jnp.maximum(s - 1, 0)
    return pl.pallas_call(
        body,
        grid=(n_tiles + 1,),
        in_specs=[pl.BlockSpec((tm, d), lambda s: (cur(s) % rt, 0)),
                  pl.BlockSpec((None, d, tn), lambda s: (l, 0, cur(s) // rt)),
                  pl.BlockSpec((None, d, tn), lambda s: (l, 0, cur(s) // rt + nt)),
                  pl.BlockSpec((None, FFN_CONV, tn), lambda s: (l, 0, prev(s) // rt)),
                  pl.BlockSpec((None, FFN_CONV, tn), lambda s: (l, 0, prev(s) // rt + nt)),
                  pl.BlockSpec((None, 1, tn), lambda s: (l, 0, prev(s) // rt)),
                  pl.BlockSpec((None, 1, tn), lambda s: (l, 0, prev(s) // rt + nt))],
        out_specs=pl.BlockSpec((tm, tn), lambda s: (prev(s) % rt, prev(s) // rt)),
        out_shape=jax.ShapeDtypeStruct((m, D_FF), BF16),
        scratch_shapes=[pltpu.VMEM((2, tm, tn), F32), pltpu.VMEM((2, tm, tn), F32),
                        pltpu.VMEM((SUBLANES, tn), F32), pltpu.VMEM((SUBLANES, tn), F32),
                        pltpu.VMEM((d, tn), BF16), pltpu.VMEM((d, tn), BF16)],
        compiler_params=_params("arbitrary"),
        name="ffn_up_conv_gate",
    )(h, w_up, w_up, cw, cw, cb, cb)


def kernel(x, mix_norm, w_in, diff_qk_norm, diff_lambda, diff_subln, gla_gk_w2, gla_gk_b, gla_norm, conv_dw_w, conv_dw_b, conv_ln_w, conv_ln_b, ssd_conv_w, ssd_conv_b, ssd_dt_bias, ssd_a_log, ssd_d, ssd_norm, w_branch, w_gate, b_gate, w_out, ffn_norm, ffn_w_up, ffn_conv_w, ffn_conv_b, ffn_w_down):
    batch, seq, d = x.shape
    depth = w_in.shape[0]
    t = batch * seq
    xf = x.reshape(t, d)

    w2_pad = jnp.zeros((depth, U_SMALL, 256), F32).at[:, 8:24, :].set(gla_gk_w2).astype(BF16)
    pad8 = lambda v: jnp.pad(v, ((0, 0), (0, LANES - 8)))[:, None, :]
    row = lambda v: v[:, None, :]
    qk_gain = jnp.tile(diff_qk_norm, (1, 1, 2))
    dskip = jnp.repeat(ssd_d, 64, axis=-1)[:, None, :]
    w_out_b = w_out.astype(BF16)
    w_in_t = jnp.swapaxes(w_in, 1, 2)

    for l in range(depth):
        lambda_init = 0.8 - 0.6 * math.exp(-0.3 * l)
        h, small = _rmsnorm_small(xf, row(mix_norm), w_in_t, l)
        u = _in_proj(h, w_in_t, l)
        ya = _diff_attention(u, qk_gain, diff_lambda, row(diff_subln), l, lambda_init, batch, seq)
        yb = _gla(u, small, w2_pad, row(gla_gk_b), row(gla_norm), l, batch, seq)
        yc = _conformer(u, conv_dw_w, row(conv_dw_b), row(conv_ln_w), row(conv_ln_b), l, batch, seq)
        yd = _ssd(u, small, ssd_conv_w, row(ssd_conv_b), pad8(ssd_dt_bias), pad8(ssd_a_log), dskip,
                  row(ssd_norm), l, batch, seq)
        merged = _merge(h, (ya, yb, yc, yd), w_gate, w_branch, b_gate, l)
        xf, h2 = _proj_res_norm(merged, w_out_b, l, xf, row(ffn_norm), name="out_proj_norm")
        act = _ffn_up(h2, ffn_w_up, ffn_conv_w, row(ffn_conv_b), l, seq)
        xf = _matmul_res(act, ffn_w_down, l, xf, tm=512, tn=512, name="ffn_down")
    return xf.reshape(batch, seq, d)
```

```python
import functools
import math

import jax
import jax.numpy as jnp
from jax import lax
from jax.experimental import pallas as pl
from jax.experimental.pallas import tpu as pltpu

F32 = jnp.float32
BF16 = jnp.bfloat16

EPS = 1e-6
LOG2E = 1.4426950408889634
D_MODEL = 2048
BRANCH = 512
D_FF = 5632
U_BIG = 5632
U_Q = 512
U_REST = U_BIG - U_Q
U_SMALL = 128
LANES = 128
SUBLANES = 8
VMEM_LIMIT = 56 * 1024 * 1024

GLA_CHUNK = 64
GLA_BLOCK = 8
GLA_GROUP = 4
SSD_CHUNK = 256
CONV_WIDTH = 31
SSD_CONV = 4
FFN_CONV = 3
FFN_ROWS = 128
PROJ_ROWS = 256
CAST_ROWS = 256
EP_ROWS = 64
MXU_COLS = 256


def _params(*sem):
    return pltpu.CompilerParams(dimension_semantics=sem, vmem_limit_bytes=VMEM_LIMIT)


def _sigmoid(x):
    return 0.5 * jnp.tanh(0.5 * x) + 0.5


def _silu(x):
    half = 0.5 * x
    return half * jnp.tanh(half) + half


def _softplus(x):
    return jnp.maximum(x, 0.0) + jnp.log1p(jnp.exp(-jnp.abs(x)))


def _split3(x):
    hi = x.astype(BF16)
    r1 = x - hi.astype(F32)
    mid = r1.astype(BF16)
    lo = (r1 - mid.astype(F32)).astype(BF16)
    return hi, mid, lo


def _dot(a, b):
    return jnp.dot(a, b, preferred_element_type=F32)


def _dot_nt(a, b):
    return lax.dot_general(a, b, (((1,), (1,)), ((), ())), preferred_element_type=F32)


def _dot_tn(a, b):
    return lax.dot_general(a, b, (((0,), (0,)), ((), ())), preferred_element_type=F32)


def _select_dot(sel, x):
    hi, mid, lo = _split3(x)
    return _dot(sel, hi) + _dot(sel, mid) + _dot(sel, lo)


def _dot_select(x, sel):
    hi, mid, lo = _split3(x)
    return _dot(hi, sel) + _dot(mid, sel) + _dot(lo, sel)


def _shift_window(win, shift, rows):
    total = win.shape[0]
    if shift % SUBLANES == 0:
        return win[shift:shift + rows]
    rolled = pltpu.roll(win, total - (shift % SUBLANES), 0)
    base = shift - shift % SUBLANES
    return rolled[base:base + rows]


LOW_COL0 = 3072
DT_COL0 = 5648


def _rmsnorm_small_kernel(x_ref, nw_ref, wlow_ref, wdt_ref, wq_ref, h_ref, s_ref, q_ref, w_scr,
                          wq_scr):
    @pl.when(pl.program_id(0) == 0)
    def _():
        dt0 = DT_COL0 % LANES
        low0 = LOW_COL0 % LANES
        w_scr[...] = jnp.zeros_like(w_scr)
        w_scr[0:8, :] = wdt_ref[dt0:dt0 + 8, :]
        w_scr[8:24, :] = wlow_ref[low0:low0 + 16, :]
        _cast_rows(wq_scr, wq_ref)

    x = x_ref[...]
    ms = jnp.mean(x * x, axis=-1, keepdims=True)
    h = (x * lax.rsqrt(ms + EPS) * nw_ref[...]).astype(BF16)
    h_ref[...] = h
    s_ref[...] = _dot_nt(h, w_scr[...].astype(BF16))
    q_ref[...] = _dot_nt(h, wq_scr[...]).astype(q_ref.dtype)


def _rmsnorm_small(x, nw3, w_in_t, l, tm=512):
    m, d = x.shape
    return pl.pallas_call(
        _rmsnorm_small_kernel,
        grid=(m // tm,),
        in_specs=[pl.BlockSpec((tm, d), lambda i: (i, 0)),
                  pl.BlockSpec((None, 1, d), lambda i: (l, 0, 0)),
                  pl.BlockSpec((None, LANES, d), lambda i: (l, LOW_COL0 // LANES, 0)),
                  pl.BlockSpec((None, LANES, d), lambda i: (l, DT_COL0 // LANES, 0)),
                  pl.BlockSpec((None, U_Q, d), lambda i: (l, 0, 0))],
        out_specs=[pl.BlockSpec((tm, d), lambda i: (i, 0)),
                   pl.BlockSpec((tm, U_SMALL), lambda i: (i, 0)),
                   pl.BlockSpec((tm, U_Q), lambda i: (i, 0))],
        out_shape=[jax.ShapeDtypeStruct((m, d), BF16), jax.ShapeDtypeStruct((m, U_SMALL), F32),
                   jax.ShapeDtypeStruct((m, U_Q), BF16)],
        scratch_shapes=[pltpu.VMEM((U_SMALL, d), F32), pltpu.VMEM((U_Q, d), BF16)],
        compiler_params=_params("arbitrary"),
        name="rmsnorm_small_proj",
    )(x, nw3, w_in_t, w_in_t, w_in_t)


def _cast_rows(dst_ref, src_ref):
    for r in range(0, src_ref.shape[0], CAST_ROWS):
        dst_ref[r:r + CAST_ROWS, :] = src_ref[r:r + CAST_ROWS, :].astype(BF16)


def _matmul_cast_res_kernel(a_ref, w_ref, r_ref, o_ref, w_scr):
    @pl.when(pl.program_id(1) == 0)
    def _():
        _cast_rows(w_scr, w_ref)

    o_ref[...] = r_ref[...] + _dot(a_ref[...], w_scr[...])


def _matmul_res(a, w3, l, residual, *, tm, tn, name):
    m, k = a.shape
    n = w3.shape[-1]
    return pl.pallas_call(
        _matmul_cast_res_kernel,
        grid=(n // tn, m // tm),
        in_specs=[pl.BlockSpec((tm, k), lambda j, i: (i, 0)),
                  pl.BlockSpec((None, k, tn), lambda j, i: (l, 0, j)),
                  pl.BlockSpec((tm, tn), lambda j, i: (i, j))],
        out_specs=pl.BlockSpec((tm, tn), lambda j, i: (i, j)),
        out_shape=jax.ShapeDtypeStruct((m, n), F32),
        scratch_shapes=[pltpu.VMEM((k, tn), BF16)],
        compiler_params=_params("arbitrary", "arbitrary"),
        name=name,
    )(a, w3, residual)


IN_TN = 512
IN_SHIFT_TILE = 3072 // IN_TN
IN_SHIFT = 16
IN_FIRST_TILE = U_Q // IN_TN


def _in_proj_kernel(a_ref, wa_ref, wb_ref, o_ref, w_scr):
    j = pl.program_id(0) + IN_FIRST_TILE
    first_row_tile = pl.program_id(1) == 0

    @pl.when(first_row_tile & (j < IN_SHIFT_TILE))
    def _():
        _cast_rows(w_scr, wa_ref)

    @pl.when(first_row_tile & (j >= IN_SHIFT_TILE))
    def _():
        w_scr[0:IN_TN - IN_SHIFT, :] = wa_ref[IN_SHIFT:IN_TN, :].astype(BF16)
        w_scr[IN_TN - IN_SHIFT:IN_TN, :] = wb_ref[0:IN_SHIFT, :].astype(BF16)

    o_ref[...] = _dot_nt(a_ref[...], w_scr[...]).astype(o_ref.dtype)


def _in_proj(a, w_in_t, l, tm=2048):
    m, k = a.shape
    per = IN_TN // LANES
    return pl.pallas_call(
        _in_proj_kernel,
        grid=(U_REST // IN_TN, m // tm),
        in_specs=[pl.BlockSpec((tm, k), lambda j, i: (i, 0)),
                  pl.BlockSpec((None, IN_TN, k), lambda j, i: (l, j + IN_FIRST_TILE, 0)),
                  pl.BlockSpec((None, LANES, k),
                               lambda j, i: (l, per * (jnp.maximum(j + IN_FIRST_TILE,
                                                                   IN_SHIFT_TILE) + 1), 0))],
        out_specs=pl.BlockSpec((tm, IN_TN), lambda j, i: (i, j)),
        out_shape=jax.ShapeDtypeStruct((m, U_REST), BF16),
        scratch_shapes=[pltpu.VMEM((IN_TN, k), BF16)],
        compiler_params=_params("arbitrary", "arbitrary"),
        name="in_proj",
    )(a, w_in_t, w_in_t)


def _proj_res_norm_kernel(tm, a_ref, w_ref, r_ref, nw_ref, x_ref, h_ref):
    for r in range(0, tm, PROJ_ROWS):
        rows = slice(r, r + PROJ_ROWS)
        xn = r_ref[rows, :] + _dot(a_ref[rows, :], w_ref[...])
        x_ref[rows, :] = xn
        ms = jnp.mean(xn * xn, axis=-1, keepdims=True)
        h_ref[rows, :] = (xn * lax.rsqrt(ms + EPS) * nw_ref[...]).astype(h_ref.dtype)


def _proj_res_norm(a, w3, l, residual, nw3, tm=512, name="proj_res_norm"):
    m, k = a.shape
    n = w3.shape[-1]
    return pl.pallas_call(
        functools.partial(_proj_res_norm_kernel, tm),
        grid=(m // tm,),
        in_specs=[pl.BlockSpec((tm, k), lambda i: (i, 0)),
                  pl.BlockSpec((None, k, n), lambda i: (l, 0, 0)),
                  pl.BlockSpec((tm, n), lambda i: (i, 0)),
                  pl.BlockSpec((None, 1, n), lambda i: (l, 0, 0))],
        out_specs=[pl.BlockSpec((tm, n), lambda i: (i, 0)),
                   pl.BlockSpec((tm, n), lambda i: (i, 0))],
        out_shape=[jax.ShapeDtypeStruct((m, n), F32), jax.ShapeDtypeStruct((m, n), BF16)],
        compiler_params=_params("arbitrary"),
        name=name,
    )(a, w3, residual, nw3)


def _diff_attn_kernel(lambda_init, seq, tq, q_ref, k_ref, v_ref, gain_ref, lam_ref, subln_ref,
                      o_ref, kn_ref, va_ref):
    lane = lax.broadcasted_iota(jnp.int32, (1, LANES), 1)
    first = lane < 64

    def halfnorm(t, g):
        sq = t * t
        s1 = jnp.sum(jnp.where(first, sq, 0.0), axis=-1, keepdims=True)
        s2 = jnp.sum(jnp.where(first, 0.0, sq), axis=-1, keepdims=True)
        ms = jnp.where(first, s1, s2) * (1.0 / 64)
        return t * lax.rsqrt(ms + EPS) * g

    kn_ref[...] = halfnorm(k_ref[...].astype(F32), gain_ref[1:2, :]).astype(BF16)
    va_ref[:, 0:LANES] = v_ref[...]
    va_ref[:, LANES:2 * LANES] = jnp.ones((seq, LANES), BF16)
    lam = lam_ref[...]
    l1 = jnp.sum(lam[0:1] * lam[1:2], axis=-1, keepdims=True)
    l2 = jnp.sum(lam[2:3] * lam[3:4], axis=-1, keepdims=True)
    lam_full = jnp.exp(l1) - jnp.exp(l2) + lambda_init

    visible = ((lax.broadcasted_iota(jnp.int32, (2 * tq, tq), 1) >> 6)
               <= ((lax.broadcasted_iota(jnp.int32, (2 * tq, tq), 0) & (tq - 1)) >> 6))

    def scores(i):
        q0 = i * tq
        qn = halfnorm(q_ref[q0:q0 + tq, :].astype(F32), gain_ref[0:1, :]) * (64 ** -0.5 * LOG2E)
        qs = jnp.concatenate([jnp.where(first, qn, 0.0), jnp.where(first, 0.0, qn)],
                             axis=0).astype(BF16)
        s_dg = jnp.where(visible, _dot_nt(qs, kn_ref[q0:q0 + tq, :]), -jnp.inf)
        s_off = _dot_nt(qs, kn_ref[0:q0, :]) if q0 else None
        return s_dg, s_off

    n_tiles = seq // tq
    nxt = scores(0)
    for i in range(n_tiles):
        q0 = i * tq
        s_dg, s_off = nxt
        nxt = scores(i + 1) if i + 1 < n_tiles else None
        m = jnp.max(s_dg, axis=-1, keepdims=True)
        if q0:
            m = jnp.maximum(m, jnp.max(s_off, axis=-1, keepdims=True))
        acc = _dot(jnp.exp2(s_dg - m).astype(BF16), va_ref[q0:q0 + tq, :])
        if q0:
            acc = acc + _dot(jnp.exp2(s_off - m).astype(BF16), va_ref[0:q0, :])
        acc = acc[:, 0:LANES] * (1.0 / acc[:, LANES:2 * LANES])
        o = acc[0:tq] - lam_full * acc[tq:2 * tq]
        ms = jnp.mean(o * o, axis=-1, keepdims=True)
        o = o * lax.rsqrt(ms + EPS) * subln_ref[...] * (1.0 - lambda_init)
        o_ref[q0:q0 + tq, :] = o.astype(o_ref.dtype)


def _diff_attention(uq, u, gain, lam, subln, l, lambda_init, batch, seq, tq=256):
    t = batch * seq
    body = functools.partial(_diff_attn_kernel, lambda_init, seq, tq)
    return pl.pallas_call(
        body,
        grid=(batch, 4),
        in_specs=[pl.BlockSpec((seq, LANES), lambda b, h: (b, h)),
                  pl.BlockSpec((seq, LANES), lambda b, h: (b, h)),
                  pl.BlockSpec((seq, LANES), lambda b, h: (b, 4 + h)),
                  pl.BlockSpec((None, 2, LANES), lambda b, h: (l, 0, 0)),
                  pl.BlockSpec((None, 4, 64), lambda b, h: (l, 0, 0)),
                  pl.BlockSpec((None, 1, LANES), lambda b, h: (l, 0, 0))],
        out_specs=pl.BlockSpec((seq, LANES), lambda b, h: (b, h)),
        out_shape=jax.ShapeDtypeStruct((t, BRANCH), BF16),
        scratch_shapes=[pltpu.VMEM((seq, LANES), BF16), pltpu.VMEM((seq, 2 * LANES), BF16)],
        compiler_params=_params("arbitrary", "arbitrary"),
        name="diff_attention",
    )(uq, u, u, gain, lam, subln)


def _gla_kernel(seq, q_ref, k_ref, v_ref, og_ref, small_ref, w2_ref, gkb_ref, nw_ref, o_ref,
                state_ref):
    cs, nb = GLA_CHUNK, GLA_CHUNK // GLA_BLOCK
    ri = lax.broadcasted_iota(jnp.int32, (cs, cs), 0)
    ci = lax.broadcasted_iota(jnp.int32, (cs, cs), 1)
    tri = (ci <= ri).astype(BF16)
    row = lax.broadcasted_iota(jnp.int32, (cs, LANES), 0)
    row_blk = row >> 3
    row_mod = row & 7
    sub3 = lax.broadcasted_iota(jnp.int32, (nb, GLA_BLOCK, LANES), 1)
    lane = lax.broadcasted_iota(jnp.int32, (1, LANES), 1)
    head0 = lane < 64
    r2 = lax.broadcasted_iota(jnp.int32, (2 * cs, cs), 0)
    c2 = lax.broadcasted_iota(jnp.int32, (2 * cs, cs), 1)
    same_blk = ((r2 & (cs - 1)) >> 3) == (c2 >> 3)
    sr = lax.broadcasted_iota(jnp.int32, (2 * LANES, LANES), 0)
    sc = lax.broadcasted_iota(jnp.int32, (2 * LANES, LANES), 1)
    state_mask = (sr >> 7) == (sc >> 6)
    state_ref[...] = jnp.zeros_like(state_ref)

    def both_heads(x):
        m0 = (lax.broadcasted_iota(jnp.int32, (1, x.shape[1]), 1) & 64) == 0
        return jnp.concatenate([jnp.where(m0, x, 0.0), jnp.where(m0, 0.0, x)], axis=0).astype(BF16)

    def group(gi, carry):
        base = gi * (GLA_GROUP * cs)
        rows = [pl.ds(pl.multiple_of(base + n * cs, cs), cs) for n in range(GLA_GROUP)]
        probs = [(n, p) for n in range(GLA_GROUP) for p in range(2)]
        lanes = [slice(p * LANES, (p + 1) * LANES) for p in range(2)]

        gk = []
        for n in range(GLA_GROUP):
            z = _dot(small_ref[rows[n], :].astype(BF16), w2_ref[...]) + gkb_ref[...]
            gk.append((jnp.minimum(z, 0.0) - jnp.log1p(jnp.exp(-jnp.abs(z)))) * (LOG2E / 16.0))
        cum = {(n, p): _select_dot(tri, gk[n][:, lanes[p]]) for n, p in probs}

        ops = {}
        for n, p in probs:
            c = cum[n, p]
            q = q_ref[rows[n], lanes[p]].astype(F32) * (64 ** -0.5)
            k = k_ref[rows[n], lanes[p]].astype(F32)
            cum3 = c.reshape(nb, GLA_BLOCK, LANES)
            q3 = q.reshape(nb, GLA_BLOCK, LANES)
            last3 = cum3[:, GLA_BLOCK - 1:GLA_BLOCK, :]
            ref3 = jnp.concatenate([jnp.zeros((1, 1, LANES), F32), last3[:nb - 1]], axis=0)
            refrow = jnp.broadcast_to(ref3, (nb, GLA_BLOCK, LANES)).reshape(cs, LANES)
            qp = q * jnp.exp2(c - refrow)
            lhs_off, rhs_off = [], []
            for blk in range(1, nb):
                lhs_off.append(jnp.where(row_blk == blk, qp, 0.0))
                m = blk * GLA_BLOCK
                kd = k[0:m] * jnp.exp2(c[m - 1:m, :] - c[0:m])
                rhs_off.append(jnp.concatenate([kd, jnp.zeros((cs - m, LANES), F32)], axis=0))
            lhs_dg, rhs_dg = [], []
            for j in range(GLA_BLOCK):
                dec = jnp.exp2(jnp.minimum(cum3 - cum3[:, j:j + 1, :], 0.0))
                lhs_dg.append(jnp.where(sub3 >= j, q3 * dec, 0.0).reshape(cs, LANES))
                rhs_dg.append(jnp.where(row_mod == j, k, 0.0))
            c_last = c[cs - 1:cs, :]
            ops[n, p] = dict(
                lhs_off=both_heads(jnp.concatenate(lhs_off, axis=1)),
                rhs_off=jnp.concatenate(rhs_off, axis=1).astype(BF16),
                lhs_dg=both_heads(jnp.concatenate(lhs_dg, axis=1)),
                rhs_dg=jnp.concatenate(rhs_dg, axis=1).astype(BF16),
                qe=(q * jnp.exp2(c)).astype(BF16),
                kdec=(k * jnp.exp2(c_last - c)).astype(BF16),
                dec=jnp.exp2(c_last))

        attn, upd = {}, {}
        for n, p in probs:
            o = ops[n, p]
            s_off = _dot_nt(o["lhs_off"], o["rhs_off"])
            s_dg = _dot_nt(o["lhs_dg"], o["rhs_dg"])
            attn[n, p] = (s_off + jnp.where(same_blk, s_dg, 0.0)).astype(BF16)
            vv = v_ref[rows[n], p * 2 * LANES:(p + 1) * 2 * LANES]
            upd[n, p] = _dot_tn(vv, o["kdec"])

        inter = {}
        for p in range(2):
            st = state_ref[p]
            for n in range(GLA_GROUP):
                inter[n, p] = _dot_nt(ops[n, p]["qe"], st.astype(BF16))
                st = jnp.where(state_mask, st * ops[n, p]["dec"] + upd[n, p], 0.0)
            state_ref[p] = st

        for n, p in probs:
            for e in range(2):
                cols = slice((2 * p + e) * LANES, (2 * p + e + 1) * LANES)
                o = (_dot(attn[n, p][e * cs:(e + 1) * cs], v_ref[rows[n], cols])
                     + inter[n, p][:, e * LANES:(e + 1) * LANES])
                ms = jnp.mean(o * o, axis=-1, keepdims=True)
                o = o * lax.rsqrt(ms + EPS) * nw_ref[...]
                o_ref[rows[n], cols] = (o * _silu(og_ref[rows[n], cols].astype(F32))
                                        ).astype(o_ref.dtype)
        return carry

    lax.fori_loop(0, seq // (GLA_GROUP * cs), group, 0)


def _gla(u, small, w2p, gkb, nw, l, batch, seq):
    t = batch * seq
    body = functools.partial(_gla_kernel, seq)
    return pl.pallas_call(
        body,
        grid=(batch,),
        in_specs=[pl.BlockSpec((seq, 256), lambda b: (b, 4)),
                  pl.BlockSpec((seq, 256), lambda b: (b, 5)),
                  pl.BlockSpec((seq, 512), lambda b: (b, 3)),
                  pl.BlockSpec((seq, 512), lambda b: (b, 4)),
                  pl.BlockSpec((seq, U_SMALL), lambda b: (b, 0)),
                  pl.BlockSpec((None, U_SMALL, 256), lambda b: (l, 0, 0)),
                  pl.BlockSpec((None, 1, 256), lambda b: (l, 0, 0)),
                  pl.BlockSpec((None, 1, LANES), lambda b: (l, 0, 0))],
        out_specs=pl.BlockSpec((seq, BRANCH), lambda b: (b, 0)),
        out_shape=jax.ShapeDtypeStruct((t, BRANCH), BF16),
        scratch_shapes=[pltpu.VMEM((2, 2 * LANES, LANES), F32)],
        compiler_params=_params("arbitrary"),
        name="gla",
    )(u, u, u, u, small, w2p, gkb, nw)


def _conformer_kernel(seq, tr, a_ref, g_ref, w_ref, b_ref, lnw_ref, lnb_ref, o_ref, gbuf, cbuf):
    halo = 32
    gbuf[0:halo, :] = jnp.zeros((halo, BRANCH), F32)
    for r in range(0, seq, 256):
        gbuf[halo + r:halo + r + 256, :] = (a_ref[r:r + 256, :].astype(F32)
                                            * _sigmoid(g_ref[r:r + 256, :].astype(F32)))

    def conv_tile(i, carry):
        r0 = pl.multiple_of(i * tr, tr)
        for c0 in range(0, BRANCH, LANES):
            win = gbuf[pl.ds(r0, tr + halo), c0:c0 + LANES]
            acc = jnp.broadcast_to(b_ref[:, c0:c0 + LANES], (tr, LANES))
            for j in range(CONV_WIDTH):
                acc = acc + w_ref[j:j + 1, c0:c0 + LANES] * _shift_window(
                    win, halo - (CONV_WIDTH - 1) + j, tr)
            cbuf[pl.ds(r0, tr), c0:c0 + LANES] = acc
        return carry

    lax.fori_loop(0, seq // tr, conv_tile, 0)

    def norm_tile(i, carry):
        r0 = pl.multiple_of(i * tr, tr)
        c = cbuf[pl.ds(r0, tr), :]
        mu = jnp.mean(c, axis=-1, keepdims=True)
        d = c - mu
        var = jnp.mean(d * d, axis=-1, keepdims=True)
        y = d * lax.rsqrt(var + EPS) * lnw_ref[...] + lnb_ref[...]
        o_ref[pl.ds(r0, tr), :] = _silu(y).astype(o_ref.dtype)
        return carry

    lax.fori_loop(0, seq // tr, norm_tile, 0)


def _conformer(u, w, b, lnw, lnb, l, batch, seq, tr=128):
    t = batch * seq
    body = functools.partial(_conformer_kernel, seq, tr)
    return pl.pallas_call(
        body,
        grid=(batch,),
        in_specs=[pl.BlockSpec((seq, BRANCH), lambda i: (i, 5)),
                  pl.BlockSpec((seq, BRANCH), lambda i: (i, 6)),
                  pl.BlockSpec((None, CONV_WIDTH, BRANCH), lambda i: (l, 0, 0)),
                  pl.BlockSpec((None, 1, BRANCH), lambda i: (l, 0, 0)),
                  pl.BlockSpec((None, 1, BRANCH), lambda i: (l, 0, 0)),
                  pl.BlockSpec((None, 1, BRANCH), lambda i: (l, 0, 0))],
        out_specs=pl.BlockSpec((seq, BRANCH), lambda i: (i, 0)),
        out_shape=jax.ShapeDtypeStruct((t, BRANCH), BF16),
        scratch_shapes=[pltpu.VMEM((seq + 32, BRANCH), F32), pltpu.VMEM((seq, BRANCH), F32)],
        compiler_params=_params("arbitrary"),
        name="conformer_conv",
    )(u, u, w, b, lnw, lnb)


def _ssd_kernel(seq, z_ref, x_ref, bc_ref, small_ref, cw_ref, cb_ref, dtb_ref, alog_ref, dsk_ref,
                nw_ref, o_ref, xbuf, state_ref):
    cs = SSD_CHUNK
    halo = SUBLANES
    xbuf[0:halo, :] = jnp.zeros((halo, 2 * BRANCH), F32)
    for r in range(0, seq, 256):
        xbuf[halo + r:halo + r + 256, 0:BRANCH] = x_ref[r:r + 256, :].astype(F32)
        xbuf[halo + r:halo + r + 256, BRANCH:2 * BRANCH] = bc_ref[r:r + 256, :].astype(F32)
    state_ref[...] = jnp.zeros_like(state_ref)

    ri = lax.broadcasted_iota(jnp.int32, (cs, cs), 0)
    ci = lax.broadcasted_iota(jnp.int32, (cs, cs), 1)
    causal = ci <= ri
    tri = causal.astype(BF16)
    er = lax.broadcasted_iota(jnp.int32, (LANES, BRANCH), 0)
    ec = lax.broadcasted_iota(jnp.int32, (LANES, BRANCH), 1)
    expand = ((ec >> 6) == er).astype(BF16)
    lane = lax.broadcasted_iota(jnp.int32, (1, LANES), 1)
    head0 = lane < 64
    a_neg = -jnp.exp(alog_ref[...])

    def chunk(c, carry):
        r0 = pl.multiple_of(c * cs, cs)
        rows = pl.ds(r0, cs)
        win = xbuf[pl.ds(r0, cs + halo), :]
        conv = jnp.broadcast_to(cb_ref[...], (cs, 2 * BRANCH))
        for j in range(SSD_CONV):
            conv = conv + cw_ref[j:j + 1, :] * _shift_window(win, halo - (SSD_CONV - 1) + j, cs)
        xc = _silu(conv)
        xs = xc[:, 0:BRANCH]
        dt = _softplus(small_ref[rows, :] + dtb_ref[...])
        cum = _select_dot(tri, dt * a_neg)
        cum2 = cum * LOG2E
        cum2_t = cum2.T
        ecum = jnp.exp(cum)
        cum_last = cum[cs - 1:cs, :]
        dt_x = _dot_select(dt, expand)
        ecum_x = _dot_select(ecum, expand)
        dte_x = _dot_select(jnp.exp(cum_last - cum), expand)
        xdt = xs * dt_x
        ys = []
        for g in range(2):
            bm = xc[:, BRANCH + g * LANES:BRANCH + (g + 1) * LANES].astype(BF16)
            cm = xc[:, BRANCH + 256 + g * LANES:BRANCH + 256 + (g + 1) * LANES].astype(BF16)
            cb = _dot_nt(cm, bm)
            gl = slice(g * 256, (g + 1) * 256)
            for pr in range(2):
                pl_ = slice(g * 256 + pr * LANES, g * 256 + (pr + 1) * LANES)
                xp = xdt[:, pl_]
                y = None
                for e in range(2):
                    h = g * 4 + pr * 2 + e
                    seg = jnp.where(causal, cum2[:, h:h + 1] - cum2_t[h:h + 1, :], -jnp.inf)
                    m = (cb * jnp.exp2(seg)).astype(BF16)
                    xh = jnp.where(head0 if e == 0 else ~head0, xp, 0.0).astype(BF16)
                    t = _dot(m, xh)
                    y = t if y is None else y + t
                ys.append(y)
            st = state_ref[g]
            y_off = _dot(cm, st.astype(BF16)) * ecum_x[:, gl]
            ys[2 * g] = ys[2 * g] + y_off[:, 0:LANES]
            ys[2 * g + 1] = ys[2 * g + 1] + y_off[:, LANES:2 * LANES]
            upd = _dot_tn(bm, (xdt[:, gl] * dte_x[:, gl]).astype(BF16))
            state_ref[g] = st * ecum_x[cs - 1:cs, gl] + upd
        y = jnp.concatenate(ys, axis=1) + dsk_ref[...] * xs
        y = y * _silu(z_ref[rows, :].astype(F32))
        outs = []
        for g in range(2):
            yg = y[:, g * 256:(g + 1) * 256]
            ms = jnp.mean(yg * yg, axis=-1, keepdims=True)
            outs.append(yg * lax.rsqrt(ms + EPS))
        o_ref[rows, :] = (jnp.concatenate(outs, axis=1) * nw_ref[...]).astype(o_ref.dtype)
        return carry

    lax.fori_loop(0, seq // cs, chunk, 0)


def _ssd(u, small, cw, cb, dtb, alog, dsk, nw, l, batch, seq):
    t = batch * seq
    body = functools.partial(_ssd_kernel, seq)
    vec = lambda n: pl.BlockSpec((None, 1, n), lambda i: (l, 0, 0))
    return pl.pallas_call(
        body,
        grid=(batch,),
        in_specs=[pl.BlockSpec((seq, BRANCH), lambda i: (i, 7)),
                  pl.BlockSpec((seq, BRANCH), lambda i: (i, 8)),
                  pl.BlockSpec((seq, BRANCH), lambda i: (i, 9)),
                  pl.BlockSpec((seq, U_SMALL), lambda i: (i, 0)),
                  pl.BlockSpec((None, SSD_CONV, 2 * BRANCH), lambda i: (l, 0, 0)),
                  vec(2 * BRANCH), vec(LANES), vec(LANES), vec(BRANCH), vec(BRANCH)],
        out_specs=pl.BlockSpec((seq, BRANCH), lambda i: (i, 0)),
        out_shape=jax.ShapeDtypeStruct((t, BRANCH), BF16),
        scratch_shapes=[pltpu.VMEM((seq + SUBLANES, 2 * BRANCH), F32),
                        pltpu.VMEM((2, LANES, 256), F32)],
        compiler_params=_params("arbitrary"),
        name="ssd",
    )(u, u, u, small, cw, cb, dtb, alog, dsk, nw)


def _merge_kernel(h_ref, ya_ref, yb_ref, yc_ref, yd_ref, wg_ref, wb_ref, bg_ref, o_ref,
                  wg_scr, wb_scr):
    @pl.when(pl.program_id(1) == 0)
    def _():
        for i in range(4):
            _cast_rows(wg_scr.at[i], wg_ref.at[i])
            _cast_rows(wb_scr.at[i], wb_ref.at[i])

    h = h_ref[...]
    acc = None
    for i, y_ref in enumerate((ya_ref, yb_ref, yc_ref, yd_ref)):
        gate = _sigmoid(_dot(h, wg_scr[i]) + bg_ref[i:i + 1, :])
        term = gate * _dot(y_ref[...], wb_scr[i])
        acc = term if acc is None else acc + term
    o_ref[...] = acc.astype(o_ref.dtype)


def _merge(h, ys, wg, wb, bg, l, tm=1024, tn=256):
    m, d = h.shape
    yspec = pl.BlockSpec((tm, BRANCH), lambda j, i: (i, 0))
    return pl.pallas_call(
        _merge_kernel,
        grid=(d // tn, m // tm),
        in_specs=[pl.BlockSpec((tm, d), lambda j, i: (i, 0)), yspec, yspec, yspec, yspec,
                  pl.BlockSpec((None, 4, d, tn), lambda j, i: (l, 0, 0, j)),
                  pl.BlockSpec((None, 4, BRANCH, tn), lambda j, i: (l, 0, 0, j)),
                  pl.BlockSpec((None, 4, tn), lambda j, i: (l, 0, j))],
        out_specs=pl.BlockSpec((tm, tn), lambda j, i: (i, j)),
        out_shape=jax.ShapeDtypeStruct((m, d), BF16),
        scratch_shapes=[pltpu.VMEM((4, d, tn), BF16), pltpu.VMEM((4, BRANCH, tn), BF16)],
        compiler_params=_params("arbitrary", "arbitrary"),
        name="gated_merge",
    )(h, *ys, wg, wb, bg)


def _ffn_up_kernel(n_tiles, row_tiles, tiles_per_seq, tm, h_ref, wg_ref, wv_ref, cwg_ref, cwv_ref,
                   cbg_ref, cbv_ref, o_ref, raw_g, raw_v, ghist, vhist, wg_scr, wv_scr):
    halo = SUBLANES
    sub = FFN_ROWS
    s = pl.program_id(0)
    cur = jnp.minimum(s, n_tiles - 1)
    prev = jnp.maximum(s - 1, 0)
    slot = s & 1

    @pl.when(s == 0)
    def _():
        raw_g[1] = jnp.zeros(raw_g.shape[1:], F32)
        raw_v[1] = jnp.zeros(raw_v.shape[1:], F32)

    @pl.when((cur % row_tiles == 0) & (s < n_tiles))
    def _():
        _cast_rows(wg_scr, wg_ref)
        _cast_rows(wv_scr, wv_ref)

    @pl.when(prev % tiles_per_seq == 0)
    def _():
        ghist[...] = jnp.zeros_like(ghist)
        vhist[...] = jnp.zeros_like(vhist)

    def conv(raw, old, hist, cw_ref, cb_ref, r, cols):
        if r == 0:
            win = jnp.concatenate([hist[:, cols], raw[old, 0:EP_ROWS, cols]], axis=0)
        else:
            win = raw[old, r - halo:r + EP_ROWS, cols]
        out = jnp.broadcast_to(cb_ref[:, cols], (EP_ROWS, LANES))
        for j in range(FFN_CONV):
            out = out + cw_ref[j:j + 1, cols] * _shift_window(win, halo - (FFN_CONV - 1) + j,
                                                              EP_ROWS)
        return out

    def step(new, old):
        tn = raw_g.shape[2]
        for r0 in range(0, tm, sub):
            h = h_ref[r0:r0 + sub, :]
            pieces = [(r, c) for r in range(r0, r0 + sub, EP_ROWS) for c in range(0, tn, LANES)]
            dots = [(raw, w, c) for c in range(0, tn, MXU_COLS)
                    for raw, w in ((raw_g, wg_scr), (raw_v, wv_scr))]
            per_dot = len(pieces) // len(dots)
            for n, (raw, w, c) in enumerate(dots):
                raw[new, r0:r0 + sub, c:c + MXU_COLS] = _dot(h, w[:, c:c + MXU_COLS])
                for r, pc in pieces[n * per_dot:(n + 1) * per_dot]:
                    cols = slice(pc, pc + LANES)
                    gate = conv(raw_g, old, ghist, cwg_ref, cbg_ref, r, cols)
                    val = conv(raw_v, old, vhist, cwv_ref, cbv_ref, r, cols)
                    o_ref[r:r + EP_ROWS, cols] = (_silu(gate) * val).astype(o_ref.dtype)
        ghist[...] = raw_g[old, tm - halo:tm, :]
        vhist[...] = raw_v[old, tm - halo:tm, :]

    @pl.when(slot == 0)
    def _():
        step(0, 1)

    @pl.when(slot == 1)
    def _():
        step(1, 0)


def _ffn_up(h, w_up, cw, cb, l, seq, tm=1024, tn=512):
    m, d = h.shape
    nt = D_FF // tn
    rt = m // tm
    n_tiles = nt * rt
    body = functools.partial(_ffn_up_kernel, n_tiles, rt, seq // tm, tm)
    cur = lambda s: jnp.minimum(s, n_tiles - 1)
    prev = lambda s: jnp.maximum(s - 1, 0)
    return pl.pallas_call(
        body,
        grid=(n_tiles + 1,),
        in_specs=[pl.BlockSpec((tm, d), lambda s: (cur(s) % rt, 0)),
                  pl.BlockSpec((None, d, tn), lambda s: (l, 0, cur(s) // rt)),
                  pl.BlockSpec((None, d, tn), lambda s: (l, 0, cur(s) // rt + nt)),
                  pl.BlockSpec((None, FFN_CONV, tn), lambda s: (l, 0, prev(s) // rt)),
                  pl.BlockSpec((None, FFN_CONV, tn), lambda s: (l, 0, prev(s) // rt + nt)),
                  pl.BlockSpec((None, 1, tn), lambda s: (l, 0, prev(s) // rt)),
                  pl.BlockSpec((None, 1, tn), lambda s: (l, 0, prev(s) // rt + nt))],
        out_specs=pl.BlockSpec((tm, tn), lambda s: (prev(s) % rt, prev(s) // rt)),
        out_shape=jax.ShapeDtypeStruct((m, D_FF), BF16),
        scratch_shapes=[pltpu.VMEM((2, tm, tn), F32), pltpu.VMEM((2, tm, tn), F32),
                        pltpu.VMEM((SUBLANES, tn), F32), pltpu.VMEM((SUBLANES, tn), F32),
                        pltpu.VMEM((d, tn), BF16), pltpu.VMEM((d, tn), BF16)],
        compiler_params=_params("arbitrary"),
        name="ffn_up_conv_gate",
    )(h, w_up, w_up, cw, cw, cb, cb)


def kernel(x, mix_norm, w_in, diff_qk_norm, diff_lambda, diff_subln, gla_gk_w2, gla_gk_b, gla_norm, conv_dw_w, conv_dw_b, conv_ln_w, conv_ln_b, ssd_conv_w, ssd_conv_b, ssd_dt_bias, ssd_a_log, ssd_d, ssd_norm, w_branch, w_gate, b_gate, w_out, ffn_norm, ffn_w_up, ffn_conv_w, ffn_conv_b, ffn_w_down):
    batch, seq, d = x.shape
    depth = w_in.shape[0]
    t = batch * seq
    xf = x.reshape(t, d)

    w2_pad = jnp.zeros((depth, U_SMALL, 256), F32).at[:, 8:24, :].set(gla_gk_w2).astype(BF16)
    pad8 = lambda v: jnp.pad(v, ((0, 0), (0, LANES - 8)))[:, None, :]
    row = lambda v: v[:, None, :]
    qk_gain = jnp.tile(diff_qk_norm, (1, 1, 2))
    dskip = jnp.repeat(ssd_d, 64, axis=-1)[:, None, :]
    w_out_b = w_out.astype(BF16)
    w_in_t = jnp.swapaxes(w_in, 1, 2)

    for l in range(depth):
        lambda_init = 0.8 - 0.6 * math.exp(-0.3 * l)
        h, small, uq = _rmsnorm_small(xf, row(mix_norm), w_in_t, l)
        u = _in_proj(h, w_in_t, l)
        ya = _diff_attention(uq, u, qk_gain, diff_lambda, row(diff_subln), l, lambda_init, batch, seq)
        yb = _gla(u, small, w2_pad, row(gla_gk_b), row(gla_norm), l, batch, seq)
        yc = _conformer(u, conv_dw_w, row(conv_dw_b), row(conv_ln_w), row(conv_ln_b), l, batch, seq)
        yd = _ssd(u, small, ssd_conv_w, row(ssd_conv_b), pad8(ssd_dt_bias), pad8(ssd_a_log), dskip,
                  row(ssd_norm), l, batch, seq)
        merged = _merge(h, (ya, yb, yc, yd), w_gate, w_branch, b_gate, l)
        xf, h2 = _proj_res_norm(merged, w_out_b, l, xf, row(ffn_norm), name="out_proj_norm")
        act = _ffn_up(h2, ffn_w_up, ffn_conv_w, row(ffn_conv_b), l, seq)
        xf = _matmul_res(act, ffn_w_down, l, xf, tm=512, tn=512, name="ffn_down")
    return xf.reshape(batch, seq, d)
```

```python
import functools
import math

import jax
import jax.numpy as jnp
from jax import lax
from jax.experimental import pallas as pl
from jax.experimental.pallas import tpu as pltpu

F32 = jnp.float32
BF16 = jnp.bfloat16

EPS = 1e-6
LOG2E = 1.4426950408889634
D_MODEL = 2048
BRANCH = 512
D_FF = 5632
U_BIG = 5632
U_Q = 512
U_REST = U_BIG - U_Q
U_SMALL = 128
LANES = 128
SUBLANES = 8
VMEM_LIMIT = 56 * 1024 * 1024

GLA_CHUNK = 64
GLA_BLOCK = 8
GLA_GROUP = 8
SSD_CHUNK = 256
CONV_WIDTH = 31
SSD_CONV = 4
FFN_CONV = 3
FFN_ROWS = 128
PROJ_ROWS = 256
CAST_ROWS = 256
EP_ROWS = 64
MXU_COLS = 256


def _params(*sem):
    return pltpu.CompilerParams(dimension_semantics=sem, vmem_limit_bytes=VMEM_LIMIT)


def _sigmoid(x):
    return 0.5 * jnp.tanh(0.5 * x) + 0.5


def _silu(x):
    half = 0.5 * x
    return half * jnp.tanh(half) + half


def _softplus(x):
    return jnp.maximum(x, 0.0) + jnp.log1p(jnp.exp(-jnp.abs(x)))


def _split3(x):
    hi = x.astype(BF16)
    r1 = x - hi.astype(F32)
    mid = r1.astype(BF16)
    lo = (r1 - mid.astype(F32)).astype(BF16)
    return hi, mid, lo


def _dot(a, b):
    return jnp.dot(a, b, preferred_element_type=F32)


def _dot_nt(a, b):
    return lax.dot_general(a, b, (((1,), (1,)), ((), ())), preferred_element_type=F32)


def _dot_tn(a, b):
    return lax.dot_general(a, b, (((0,), (0,)), ((), ())), preferred_element_type=F32)


def _select_dot(sel, x):
    hi, mid, lo = _split3(x)
    return _dot(sel, hi) + _dot(sel, mid) + _dot(sel, lo)


def _dot_select(x, sel):
    hi, mid, lo = _split3(x)
    return _dot(hi, sel) + _dot(mid, sel) + _dot(lo, sel)


def _shift_window(win, shift, rows):
    total = win.shape[0]
    if shift % SUBLANES == 0:
        return win[shift:shift + rows]
    rolled = pltpu.roll(win, total - (shift % SUBLANES), 0)
    base = shift - shift % SUBLANES
    return rolled[base:base + rows]


LOW_COL0 = 3072
DT_COL0 = 5648


def _rmsnorm_small_kernel(x_ref, nw_ref, wlow_ref, wdt_ref, wq_ref, h_ref, s_ref, q_ref, w_scr,
                          wq_scr):
    @pl.when(pl.program_id(0) == 0)
    def _():
        dt0 = DT_COL0 % LANES
        low0 = LOW_COL0 % LANES
        w_scr[...] = jnp.zeros_like(w_scr)
        w_scr[0:8, :] = wdt_ref[dt0:dt0 + 8, :]
        w_scr[8:24, :] = wlow_ref[low0:low0 + 16, :]
        _cast_rows(wq_scr, wq_ref)

    x = x_ref[...]
    ms = jnp.mean(x * x, axis=-1, keepdims=True)
    h = (x * lax.rsqrt(ms + EPS) * nw_ref[...]).astype(BF16)
    h_ref[...] = h
    s_ref[...] = _dot_nt(h, w_scr[...].astype(BF16))
    q_ref[...] = _dot_nt(h, wq_scr[...]).astype(q_ref.dtype)


def _rmsnorm_small(x, nw3, w_in_t, l, tm=1024):
    m, d = x.shape
    return pl.pallas_call(
        _rmsnorm_small_kernel,
        grid=(m // tm,),
        in_specs=[pl.BlockSpec((tm, d), lambda i: (i, 0)),
                  pl.BlockSpec((None, 1, d), lambda i: (l, 0, 0)),
                  pl.BlockSpec((None, LANES, d), lambda i: (l, LOW_COL0 // LANES, 0)),
                  pl.BlockSpec((None, LANES, d), lambda i: (l, DT_COL0 // LANES, 0)),
                  pl.BlockSpec((None, U_Q, d), lambda i: (l, 0, 0))],
        out_specs=[pl.BlockSpec((tm, d), lambda i: (i, 0)),
                   pl.BlockSpec((tm, U_SMALL), lambda i: (i, 0)),
                   pl.BlockSpec((tm, U_Q), lambda i: (i, 0))],
        out_shape=[jax.ShapeDtypeStruct((m, d), BF16), jax.ShapeDtypeStruct((m, U_SMALL), F32),
                   jax.ShapeDtypeStruct((m, U_Q), BF16)],
        scratch_shapes=[pltpu.VMEM((U_SMALL, d), F32), pltpu.VMEM((U_Q, d), BF16)],
        compiler_params=_params("arbitrary"),
        name="rmsnorm_small_proj",
    )(x, nw3, w_in_t, w_in_t, w_in_t)


def _cast_rows(dst_ref, src_ref):
    for r in range(0, src_ref.shape[0], CAST_ROWS):
        dst_ref[r:r + CAST_ROWS, :] = src_ref[r:r + CAST_ROWS, :].astype(BF16)


def _matmul_cast_res_kernel(a_ref, w_ref, r_ref, o_ref, w_scr):
    @pl.when(pl.program_id(1) == 0)
    def _():
        _cast_rows(w_scr, w_ref)

    o_ref[...] = r_ref[...] + _dot(a_ref[...], w_scr[...])


def _matmul_res(a, w3, l, residual, *, tm, tn, name):
    m, k = a.shape
    n = w3.shape[-1]
    return pl.pallas_call(
        _matmul_cast_res_kernel,
        grid=(n // tn, m // tm),
        in_specs=[pl.BlockSpec((tm, k), lambda j, i: (i, 0)),
                  pl.BlockSpec((None, k, tn), lambda j, i: (l, 0, j)),
                  pl.BlockSpec((tm, tn), lambda j, i: (i, j))],
        out_specs=pl.BlockSpec((tm, tn), lambda j, i: (i, j)),
        out_shape=jax.ShapeDtypeStruct((m, n), F32),
        scratch_shapes=[pltpu.VMEM((k, tn), BF16)],
        compiler_params=_params("arbitrary", "arbitrary"),
        name=name,
    )(a, w3, residual)


IN_TN = 512
IN_SHIFT_TILE = 3072 // IN_TN
IN_SHIFT = 16
IN_FIRST_TILE = U_Q // IN_TN


def _in_proj_kernel(a_ref, wa_ref, wb_ref, o_ref, w_scr):
    j = pl.program_id(0) + IN_FIRST_TILE
    first_row_tile = pl.program_id(1) == 0

    @pl.when(first_row_tile & (j < IN_SHIFT_TILE))
    def _():
        _cast_rows(w_scr, wa_ref)

    @pl.when(first_row_tile & (j >= IN_SHIFT_TILE))
    def _():
        w_scr[0:IN_TN - IN_SHIFT, :] = wa_ref[IN_SHIFT:IN_TN, :].astype(BF16)
        w_scr[IN_TN - IN_SHIFT:IN_TN, :] = wb_ref[0:IN_SHIFT, :].astype(BF16)

    o_ref[...] = _dot_nt(a_ref[...], w_scr[...]).astype(o_ref.dtype)


def _in_proj(a, w_in_t, l, tm=2048):
    m, k = a.shape
    per = IN_TN // LANES
    return pl.pallas_call(
        _in_proj_kernel,
        grid=(U_REST // IN_TN, m // tm),
        in_specs=[pl.BlockSpec((tm, k), lambda j, i: (i, 0)),
                  pl.BlockSpec((None, IN_TN, k), lambda j, i: (l, j + IN_FIRST_TILE, 0)),
                  pl.BlockSpec((None, LANES, k),
                               lambda j, i: (l, per * (jnp.maximum(j + IN_FIRST_TILE,
                                                                   IN_SHIFT_TILE) + 1), 0))],
        out_specs=pl.BlockSpec((tm, IN_TN), lambda j, i: (i, j)),
        out_shape=jax.ShapeDtypeStruct((m, U_REST), BF16),
        scratch_shapes=[pltpu.VMEM((IN_TN, k), BF16)],
        compiler_params=_params("arbitrary", "arbitrary"),
        name="in_proj",
    )(a, w_in_t, w_in_t)


def _proj_res_norm_kernel(tm, a_ref, w_ref, r_ref, nw_ref, x_ref, h_ref):
    for r in range(0, tm, PROJ_ROWS):
        rows = slice(r, r + PROJ_ROWS)
        xn = r_ref[rows, :] + _dot(a_ref[rows, :], w_ref[...])
        x_ref[rows, :] = xn
        ms = jnp.mean(xn * xn, axis=-1, keepdims=True)
        h_ref[rows, :] = (xn * lax.rsqrt(ms + EPS) * nw_ref[...]).astype(h_ref.dtype)


def _proj_res_norm(a, w3, l, residual, nw3, tm=512, name="proj_res_norm"):
    m, k = a.shape
    n = w3.shape[-1]
    return pl.pallas_call(
        functools.partial(_proj_res_norm_kernel, tm),
        grid=(m // tm,),
        in_specs=[pl.BlockSpec((tm, k), lambda i: (i, 0)),
                  pl.BlockSpec((None, k, n), lambda i: (l, 0, 0)),
                  pl.BlockSpec((tm, n), lambda i: (i, 0)),
                  pl.BlockSpec((None, 1, n), lambda i: (l, 0, 0))],
        out_specs=[pl.BlockSpec((tm, n), lambda i: (i, 0)),
                   pl.BlockSpec((tm, n), lambda i: (i, 0))],
        out_shape=[jax.ShapeDtypeStruct((m, n), F32), jax.ShapeDtypeStruct((m, n), BF16)],
        compiler_params=_params("arbitrary"),
        name=name,
    )(a, w3, residual, nw3)


def _diff_attn_kernel(lambda_init, seq, tq, q_ref, k_ref, v_ref, gain_ref, lam_ref, subln_ref,
                      o_ref, kn_ref, va_ref):
    lane = lax.broadcasted_iota(jnp.int32, (1, LANES), 1)
    first = lane < 64

    def halfnorm(t, g):
        sq = t * t
        s1 = jnp.sum(jnp.where(first, sq, 0.0), axis=-1, keepdims=True)
        s2 = jnp.sum(jnp.where(first, 0.0, sq), axis=-1, keepdims=True)
        ms = jnp.where(first, s1, s2) * (1.0 / 64)
        return t * lax.rsqrt(ms + EPS) * g

    kn_ref[...] = halfnorm(k_ref[...].astype(F32), gain_ref[1:2, :]).astype(BF16)
    va_ref[:, 0:LANES] = v_ref[...]
    va_ref[:, LANES:2 * LANES] = jnp.ones((seq, LANES), BF16)
    lam = lam_ref[...]
    l1 = jnp.sum(lam[0:1] * lam[1:2], axis=-1, keepdims=True)
    l2 = jnp.sum(lam[2:3] * lam[3:4], axis=-1, keepdims=True)
    lam_full = jnp.exp(l1) - jnp.exp(l2) + lambda_init

    visible = ((lax.broadcasted_iota(jnp.int32, (2 * tq, tq), 1) >> 6)
               <= ((lax.broadcasted_iota(jnp.int32, (2 * tq, tq), 0) & (tq - 1)) >> 6))

    def scores(i):
        q0 = i * tq
        qn = halfnorm(q_ref[q0:q0 + tq, :].astype(F32), gain_ref[0:1, :]) * (64 ** -0.5 * LOG2E)
        qs = jnp.concatenate([jnp.where(first, qn, 0.0), jnp.where(first, 0.0, qn)],
                             axis=0).astype(BF16)
        s_dg = jnp.where(visible, _dot_nt(qs, kn_ref[q0:q0 + tq, :]), -jnp.inf)
        s_off = _dot_nt(qs, kn_ref[0:q0, :]) if q0 else None
        return s_dg, s_off

    n_tiles = seq // tq
    nxt = scores(0)
    for i in range(n_tiles):
        q0 = i * tq
        s_dg, s_off = nxt
        nxt = scores(i + 1) if i + 1 < n_tiles else None
        m = jnp.max(s_dg, axis=-1, keepdims=True)
        if q0:
            m = jnp.maximum(m, jnp.max(s_off, axis=-1, keepdims=True))
        acc = _dot(jnp.exp2(s_dg - m).astype(BF16), va_ref[q0:q0 + tq, :])
        if q0:
            acc = acc + _dot(jnp.exp2(s_off - m).astype(BF16), va_ref[0:q0, :])
        acc = acc[:, 0:LANES] * (1.0 / acc[:, LANES:2 * LANES])
        o = acc[0:tq] - lam_full * acc[tq:2 * tq]
        ms = jnp.mean(o * o, axis=-1, keepdims=True)
        o = o * lax.rsqrt(ms + EPS) * subln_ref[...] * (1.0 - lambda_init)
        o_ref[q0:q0 + tq, :] = o.astype(o_ref.dtype)


def _diff_attention(uq, u, gain, lam, subln, l, lambda_init, batch, seq, tq=256):
    t = batch * seq
    body = functools.partial(_diff_attn_kernel, lambda_init, seq, tq)
    return pl.pallas_call(
        body,
        grid=(batch, 4),
        in_specs=[pl.BlockSpec((seq, LANES), lambda b, h: (b, h)),
                  pl.BlockSpec((seq, LANES), lambda b, h: (b, h)),
                  pl.BlockSpec((seq, LANES), lambda b, h: (b, 4 + h)),
                  pl.BlockSpec((None, 2, LANES), lambda b, h: (l, 0, 0)),
                  pl.BlockSpec((None, 4, 64), lambda b, h: (l, 0, 0)),
                  pl.BlockSpec((None, 1, LANES), lambda b, h: (l, 0, 0))],
        out_specs=pl.BlockSpec((seq, LANES), lambda b, h: (b, h)),
        out_shape=jax.ShapeDtypeStruct((t, BRANCH), BF16),
        scratch_shapes=[pltpu.VMEM((seq, LANES), BF16), pltpu.VMEM((seq, 2 * LANES), BF16)],
        compiler_params=_params("arbitrary", "arbitrary"),
        name="diff_attention",
    )(uq, u, u, gain, lam, subln)


def _gla_kernel(seq, q_ref, k_ref, v_ref, og_ref, small_ref, w2_ref, gkb_ref, nw_ref, o_ref,
                state_ref):
    cs, nb = GLA_CHUNK, GLA_CHUNK // GLA_BLOCK
    ri = lax.broadcasted_iota(jnp.int32, (cs, cs), 0)
    ci = lax.broadcasted_iota(jnp.int32, (cs, cs), 1)
    tri = (ci <= ri).astype(BF16)
    row = lax.broadcasted_iota(jnp.int32, (cs, LANES), 0)
    row_blk = row >> 3
    row_mod = row & 7
    sub3 = lax.broadcasted_iota(jnp.int32, (nb, GLA_BLOCK, LANES), 1)
    lane = lax.broadcasted_iota(jnp.int32, (1, LANES), 1)
    head0 = lane < 64
    r2 = lax.broadcasted_iota(jnp.int32, (2 * cs, cs), 0)
    c2 = lax.broadcasted_iota(jnp.int32, (2 * cs, cs), 1)
    same_blk = ((r2 & (cs - 1)) >> 3) == (c2 >> 3)
    sr = lax.broadcasted_iota(jnp.int32, (2 * LANES, LANES), 0)
    sc = lax.broadcasted_iota(jnp.int32, (2 * LANES, LANES), 1)
    state_mask = (sr >> 7) == (sc >> 6)
    state_ref[...] = jnp.zeros_like(state_ref)

    def both_heads(x):
        m0 = (lax.broadcasted_iota(jnp.int32, (1, x.shape[1]), 1) & 64) == 0
        return jnp.concatenate([jnp.where(m0, x, 0.0), jnp.where(m0, 0.0, x)], axis=0).astype(BF16)

    def group(gi, carry):
        base = gi * (GLA_GROUP * cs)
        rows = [pl.ds(pl.multiple_of(base + n * cs, cs), cs) for n in range(GLA_GROUP)]
        probs = [(n, p) for n in range(GLA_GROUP) for p in range(2)]
        lanes = [slice(p * LANES, (p + 1) * LANES) for p in range(2)]

        gk = []
        for n in range(GLA_GROUP):
            z = _dot(small_ref[rows[n], :].astype(BF16), w2_ref[...]) + gkb_ref[...]
            gk.append((jnp.minimum(z, 0.0) - jnp.log1p(jnp.exp(-jnp.abs(z)))) * (LOG2E / 16.0))
        cum = {(n, p): _select_dot(tri, gk[n][:, lanes[p]]) for n, p in probs}

        ops = {}
        for n, p in probs:
            c = cum[n, p]
            q = q_ref[rows[n], lanes[p]].astype(F32) * (64 ** -0.5)
            k = k_ref[rows[n], lanes[p]].astype(F32)
            cum3 = c.reshape(nb, GLA_BLOCK, LANES)
            q3 = q.reshape(nb, GLA_BLOCK, LANES)
            last3 = cum3[:, GLA_BLOCK - 1:GLA_BLOCK, :]
            ref3 = jnp.concatenate([jnp.zeros((1, 1, LANES), F32), last3[:nb - 1]], axis=0)
            refrow = jnp.broadcast_to(ref3, (nb, GLA_BLOCK, LANES)).reshape(cs, LANES)
            qp = q * jnp.exp2(c - refrow)
            lhs_off, rhs_off = [], []
            for blk in range(1, nb):
                lhs_off.append(jnp.where(row_blk == blk, qp, 0.0))
                m = blk * GLA_BLOCK
                kd = k[0:m] * jnp.exp2(c[m - 1:m, :] - c[0:m])
                rhs_off.append(jnp.concatenate([kd, jnp.zeros((cs - m, LANES), F32)], axis=0))
            lhs_dg, rhs_dg = [], []
            for j in range(GLA_BLOCK):
                dec = jnp.exp2(jnp.minimum(cum3 - cum3[:, j:j + 1, :], 0.0))
                lhs_dg.append(jnp.where(sub3 >= j, q3 * dec, 0.0).reshape(cs, LANES))
                rhs_dg.append(jnp.where(row_mod == j, k, 0.0))
            c_last = c[cs - 1:cs, :]
            ops[n, p] = dict(
                lhs_off=both_heads(jnp.concatenate(lhs_off, axis=1)),
                rhs_off=jnp.concatenate(rhs_off, axis=1).astype(BF16),
                lhs_dg=both_heads(jnp.concatenate(lhs_dg, axis=1)),
                rhs_dg=jnp.concatenate(rhs_dg, axis=1).astype(BF16),
                qe=(q * jnp.exp2(c)).astype(BF16),
                kdec=(k * jnp.exp2(c_last - c)).astype(BF16),
                dec=jnp.exp2(c_last))

        attn, upd = {}, {}
        for n, p in probs:
            o = ops[n, p]
            s_off = _dot_nt(o["lhs_off"], o["rhs_off"])
            s_dg = _dot_nt(o["lhs_dg"], o["rhs_dg"])
            attn[n, p] = (s_off + jnp.where(same_blk, s_dg, 0.0)).astype(BF16)
            vv = v_ref[rows[n], p * 2 * LANES:(p + 1) * 2 * LANES]
            upd[n, p] = _dot_tn(vv, o["kdec"])

        inter = {}
        for p in range(2):
            st = state_ref[p]
            for n in range(GLA_GROUP):
                inter[n, p] = _dot_nt(ops[n, p]["qe"], st.astype(BF16))
                st = jnp.where(state_mask, st * ops[n, p]["dec"] + upd[n, p], 0.0)
            state_ref[p] = st

        for n, p in probs:
            for e in range(2):
                cols = slice((2 * p + e) * LANES, (2 * p + e + 1) * LANES)
                o = (_dot(attn[n, p][e * cs:(e + 1) * cs], v_ref[rows[n], cols])
                     + inter[n, p][:, e * LANES:(e + 1) * LANES])
                ms = jnp.mean(o * o, axis=-1, keepdims=True)
                o = o * lax.rsqrt(ms + EPS) * nw_ref[...]
                o_ref[rows[n], cols] = (o * _silu(og_ref[rows[n], cols].astype(F32))
                                        ).astype(o_ref.dtype)
        return carry

    lax.fori_loop(0, seq // (GLA_GROUP * cs), group, 0)


def _gla(u, small, w2p, gkb, nw, l, batch, seq):
    t = batch * seq
    body = functools.partial(_gla_kernel, seq)
    return pl.pallas_call(
        body,
        grid=(batch,),
        in_specs=[pl.BlockSpec((seq, 256), lambda b: (b, 4)),
                  pl.BlockSpec((seq, 256), lambda b: (b, 5)),
                  pl.BlockSpec((seq, 512), lambda b: (b, 3)),
                  pl.BlockSpec((seq, 512), lambda b: (b, 4)),
                  pl.BlockSpec((seq, U_SMALL), lambda b: (b, 0)),
                  pl.BlockSpec((None, U_SMALL, 256), lambda b: (l, 0, 0)),
                  pl.BlockSpec((None, 1, 256), lambda b: (l, 0, 0)),
                  pl.BlockSpec((None, 1, LANES), lambda b: (l, 0, 0))],
        out_specs=pl.BlockSpec((seq, BRANCH), lambda b: (b, 0)),
        out_shape=jax.ShapeDtypeStruct((t, BRANCH), BF16),
        scratch_shapes=[pltpu.VMEM((2, 2 * LANES, LANES), F32)],
        compiler_params=_params("arbitrary"),
        name="gla",
    )(u, u, u, u, small, w2p, gkb, nw)


def _conformer_kernel(seq, tr, a_ref, g_ref, w_ref, b_ref, lnw_ref, lnb_ref, o_ref, gbuf, cbuf):
    halo = 32
    gbuf[0:halo, :] = jnp.zeros((halo, BRANCH), F32)
    for r in range(0, seq, 256):
        gbuf[halo + r:halo + r + 256, :] = (a_ref[r:r + 256, :].astype(F32)
                                            * _sigmoid(g_ref[r:r + 256, :].astype(F32)))

    def conv_tile(i, carry):
        r0 = pl.multiple_of(i * tr, tr)
        for c0 in range(0, BRANCH, LANES):
            win = gbuf[pl.ds(r0, tr + halo), c0:c0 + LANES]
            acc = jnp.broadcast_to(b_ref[:, c0:c0 + LANES], (tr, LANES))
            for j in range(CONV_WIDTH):
                acc = acc + w_ref[j:j + 1, c0:c0 + LANES] * _shift_window(
                    win, halo - (CONV_WIDTH - 1) + j, tr)
            cbuf[pl.ds(r0, tr), c0:c0 + LANES] = acc
        return carry

    lax.fori_loop(0, seq // tr, conv_tile, 0)

    def norm_tile(i, carry):
        r0 = pl.multiple_of(i * tr, tr)
        c = cbuf[pl.ds(r0, tr), :]
        mu = jnp.mean(c, axis=-1, keepdims=True)
        d = c - mu
        var = jnp.mean(d * d, axis=-1, keepdims=True)
        y = d * lax.rsqrt(var + EPS) * lnw_ref[...] + lnb_ref[...]
        o_ref[pl.ds(r0, tr), :] = _silu(y).astype(o_ref.dtype)
        return carry

    lax.fori_loop(0, seq // tr, norm_tile, 0)


def _conformer(u, w, b, lnw, lnb, l, batch, seq, tr=128):
    t = batch * seq
    body = functools.partial(_conformer_kernel, seq, tr)
    return pl.pallas_call(
        body,
        grid=(batch,),
        in_specs=[pl.BlockSpec((seq, BRANCH), lambda i: (i, 5)),
                  pl.BlockSpec((seq, BRANCH), lambda i: (i, 6)),
                  pl.BlockSpec((None, CONV_WIDTH, BRANCH), lambda i: (l, 0, 0)),
                  pl.BlockSpec((None, 1, BRANCH), lambda i: (l, 0, 0)),
                  pl.BlockSpec((None, 1, BRANCH), lambda i: (l, 0, 0)),
                  pl.BlockSpec((None, 1, BRANCH), lambda i: (l, 0, 0))],
        out_specs=pl.BlockSpec((seq, BRANCH), lambda i: (i, 0)),
        out_shape=jax.ShapeDtypeStruct((t, BRANCH), BF16),
        scratch_shapes=[pltpu.VMEM((seq + 32, BRANCH), F32), pltpu.VMEM((seq, BRANCH), F32)],
        compiler_params=_params("arbitrary"),
        name="conformer_conv",
    )(u, u, w, b, lnw, lnb)


def _ssd_kernel(seq, z_ref, x_ref, bc_ref, small_ref, cw_ref, cb_ref, dtb_ref, alog_ref, dsk_ref,
                nw_ref, o_ref, xbuf, state_ref):
    cs = SSD_CHUNK
    halo = SUBLANES
    xbuf[0:halo, :] = jnp.zeros((halo, 2 * BRANCH), F32)
    for r in range(0, seq, 256):
        xbuf[halo + r:halo + r + 256, 0:BRANCH] = x_ref[r:r + 256, :].astype(F32)
        xbuf[halo + r:halo + r + 256, BRANCH:2 * BRANCH] = bc_ref[r:r + 256, :].astype(F32)
    state_ref[...] = jnp.zeros_like(state_ref)

    ri = lax.broadcasted_iota(jnp.int32, (cs, cs), 0)
    ci = lax.broadcasted_iota(jnp.int32, (cs, cs), 1)
    causal = ci <= ri
    tri = causal.astype(BF16)
    er = lax.broadcasted_iota(jnp.int32, (LANES, BRANCH), 0)
    ec = lax.broadcasted_iota(jnp.int32, (LANES, BRANCH), 1)
    expand = ((ec >> 6) == er).astype(BF16)
    lane = lax.broadcasted_iota(jnp.int32, (1, LANES), 1)
    head0 = lane < 64
    a_neg = -jnp.exp(alog_ref[...])

    def chunk(c, carry):
        r0 = pl.multiple_of(c * cs, cs)
        rows = pl.ds(r0, cs)
        win = xbuf[pl.ds(r0, cs + halo), :]
        conv = jnp.broadcast_to(cb_ref[...], (cs, 2 * BRANCH))
        for j in range(SSD_CONV):
            conv = conv + cw_ref[j:j + 1, :] * _shift_window(win, halo - (SSD_CONV - 1) + j, cs)
        xc = _silu(conv)
        xs = xc[:, 0:BRANCH]
        dt = _softplus(small_ref[rows, :] + dtb_ref[...])
        cum = _select_dot(tri, dt * a_neg)
        cum2 = cum * LOG2E
        cum2_t = cum2.T
        ecum = jnp.exp(cum)
        cum_last = cum[cs - 1:cs, :]
        dt_x = _dot_select(dt, expand)
        ecum_x = _dot_select(ecum, expand)
        dte_x = _dot_select(jnp.exp(cum_last - cum), expand)
        xdt = xs * dt_x
        ys = []
        for g in range(2):
            bm = xc[:, BRANCH + g * LANES:BRANCH + (g + 1) * LANES].astype(BF16)
            cm = xc[:, BRANCH + 256 + g * LANES:BRANCH + 256 + (g + 1) * LANES].astype(BF16)
            cb = _dot_nt(cm, bm)
            gl = slice(g * 256, (g + 1) * 256)
            for pr in range(2):
                pl_ = slice(g * 256 + pr * LANES, g * 256 + (pr + 1) * LANES)
                xp = xdt[:, pl_]
                y = None
                for e in range(2):
                    h = g * 4 + pr * 2 + e
                    seg = jnp.where(causal, cum2[:, h:h + 1] - cum2_t[h:h + 1, :], -jnp.inf)
                    m = (cb * jnp.exp2(seg)).astype(BF16)
                    xh = jnp.where(head0 if e == 0 else ~head0, xp, 0.0).astype(BF16)
                    t = _dot(m, xh)
                    y = t if y is None else y + t
                ys.append(y)
            st = state_ref[g]
            y_off = _dot(cm, st.astype(BF16)) * ecum_x[:, gl]
            ys[2 * g] = ys[2 * g] + y_off[:, 0:LANES]
            ys[2 * g + 1] = ys[2 * g + 1] + y_off[:, LANES:2 * LANES]
            upd = _dot_tn(bm, (xdt[:, gl] * dte_x[:, gl]).astype(BF16))
            state_ref[g] = st * ecum_x[cs - 1:cs, gl] + upd
        y = jnp.concatenate(ys, axis=1) + dsk_ref[...] * xs
        y = y * _silu(z_ref[rows, :].astype(F32))
        outs = []
        for g in range(2):
            yg = y[:, g * 256:(g + 1) * 256]
            ms = jnp.mean(yg * yg, axis=-1, keepdims=True)
            outs.append(yg * lax.rsqrt(ms + EPS))
        o_ref[rows, :] = (jnp.concatenate(outs, axis=1) * nw_ref[...]).astype(o_ref.dtype)
        return carry

    lax.fori_loop(0, seq // cs, chunk, 0)


def _ssd(u, small, cw, cb, dtb, alog, dsk, nw, l, batch, seq):
    t = batch * seq
    body = functools.partial(_ssd_kernel, seq)
    vec = lambda n: pl.BlockSpec((None, 1, n), lambda i: (l, 0, 0))
    return pl.pallas_call(
        body,
        grid=(batch,),
        in_specs=[pl.BlockSpec((seq, BRANCH), lambda i: (i, 7)),
                  pl.BlockSpec((seq, BRANCH), lambda i: (i, 8)),
                  pl.BlockSpec((seq, BRANCH), lambda i: (i, 9)),
                  pl.BlockSpec((seq, U_SMALL), lambda i: (i, 0)),
                  pl.BlockSpec((None, SSD_CONV, 2 * BRANCH), lambda i: (l, 0, 0)),
                  vec(2 * BRANCH), vec(LANES), vec(LANES), vec(BRANCH), vec(BRANCH)],
        out_specs=pl.BlockSpec((seq, BRANCH), lambda i: (i, 0)),
        out_shape=jax.ShapeDtypeStruct((t, BRANCH), BF16),
        scratch_shapes=[pltpu.VMEM((seq + SUBLANES, 2 * BRANCH), F32),
                        pltpu.VMEM((2, LANES, 256), F32)],
        compiler_params=_params("arbitrary"),
        name="ssd",
    )(u, u, u, small, cw, cb, dtb, alog, dsk, nw)


def _merge_kernel(h_ref, ya_ref, yb_ref, yc_ref, yd_ref, wg_ref, wb_ref, bg_ref, o_ref,
                  wg_scr, wb_scr):
    @pl.when(pl.program_id(1) == 0)
    def _():
        for i in range(4):
            _cast_rows(wg_scr.at[i], wg_ref.at[i])
            _cast_rows(wb_scr.at[i], wb_ref.at[i])

    h = h_ref[...]
    acc = None
    for i, y_ref in enumerate((ya_ref, yb_ref, yc_ref, yd_ref)):
        gate = _sigmoid(_dot(h, wg_scr[i]) + bg_ref[i:i + 1, :])
        term = gate * _dot(y_ref[...], wb_scr[i])
        acc = term if acc is None else acc + term
    o_ref[...] = acc.astype(o_ref.dtype)


def _merge(h, ys, wg, wb, bg, l, tm=1024, tn=256):
    m, d = h.shape
    yspec = pl.BlockSpec((tm, BRANCH), lambda j, i: (i, 0))
    return pl.pallas_call(
        _merge_kernel,
        grid=(d // tn, m // tm),
        in_specs=[pl.BlockSpec((tm, d), lambda j, i: (i, 0)), yspec, yspec, yspec, yspec,
                  pl.BlockSpec((None, 4, d, tn), lambda j, i: (l, 0, 0, j)),
                  pl.BlockSpec((None, 4, BRANCH, tn), lambda j, i: (l, 0, 0, j)),
                  pl.BlockSpec((None, 4, tn), lambda j, i: (l, 0, j))],
        out_specs=pl.BlockSpec((tm, tn), lambda j, i: (i, j)),
        out_shape=jax.ShapeDtypeStruct((m, d), BF16),
        scratch_shapes=[pltpu.VMEM((4, d, tn), BF16), pltpu.VMEM((4, BRANCH, tn), BF16)],
        compiler_params=_params("arbitrary", "arbitrary"),
        name="gated_merge",
    )(h, *ys, wg, wb, bg)


def _ffn_up_kernel(n_tiles, row_tiles, tiles_per_seq, tm, h_ref, wg_ref, wv_ref, cwg_ref, cwv_ref,
                   cbg_ref, cbv_ref, o_ref, raw_g, raw_v, ghist, vhist, wg_scr, wv_scr):
    halo = SUBLANES
    sub = FFN_ROWS
    s = pl.program_id(0)
    cur = jnp.minimum(s, n_tiles - 1)
    prev = jnp.maximum(s - 1, 0)
    slot = s & 1

    @pl.when(s == 0)
    def _():
        raw_g[1] = jnp.zeros(raw_g.shape[1:], F32)
        raw_v[1] = jnp.zeros(raw_v.shape[1:], F32)

    @pl.when((cur % row_tiles == 0) & (s < n_tiles))
    def _():
        _cast_rows(wg_scr, wg_ref)
        _cast_rows(wv_scr, wv_ref)

    @pl.when(prev % tiles_per_seq == 0)
    def _():
        ghist[...] = jnp.zeros_like(ghist)
        vhist[...] = jnp.zeros_like(vhist)

    def conv(raw, old, hist, cw_ref, cb_ref, r, cols):
        if r == 0:
            win = jnp.concatenate([hist[:, cols], raw[old, 0:EP_ROWS, cols]], axis=0)
        else:
            win = raw[old, r - halo:r + EP_ROWS, cols]
        out = jnp.broadcast_to(cb_ref[:, cols], (EP_ROWS, LANES))
        for j in range(FFN_CONV):
            out = out + cw_ref[j:j + 1, cols] * _shift_window(win, halo - (FFN_CONV - 1) + j,
                                                              EP_ROWS)
        return out

    def step(new, old):
        tn = raw_g.shape[2]
        for r0 in range(0, tm, sub):
            h = h_ref[r0:r0 + sub, :]
            pieces = [(r, c) for r in range(r0, r0 + sub, EP_ROWS) for c in range(0, tn, LANES)]
            dots = [(raw, w, c) for c in range(0, tn, MXU_COLS)
                    for raw, w in ((raw_g, wg_scr), (raw_v, wv_scr))]
            per_dot = len(pieces) // len(dots)
            for n, (raw, w, c) in enumerate(dots):
                raw[new, r0:r0 + sub, c:c + MXU_COLS] = _dot(h, w[:, c:c + MXU_COLS])
                for r, pc in pieces[n * per_dot:(n + 1) * per_dot]:
                    cols = slice(pc, pc + LANES)
                    gate = conv(raw_g, old, ghist, cwg_ref, cbg_ref, r, cols)
                    val = conv(raw_v, old, vhist, cwv_ref, cbv_ref, r, cols)
                    o_ref[r:r + EP_ROWS, cols] = (_silu(gate) * val).astype(o_ref.dtype)
        ghist[...] = raw_g[old, tm - halo:tm, :]
        vhist[...] = raw_v[old, tm - halo:tm, :]

    @pl.when(slot == 0)
    def _():
        step(0, 1)

    @pl.when(slot == 1)
    def _():
        step(1, 0)


def _ffn_up(h, w_up, cw, cb, l, seq, tm=1024, tn=512):
    m, d = h.shape
    nt = D_FF // tn
    rt = m // tm
    n_tiles = nt * rt
    body = functools.partial(_ffn_up_kernel, n_tiles, rt, seq // tm, tm)
    cur = lambda s: jnp.minimum(s, n_tiles - 1)
    prev = lambda s: jnp.maximum(s - 1, 0)
    return pl.pallas_call(
        body,
        grid=(n_tiles + 1,),
        in_specs=[pl.BlockSpec((tm, d), lambda s: (cur(s) % rt, 0)),
                  pl.BlockSpec((None, d, tn), lambda s: (l, 0, cur(s) // rt)),
                  pl.BlockSpec((None, d, tn), lambda s: (l, 0, cur(s) // rt + nt)),
                  pl.BlockSpec((None, FFN_CONV, tn), lambda s: (l, 0, prev(s) // rt)),
                  pl.BlockSpec((None, FFN_CONV, tn), lambda s: (l, 0, prev(s) // rt + nt)),
                  pl.BlockSpec((None, 1, tn), lambda s: (l, 0, prev(s) // rt)),
                  pl.BlockSpec((None, 1, tn), lambda s: (l, 0, prev(s) // rt + nt))],
        out_specs=pl.BlockSpec((tm, tn), lambda s: (prev(s) % rt, prev(s) // rt)),
        out_shape=jax.ShapeDtypeStruct((m, D_FF), BF16),
        scratch_shapes=[pltpu.VMEM((2, tm, tn), F32), pltpu.VMEM((2, tm, tn), F32),
                        pltpu.VMEM((SUBLANES, tn), F32), pltpu.VMEM((SUBLANES, tn), F32),
                        pltpu.VMEM((d, tn), BF16), pltpu.VMEM((d, tn), BF16)],
        compiler_params=_params("arbitrary"),
        name="ffn_up_conv_gate",
    )(h, w_up, w_up, cw, cw, cb, cb)


def kernel(x, mix_norm, w_in, diff_qk_norm, diff_lambda, diff_subln, gla_gk_w2, gla_gk_b, gla_norm, conv_dw_w, conv_dw_b, conv_ln_w, conv_ln_b, ssd_conv_w, ssd_conv_b, ssd_dt_bias, ssd_a_log, ssd_d, ssd_norm, w_branch, w_gate, b_gate, w_out, ffn_norm, ffn_w_up, ffn_conv_w, ffn_conv_b, ffn_w_down):
    batch, seq, d = x.shape
    depth = w_in.shape[0]
    t = batch * seq
    xf = x.reshape(t, d)

    w2_pad = jnp.zeros((depth, U_SMALL, 256), F32).at[:, 8:24, :].set(gla_gk_w2).astype(BF16)
    pad8 = lambda v: jnp.pad(v, ((0, 0), (0, LANES - 8)))[:, None, :]
    row = lambda v: v[:, None, :]
    qk_gain = jnp.tile(diff_qk_norm, (1, 1, 2))
    dskip = jnp.repeat(ssd_d, 64, axis=-1)[:, None, :]
    w_out_b = w_out.astype(BF16)
    w_in_t = jnp.swapaxes(w_in, 1, 2)

    for l in range(depth):
        lambda_init = 0.8 - 0.6 * math.exp(-0.3 * l)
        h, small, uq = _rmsnorm_small(xf, row(mix_norm), w_in_t, l)
        u = _in_proj(h, w_in_t, l)
        ya = _diff_attention(uq, u, qk_gain, diff_lambda, row(diff_subln), l, lambda_init, batch, seq)
        yb = _gla(u, small, w2_pad, row(gla_gk_b), row(gla_norm), l, batch, seq)
        yc = _conformer(u, conv_dw_w, row(conv_dw_b), row(conv_ln_w), row(conv_ln_b), l, batch, seq)
        yd = _ssd(u, small, ssd_conv_w, row(ssd_conv_b), pad8(ssd_dt_bias), pad8(ssd_a_log), dskip,
                  row(ssd_norm), l, batch, seq)
        merged = _merge(h, (ya, yb, yc, yd), w_gate, w_branch, b_gate, l)
        xf, h2 = _proj_res_norm(merged, w_out_b, l, xf, row(ffn_norm), name="out_proj_norm")
        act = _ffn_up(h2, ffn_w_up, ffn_conv_w, row(ffn_conv_b), l, seq)
        xf = _matmul_res(act, ffn_w_down, l, xf, tm=512, tn=512, name="ffn_down")
    return xf.reshape(batch, seq, d)
```

```python
import functools
import math

import jax
import jax.numpy as jnp
from jax import lax
from jax.experimental import pallas as pl
from jax.experimental.pallas import tpu as pltpu

F32 = jnp.float32
BF16 = jnp.bfloat16

EPS = 1e-6
LOG2E = 1.4426950408889634
D_MODEL = 2048
BRANCH = 512
D_FF = 5632
U_BIG = 5632
U_Q = 512
U_REST = U_BIG - U_Q
U_SMALL = 128
LANES = 128
SUBLANES = 8
VMEM_LIMIT = 56 * 1024 * 1024

GLA_CHUNK = 64
GLA_BLOCK = 8
GLA_GROUP = 8
SSD_CHUNK = 256
CONV_WIDTH = 31
SSD_CONV = 4
FFN_CONV = 3
FFN_ROWS = 128
PROJ_ROWS = 256
CAST_ROWS = 256
EP_ROWS = 64
MXU_COLS = 256


def _params(*sem):
    return pltpu.CompilerParams(dimension_semantics=sem, vmem_limit_bytes=VMEM_LIMIT)


def _sigmoid(x):
    return 0.5 * jnp.tanh(0.5 * x) + 0.5


def _silu(x):
    half = 0.5 * x
    return half * jnp.tanh(half) + half


def _softplus(x):
    return jnp.maximum(x, 0.0) + jnp.log1p(jnp.exp(-jnp.abs(x)))


def _split3(x):
    hi = x.astype(BF16)
    r1 = x - hi.astype(F32)
    mid = r1.astype(BF16)
    lo = (r1 - mid.astype(F32)).astype(BF16)
    return hi, mid, lo


def _dot(a, b):
    return jnp.dot(a, b, preferred_element_type=F32)


def _dot_nt(a, b):
    return lax.dot_general(a, b, (((1,), (1,)), ((), ())), preferred_element_type=F32)


def _dot_tn(a, b):
    return lax.dot_general(a, b, (((0,), (0,)), ((), ())), preferred_element_type=F32)


def _select_dot(sel, x):
    hi, mid, lo = _split3(x)
    return _dot(sel, hi) + _dot(sel, mid) + _dot(sel, lo)


def _dot_select(x, sel):
    hi, mid, lo = _split3(x)
    return _dot(hi, sel) + _dot(mid, sel) + _dot(lo, sel)


def _shift_window(win, shift, rows):
    total = win.shape[0]
    if shift % SUBLANES == 0:
        return win[shift:shift + rows]
    rolled = pltpu.roll(win, total - (shift % SUBLANES), 0)
    base = shift - shift % SUBLANES
    return rolled[base:base + rows]


LOW_COL0 = 3072
DT_COL0 = 5648


def _rmsnorm_small_kernel(x_ref, nw_ref, wlow_ref, wdt_ref, wq_ref, h_ref, s_ref, q_ref, w_scr,
                          wq_scr):
    @pl.when(pl.program_id(0) == 0)
    def _():
        dt0 = DT_COL0 % LANES
        low0 = LOW_COL0 % LANES
        w_scr[...] = jnp.zeros_like(w_scr)
        w_scr[0:8, :] = wdt_ref[dt0:dt0 + 8, :]
        w_scr[8:24, :] = wlow_ref[low0:low0 + 16, :]
        _cast_rows(wq_scr, wq_ref)

    x = x_ref[...]
    ms = jnp.mean(x * x, axis=-1, keepdims=True)
    h = (x * lax.rsqrt(ms + EPS) * nw_ref[...]).astype(BF16)
    h_ref[...] = h
    s_ref[...] = _dot_nt(h, w_scr[...].astype(BF16))
    q_ref[...] = _dot_nt(h, wq_scr[...]).astype(q_ref.dtype)


def _rmsnorm_small(x, nw3, w_in_t, l, tm=1024):
    m, d = x.shape
    return pl.pallas_call(
        _rmsnorm_small_kernel,
        grid=(m // tm,),
        in_specs=[pl.BlockSpec((tm, d), lambda i: (i, 0)),
                  pl.BlockSpec((None, 1, d), lambda i: (l, 0, 0)),
                  pl.BlockSpec((None, LANES, d), lambda i: (l, LOW_COL0 // LANES, 0)),
                  pl.BlockSpec((None, LANES, d), lambda i: (l, DT_COL0 // LANES, 0)),
                  pl.BlockSpec((None, U_Q, d), lambda i: (l, 0, 0))],
        out_specs=[pl.BlockSpec((tm, d), lambda i: (i, 0)),
                   pl.BlockSpec((tm, U_SMALL), lambda i: (i, 0)),
                   pl.BlockSpec((tm, U_Q), lambda i: (i, 0))],
        out_shape=[jax.ShapeDtypeStruct((m, d), BF16), jax.ShapeDtypeStruct((m, U_SMALL), F32),
                   jax.ShapeDtypeStruct((m, U_Q), BF16)],
        scratch_shapes=[pltpu.VMEM((U_SMALL, d), F32), pltpu.VMEM((U_Q, d), BF16)],
        compiler_params=_params("arbitrary"),
        name="rmsnorm_small_proj",
    )(x, nw3, w_in_t, w_in_t, w_in_t)


def _cast_rows(dst_ref, src_ref):
    for r in range(0, src_ref.shape[0], CAST_ROWS):
        dst_ref[r:r + CAST_ROWS, :] = src_ref[r:r + CAST_ROWS, :].astype(BF16)


def _matmul_cast_res_kernel(a_ref, w_ref, r_ref, o_ref, w_scr):
    @pl.when(pl.program_id(1) == 0)
    def _():
        _cast_rows(w_scr, w_ref)

    o_ref[...] = r_ref[...] + _dot(a_ref[...], w_scr[...])


def _matmul_res(a, w3, l, residual, *, tm, tn, name):
    m, k = a.shape
    n = w3.shape[-1]
    return pl.pallas_call(
        _matmul_cast_res_kernel,
        grid=(n // tn, m // tm),
        in_specs=[pl.BlockSpec((tm, k), lambda j, i: (i, 0)),
                  pl.BlockSpec((None, k, tn), lambda j, i: (l, 0, j)),
                  pl.BlockSpec((tm, tn), lambda j, i: (i, j))],
        out_specs=pl.BlockSpec((tm, tn), lambda j, i: (i, j)),
        out_shape=jax.ShapeDtypeStruct((m, n), F32),
        scratch_shapes=[pltpu.VMEM((k, tn), BF16)],
        compiler_params=_params("arbitrary", "arbitrary"),
        name=name,
    )(a, w3, residual)


IN_TN = 512
IN_SHIFT_TILE = 3072 // IN_TN
IN_SHIFT = 16
IN_FIRST_TILE = U_Q // IN_TN


def _in_proj_kernel(a_ref, wa_ref, wb_ref, o_ref, w_scr):
    j = pl.program_id(0) + IN_FIRST_TILE
    first_row_tile = pl.program_id(1) == 0

    @pl.when(first_row_tile & (j < IN_SHIFT_TILE))
    def _():
        _cast_rows(w_scr, wa_ref)

    @pl.when(first_row_tile & (j >= IN_SHIFT_TILE))
    def _():
        w_scr[0:IN_TN - IN_SHIFT, :] = wa_ref[IN_SHIFT:IN_TN, :].astype(BF16)
        w_scr[IN_TN - IN_SHIFT:IN_TN, :] = wb_ref[0:IN_SHIFT, :].astype(BF16)

    o_ref[...] = _dot_nt(a_ref[...], w_scr[...]).astype(o_ref.dtype)


def _in_proj(a, w_in_t, l, tm=2048):
    m, k = a.shape
    per = IN_TN // LANES
    return pl.pallas_call(
        _in_proj_kernel,
        grid=(U_REST // IN_TN, m // tm),
        in_specs=[pl.BlockSpec((tm, k), lambda j, i: (i, 0)),
                  pl.BlockSpec((None, IN_TN, k), lambda j, i: (l, j + IN_FIRST_TILE, 0)),
                  pl.BlockSpec((None, LANES, k),
                               lambda j, i: (l, per * (jnp.maximum(j + IN_FIRST_TILE,
                                                                   IN_SHIFT_TILE) + 1), 0))],
        out_specs=pl.BlockSpec((tm, IN_TN), lambda j, i: (i, j)),
        out_shape=jax.ShapeDtypeStruct((m, U_REST), BF16),
        scratch_shapes=[pltpu.VMEM((IN_TN, k), BF16)],
        compiler_params=_params("arbitrary", "arbitrary"),
        name="in_proj",
    )(a, w_in_t, w_in_t)


def _proj_res_norm_kernel(tm, a_ref, w_ref, r_ref, nw_ref, x_ref, h_ref, w_scr):
    @pl.when(pl.program_id(0) == 0)
    def _():
        _cast_rows(w_scr, w_ref)

    for r in range(0, tm, PROJ_ROWS):
        rows = slice(r, r + PROJ_ROWS)
        xn = r_ref[rows, :] + _dot(a_ref[rows, :], w_scr[...])
        x_ref[rows, :] = xn
        ms = jnp.mean(xn * xn, axis=-1, keepdims=True)
        h_ref[rows, :] = (xn * lax.rsqrt(ms + EPS) * nw_ref[...]).astype(h_ref.dtype)


def _proj_res_norm(a, w3, l, residual, nw3, tm=512, name="proj_res_norm"):
    m, k = a.shape
    n = w3.shape[-1]
    return pl.pallas_call(
        functools.partial(_proj_res_norm_kernel, tm),
        grid=(m // tm,),
        in_specs=[pl.BlockSpec((tm, k), lambda i: (i, 0)),
                  pl.BlockSpec((None, k, n), lambda i: (l, 0, 0), pipeline_mode=pl.Buffered(1)),
                  pl.BlockSpec((tm, n), lambda i: (i, 0)),
                  pl.BlockSpec((None, 1, n), lambda i: (l, 0, 0))],
        out_specs=[pl.BlockSpec((tm, n), lambda i: (i, 0)),
                   pl.BlockSpec((tm, n), lambda i: (i, 0))],
        out_shape=[jax.ShapeDtypeStruct((m, n), F32), jax.ShapeDtypeStruct((m, n), BF16)],
        scratch_shapes=[pltpu.VMEM((k, n), BF16)],
        compiler_params=_params("arbitrary"),
        name=name,
    )(a, w3, residual, nw3)


def _diff_attn_kernel(lambda_init, seq, tq, q_ref, k_ref, v_ref, gain_ref, lam_ref, subln_ref,
                      o_ref, kn_ref, va_ref):
    lane = lax.broadcasted_iota(jnp.int32, (1, LANES), 1)
    first = lane < 64

    def halfnorm(t, g):
        sq = t * t
        s1 = jnp.sum(jnp.where(first, sq, 0.0), axis=-1, keepdims=True)
        s2 = jnp.sum(jnp.where(first, 0.0, sq), axis=-1, keepdims=True)
        ms = jnp.where(first, s1, s2) * (1.0 / 64)
        return t * lax.rsqrt(ms + EPS) * g

    kn_ref[...] = halfnorm(k_ref[...].astype(F32), gain_ref[1:2, :]).astype(BF16)
    va_ref[:, 0:LANES] = v_ref[...]
    va_ref[:, LANES:2 * LANES] = jnp.ones((seq, LANES), BF16)
    lam = lam_ref[...]
    l1 = jnp.sum(lam[0:1] * lam[1:2], axis=-1, keepdims=True)
    l2 = jnp.sum(lam[2:3] * lam[3:4], axis=-1, keepdims=True)
    lam_full = jnp.exp(l1) - jnp.exp(l2) + lambda_init

    visible = ((lax.broadcasted_iota(jnp.int32, (2 * tq, tq), 1) >> 6)
               <= ((lax.broadcasted_iota(jnp.int32, (2 * tq, tq), 0) & (tq - 1)) >> 6))

    def scores(i):
        q0 = i * tq
        qn = halfnorm(q_ref[q0:q0 + tq, :].astype(F32), gain_ref[0:1, :]) * (64 ** -0.5 * LOG2E)
        qs = jnp.concatenate([jnp.where(first, qn, 0.0), jnp.where(first, 0.0, qn)],
                             axis=0).astype(BF16)
        s_dg = jnp.where(visible, _dot_nt(qs, kn_ref[q0:q0 + tq, :]), -jnp.inf)
        s_off = _dot_nt(qs, kn_ref[0:q0, :]) if q0 else None
        return s_dg, s_off

    n_tiles = seq // tq
    nxt = scores(0)
    for i in range(n_tiles):
        q0 = i * tq
        s_dg, s_off = nxt
        nxt = scores(i + 1) if i + 1 < n_tiles else None
        m = jnp.max(s_dg, axis=-1, keepdims=True)
        if q0:
            m = jnp.maximum(m, jnp.max(s_off, axis=-1, keepdims=True))
        acc = _dot(jnp.exp2(s_dg - m).astype(BF16), va_ref[q0:q0 + tq, :])
        if q0:
            acc = acc + _dot(jnp.exp2(s_off - m).astype(BF16), va_ref[0:q0, :])
        acc = acc[:, 0:LANES] * (1.0 / acc[:, LANES:2 * LANES])
        o = acc[0:tq] - lam_full * acc[tq:2 * tq]
        ms = jnp.mean(o * o, axis=-1, keepdims=True)
        o = o * lax.rsqrt(ms + EPS) * subln_ref[...] * (1.0 - lambda_init)
        o_ref[q0:q0 + tq, :] = o.astype(o_ref.dtype)


def _diff_attention(uq, u, gain, lam, subln, l, lambda_init, batch, seq, tq=256):
    t = batch * seq
    body = functools.partial(_diff_attn_kernel, lambda_init, seq, tq)
    return pl.pallas_call(
        body,
        grid=(batch, 4),
        in_specs=[pl.BlockSpec((seq, LANES), lambda b, h: (b, h)),
                  pl.BlockSpec((seq, LANES), lambda b, h: (b, h)),
                  pl.BlockSpec((seq, LANES), lambda b, h: (b, 4 + h)),
                  pl.BlockSpec((None, 2, LANES), lambda b, h: (l, 0, 0)),
                  pl.BlockSpec((None, 4, 64), lambda b, h: (l, 0, 0)),
                  pl.BlockSpec((None, 1, LANES), lambda b, h: (l, 0, 0))],
        out_specs=pl.BlockSpec((seq, LANES), lambda b, h: (b, h)),
        out_shape=jax.ShapeDtypeStruct((t, BRANCH), BF16),
        scratch_shapes=[pltpu.VMEM((seq, LANES), BF16), pltpu.VMEM((seq, 2 * LANES), BF16)],
        compiler_params=_params("arbitrary", "arbitrary"),
        name="diff_attention",
    )(uq, u, u, gain, lam, subln)


def _gla_kernel(seq, q_ref, k_ref, v_ref, og_ref, small_ref, w2_ref, gkb_ref, nw_ref, o_ref,
                state_ref):
    cs, nb = GLA_CHUNK, GLA_CHUNK // GLA_BLOCK
    ri = lax.broadcasted_iota(jnp.int32, (cs, cs), 0)
    ci = lax.broadcasted_iota(jnp.int32, (cs, cs), 1)
    tri = (ci <= ri).astype(BF16)
    row = lax.broadcasted_iota(jnp.int32, (cs, LANES), 0)
    row_blk = row >> 3
    row_mod = row & 7
    sub3 = lax.broadcasted_iota(jnp.int32, (nb, GLA_BLOCK, LANES), 1)
    lane = lax.broadcasted_iota(jnp.int32, (1, LANES), 1)
    head0 = lane < 64
    r2 = lax.broadcasted_iota(jnp.int32, (2 * cs, cs), 0)
    c2 = lax.broadcasted_iota(jnp.int32, (2 * cs, cs), 1)
    same_blk = ((r2 & (cs - 1)) >> 3) == (c2 >> 3)
    sr = lax.broadcasted_iota(jnp.int32, (2 * LANES, LANES), 0)
    sc = lax.broadcasted_iota(jnp.int32, (2 * LANES, LANES), 1)
    state_mask = (sr >> 7) == (sc >> 6)
    state_ref[...] = jnp.zeros_like(state_ref)

    def both_heads(x):
        m0 = (lax.broadcasted_iota(jnp.int32, (1, x.shape[1]), 1) & 64) == 0
        return jnp.concatenate([jnp.where(m0, x, 0.0), jnp.where(m0, 0.0, x)], axis=0).astype(BF16)

    def group(gi, carry):
        base = gi * (GLA_GROUP * cs)
        rows = [pl.ds(pl.multiple_of(base + n * cs, cs), cs) for n in range(GLA_GROUP)]
        probs = [(n, p) for n in range(GLA_GROUP) for p in range(2)]
        lanes = [slice(p * LANES, (p + 1) * LANES) for p in range(2)]

        gk = []
        for n in range(GLA_GROUP):
            z = _dot(small_ref[rows[n], :].astype(BF16), w2_ref[...]) + gkb_ref[...]
            gk.append((jnp.minimum(z, 0.0) - jnp.log1p(jnp.exp(-jnp.abs(z)))) * (LOG2E / 16.0))
        cum = {(n, p): _select_dot(tri, gk[n][:, lanes[p]]) for n, p in probs}

        ops = {}
        for n, p in probs:
            c = cum[n, p]
            q = q_ref[rows[n], lanes[p]].astype(F32) * (64 ** -0.5)
            k = k_ref[rows[n], lanes[p]].astype(F32)
            cum3 = c.reshape(nb, GLA_BLOCK, LANES)
            q3 = q.reshape(nb, GLA_BLOCK, LANES)
            last3 = cum3[:, GLA_BLOCK - 1:GLA_BLOCK, :]
            ref3 = jnp.concatenate([jnp.zeros((1, 1, LANES), F32), last3[:nb - 1]], axis=0)
            refrow = jnp.broadcast_to(ref3, (nb, GLA_BLOCK, LANES)).reshape(cs, LANES)
            qp = q * jnp.exp2(c - refrow)
            lhs_off, rhs_off = [], []
            for blk in range(1, nb):
                lhs_off.append(jnp.where(row_blk == blk, qp, 0.0))
                m = blk * GLA_BLOCK
                kd = k[0:m] * jnp.exp2(c[m - 1:m, :] - c[0:m])
                rhs_off.append(jnp.concatenate([kd, jnp.zeros((cs - m, LANES), F32)], axis=0))
            lhs_dg, rhs_dg = [], []
            for j in range(GLA_BLOCK):
                dec = jnp.exp2(jnp.minimum(cum3 - cum3[:, j:j + 1, :], 0.0))
                lhs_dg.append(jnp.where(sub3 >= j, q3 * dec, 0.0).reshape(cs, LANES))
                rhs_dg.append(jnp.where(row_mod == j, k, 0.0))
            c_last = c[cs - 1:cs, :]
            ops[n, p] = dict(
                lhs_off=both_heads(jnp.concatenate(lhs_off, axis=1)),
                rhs_off=jnp.concatenate(rhs_off, axis=1).astype(BF16),
                lhs_dg=both_heads(jnp.concatenate(lhs_dg, axis=1)),
                rhs_dg=jnp.concatenate(rhs_dg, axis=1).astype(BF16),
                qe=(q * jnp.exp2(c)).astype(BF16),
                kdec=(k * jnp.exp2(c_last - c)).astype(BF16),
                dec=jnp.exp2(c_last))

        attn, upd = {}, {}
        for n, p in probs:
            o = ops[n, p]
            s_off = _dot_nt(o["lhs_off"], o["rhs_off"])
            s_dg = _dot_nt(o["lhs_dg"], o["rhs_dg"])
            attn[n, p] = (s_off + jnp.where(same_blk, s_dg, 0.0)).astype(BF16)
            vv = v_ref[rows[n], p * 2 * LANES:(p + 1) * 2 * LANES]
            upd[n, p] = _dot_tn(vv, o["kdec"])

        inter = {}
        for p in range(2):
            st = state_ref[p]
            for n in range(GLA_GROUP):
                inter[n, p] = _dot_nt(ops[n, p]["qe"], st.astype(BF16))
                st = jnp.where(state_mask, st * ops[n, p]["dec"] + upd[n, p], 0.0)
            state_ref[p] = st

        for n, p in probs:
            for e in range(2):
                cols = slice((2 * p + e) * LANES, (2 * p + e + 1) * LANES)
                o = (_dot(attn[n, p][e * cs:(e + 1) * cs], v_ref[rows[n], cols])
                     + inter[n, p][:, e * LANES:(e + 1) * LANES])
                ms = jnp.mean(o * o, axis=-1, keepdims=True)
                o = o * lax.rsqrt(ms + EPS) * nw_ref[...]
                o_ref[rows[n], cols] = (o * _silu(og_ref[rows[n], cols].astype(F32))
                                        ).astype(o_ref.dtype)
        return carry

    lax.fori_loop(0, seq // (GLA_GROUP * cs), group, 0)


def _gla(u, small, w2p, gkb, nw, l, batch, seq):
    t = batch * seq
    body = functools.partial(_gla_kernel, seq)
    return pl.pallas_call(
        body,
        grid=(batch,),
        in_specs=[pl.BlockSpec((seq, 256), lambda b: (b, 4)),
                  pl.BlockSpec((seq, 256), lambda b: (b, 5)),
                  pl.BlockSpec((seq, 512), lambda b: (b, 3)),
                  pl.BlockSpec((seq, 512), lambda b: (b, 4)),
                  pl.BlockSpec((seq, U_SMALL), lambda b: (b, 0)),
                  pl.BlockSpec((None, U_SMALL, 256), lambda b: (l, 0, 0)),
                  pl.BlockSpec((None, 1, 256), lambda b: (l, 0, 0)),
                  pl.BlockSpec((None, 1, LANES), lambda b: (l, 0, 0))],
        out_specs=pl.BlockSpec((seq, BRANCH), lambda b: (b, 0)),
        out_shape=jax.ShapeDtypeStruct((t, BRANCH), BF16),
        scratch_shapes=[pltpu.VMEM((2, 2 * LANES, LANES), F32)],
        compiler_params=_params("arbitrary"),
        name="gla",
    )(u, u, u, u, small, w2p, gkb, nw)


def _conformer_kernel(seq, tr, a_ref, g_ref, w_ref, b_ref, lnw_ref, lnb_ref, o_ref, gbuf, cbuf):
    halo = 32
    gbuf[0:halo, :] = jnp.zeros((halo, BRANCH), F32)
    for r in range(0, seq, 256):
        gbuf[halo + r:halo + r + 256, :] = (a_ref[r:r + 256, :].astype(F32)
                                            * _sigmoid(g_ref[r:r + 256, :].astype(F32)))

    def conv_tile(i, carry):
        r0 = pl.multiple_of(i * tr, tr)
        for c0 in range(0, BRANCH, LANES):
            win = gbuf[pl.ds(r0, tr + halo), c0:c0 + LANES]
            acc = jnp.broadcast_to(b_ref[:, c0:c0 + LANES], (tr, LANES))
            for j in range(CONV_WIDTH):
                acc = acc + w_ref[j:j + 1, c0:c0 + LANES] * _shift_window(
                    win, halo - (CONV_WIDTH - 1) + j, tr)
            cbuf[pl.ds(r0, tr), c0:c0 + LANES] = acc
        return carry

    lax.fori_loop(0, seq // tr, conv_tile, 0)

    def norm_tile(i, carry):
        r0 = pl.multiple_of(i * tr, tr)
        c = cbuf[pl.ds(r0, tr), :]
        mu = jnp.mean(c, axis=-1, keepdims=True)
        d = c - mu
        var = jnp.mean(d * d, axis=-1, keepdims=True)
        y = d * lax.rsqrt(var + EPS) * lnw_ref[...] + lnb_ref[...]
        o_ref[pl.ds(r0, tr), :] = _silu(y).astype(o_ref.dtype)
        return carry

    lax.fori_loop(0, seq // tr, norm_tile, 0)


def _conformer(u, w, b, lnw, lnb, l, batch, seq, tr=128):
    t = batch * seq
    body = functools.partial(_conformer_kernel, seq, tr)
    return pl.pallas_call(
        body,
        grid=(batch,),
        in_specs=[pl.BlockSpec((seq, BRANCH), lambda i: (i, 5)),
                  pl.BlockSpec((seq, BRANCH), lambda i: (i, 6)),
                  pl.BlockSpec((None, CONV_WIDTH, BRANCH), lambda i: (l, 0, 0)),
                  pl.BlockSpec((None, 1, BRANCH), lambda i: (l, 0, 0)),
                  pl.BlockSpec((None, 1, BRANCH), lambda i: (l, 0, 0)),
                  pl.BlockSpec((None, 1, BRANCH), lambda i: (l, 0, 0))],
        out_specs=pl.BlockSpec((seq, BRANCH), lambda i: (i, 0)),
        out_shape=jax.ShapeDtypeStruct((t, BRANCH), BF16),
        scratch_shapes=[pltpu.VMEM((seq + 32, BRANCH), F32), pltpu.VMEM((seq, BRANCH), F32)],
        compiler_params=_params("arbitrary"),
        name="conformer_conv",
    )(u, u, w, b, lnw, lnb)


def _ssd_kernel(seq, z_ref, x_ref, bc_ref, small_ref, cw_ref, cb_ref, dtb_ref, alog_ref, dsk_ref,
                nw_ref, o_ref, xbuf, state_ref):
    cs = SSD_CHUNK
    halo = SUBLANES
    xbuf[0:halo, :] = jnp.zeros((halo, 2 * BRANCH), F32)
    for r in range(0, seq, 256):
        xbuf[halo + r:halo + r + 256, 0:BRANCH] = x_ref[r:r + 256, :].astype(F32)
        xbuf[halo + r:halo + r + 256, BRANCH:2 * BRANCH] = bc_ref[r:r + 256, :].astype(F32)
    state_ref[...] = jnp.zeros_like(state_ref)

    ri = lax.broadcasted_iota(jnp.int32, (cs, cs), 0)
    ci = lax.broadcasted_iota(jnp.int32, (cs, cs), 1)
    causal = ci <= ri
    tri = causal.astype(BF16)
    er = lax.broadcasted_iota(jnp.int32, (LANES, BRANCH), 0)
    ec = lax.broadcasted_iota(jnp.int32, (LANES, BRANCH), 1)
    expand = ((ec >> 6) == er).astype(BF16)
    lane = lax.broadcasted_iota(jnp.int32, (1, LANES), 1)
    head0 = lane < 64
    a_neg = -jnp.exp(alog_ref[...])

    def chunk(c, carry):
        r0 = pl.multiple_of(c * cs, cs)
        rows = pl.ds(r0, cs)
        win = xbuf[pl.ds(r0, cs + halo), :]
        conv = jnp.broadcast_to(cb_ref[...], (cs, 2 * BRANCH))
        for j in range(SSD_CONV):
            conv = conv + cw_ref[j:j + 1, :] * _shift_window(win, halo - (SSD_CONV - 1) + j, cs)
        xc = _silu(conv)
        xs = xc[:, 0:BRANCH]
        dt = _softplus(small_ref[rows, :] + dtb_ref[...])
        cum = _select_dot(tri, dt * a_neg)
        cum2 = cum * LOG2E
        cum2_t = cum2.T
        ecum = jnp.exp(cum)
        cum_last = cum[cs - 1:cs, :]
        dt_x = _dot_select(dt, expand)
        ecum_x = _dot_select(ecum, expand)
        dte_x = _dot_select(jnp.exp(cum_last - cum), expand)
        xdt = xs * dt_x
        ys = []
        for g in range(2):
            bm = xc[:, BRANCH + g * LANES:BRANCH + (g + 1) * LANES].astype(BF16)
            cm = xc[:, BRANCH + 256 + g * LANES:BRANCH + 256 + (g + 1) * LANES].astype(BF16)
            cb = _dot_nt(cm, bm)
            gl = slice(g * 256, (g + 1) * 256)
            for pr in range(2):
                pl_ = slice(g * 256 + pr * LANES, g * 256 + (pr + 1) * LANES)
                xp = xdt[:, pl_]
                y = None
                for e in range(2):
                    h = g * 4 + pr * 2 + e
                    seg = jnp.where(causal, cum2[:, h:h + 1] - cum2_t[h:h + 1, :], -jnp.inf)
                    m = (cb * jnp.exp2(seg)).astype(BF16)
                    xh = jnp.where(head0 if e == 0 else ~head0, xp, 0.0).astype(BF16)
                    t = _dot(m, xh)
                    y = t if y is None else y + t
                ys.append(y)
            st = state_ref[g]
            y_off = _dot(cm, st.astype(BF16)) * ecum_x[:, gl]
            ys[2 * g] = ys[2 * g] + y_off[:, 0:LANES]
            ys[2 * g + 1] = ys[2 * g + 1] + y_off[:, LANES:2 * LANES]
            upd = _dot_tn(bm, (xdt[:, gl] * dte_x[:, gl]).astype(BF16))
            state_ref[g] = st * ecum_x[cs - 1:cs, gl] + upd
        y = jnp.concatenate(ys, axis=1) + dsk_ref[...] * xs
        y = y * _silu(z_ref[rows, :].astype(F32))
        outs = []
        for g in range(2):
            yg = y[:, g * 256:(g + 1) * 256]
            ms = jnp.mean(yg * yg, axis=-1, keepdims=True)
            outs.append(yg * lax.rsqrt(ms + EPS))
        o_ref[rows, :] = (jnp.concatenate(outs, axis=1) * nw_ref[...]).astype(o_ref.dtype)
        return carry

    lax.fori_loop(0, seq // cs, chunk, 0)


def _ssd(u, small, cw, cb, dtb, alog, dsk, nw, l, batch, seq):
    t = batch * seq
    body = functools.partial(_ssd_kernel, seq)
    vec = lambda n: pl.BlockSpec((None, 1, n), lambda i: (l, 0, 0))
    return pl.pallas_call(
        body,
        grid=(batch,),
        in_specs=[pl.BlockSpec((seq, BRANCH), lambda i: (i, 7)),
                  pl.BlockSpec((seq, BRANCH), lambda i: (i, 8)),
                  pl.BlockSpec((seq, BRANCH), lambda i: (i, 9)),
                  pl.BlockSpec((seq, U_SMALL), lambda i: (i, 0)),
                  pl.BlockSpec((None, SSD_CONV, 2 * BRANCH), lambda i: (l, 0, 0)),
                  vec(2 * BRANCH), vec(LANES), vec(LANES), vec(BRANCH), vec(BRANCH)],
        out_specs=pl.BlockSpec((seq, BRANCH), lambda i: (i, 0)),
        out_shape=jax.ShapeDtypeStruct((t, BRANCH), BF16),
        scratch_shapes=[pltpu.VMEM((seq + SUBLANES, 2 * BRANCH), F32),
                        pltpu.VMEM((2, LANES, 256), F32)],
        compiler_params=_params("arbitrary"),
        name="ssd",
    )(u, u, u, small, cw, cb, dtb, alog, dsk, nw)


def _merge_kernel(h_ref, ya_ref, yb_ref, yc_ref, yd_ref, wg_ref, wb_ref, bg_ref, o_ref,
                  wg_scr, wb_scr):
    @pl.when(pl.program_id(1) == 0)
    def _():
        for i in range(4):
            _cast_rows(wg_scr.at[i], wg_ref.at[i])
            _cast_rows(wb_scr.at[i], wb_ref.at[i])

    h = h_ref[...]
    acc = None
    for i, y_ref in enumerate((ya_ref, yb_ref, yc_ref, yd_ref)):
        gate = _sigmoid(_dot(h, wg_scr[i]) + bg_ref[i:i + 1, :])
        term = gate * _dot(y_ref[...], wb_scr[i])
        acc = term if acc is None else acc + term
    o_ref[...] = acc.astype(o_ref.dtype)


def _merge(h, ys, wg, wb, bg, l, tm=1024, tn=256):
    m, d = h.shape
    yspec = pl.BlockSpec((tm, BRANCH), lambda j, i: (i, 0))
    return pl.pallas_call(
        _merge_kernel,
        grid=(d // tn, m // tm),
        in_specs=[pl.BlockSpec((tm, d), lambda j, i: (i, 0)), yspec, yspec, yspec, yspec,
                  pl.BlockSpec((None, 4, d, tn), lambda j, i: (l, 0, 0, j)),
                  pl.BlockSpec((None, 4, BRANCH, tn), lambda j, i: (l, 0, 0, j)),
                  pl.BlockSpec((None, 4, tn), lambda j, i: (l, 0, j))],
        out_specs=pl.BlockSpec((tm, tn), lambda j, i: (i, j)),
        out_shape=jax.ShapeDtypeStruct((m, d), BF16),
        scratch_shapes=[pltpu.VMEM((4, d, tn), BF16), pltpu.VMEM((4, BRANCH, tn), BF16)],
        compiler_params=_params("arbitrary", "arbitrary"),
        name="gated_merge",
    )(h, *ys, wg, wb, bg)


def _ffn_up_kernel(n_tiles, row_tiles, tiles_per_seq, tm, h_ref, wg_ref, wv_ref, cwg_ref, cwv_ref,
                   cbg_ref, cbv_ref, o_ref, raw_g, raw_v, ghist, vhist, wg_scr, wv_scr):
    halo = SUBLANES
    sub = FFN_ROWS
    s = pl.program_id(0)
    cur = jnp.minimum(s, n_tiles - 1)
    prev = jnp.maximum(s - 1, 0)
    slot = s & 1

    @pl.when(s == 0)
    def _():
        raw_g[1] = jnp.zeros(raw_g.shape[1:], F32)
        raw_v[1] = jnp.zeros(raw_v.shape[1:], F32)

    @pl.when((cur % row_tiles == 0) & (s < n_tiles))
    def _():
        _cast_rows(wg_scr, wg_ref)
        _cast_rows(wv_scr, wv_ref)

    @pl.when(prev % tiles_per_seq == 0)
    def _():
        ghist[...] = jnp.zeros_like(ghist)
        vhist[...] = jnp.zeros_like(vhist)

    def conv(raw, old, hist, cw_ref, cb_ref, r, cols):
        if r == 0:
            win = jnp.concatenate([hist[:, cols], raw[old, 0:EP_ROWS, cols]], axis=0)
        else:
            win = raw[old, r - halo:r + EP_ROWS, cols]
        out = jnp.broadcast_to(cb_ref[:, cols], (EP_ROWS, LANES))
        for j in range(FFN_CONV):
            out = out + cw_ref[j:j + 1, cols] * _shift_window(win, halo - (FFN_CONV - 1) + j,
                                                              EP_ROWS)
        return out

    def step(new, old):
        tn = raw_g.shape[2]
        for r0 in range(0, tm, sub):
            h = h_ref[r0:r0 + sub, :]
            pieces = [(r, c) for r in range(r0, r0 + sub, EP_ROWS) for c in range(0, tn, LANES)]
            dots = [(raw, w, c) for c in range(0, tn, MXU_COLS)
                    for raw, w in ((raw_g, wg_scr), (raw_v, wv_scr))]
            per_dot = len(pieces) // len(dots)
            for n, (raw, w, c) in enumerate(dots):
                raw[new, r0:r0 + sub, c:c + MXU_COLS] = _dot(h, w[:, c:c + MXU_COLS])
                for r, pc in pieces[n * per_dot:(n + 1) * per_dot]:
                    cols = slice(pc, pc + LANES)
                    gate = conv(raw_g, old, ghist, cwg_ref, cbg_ref, r, cols)
                    val = conv(raw_v, old, vhist, cwv_ref, cbv_ref, r, cols)
                    o_ref[r:r + EP_ROWS, cols] = (_silu(gate) * val).astype(o_ref.dtype)
        ghist[...] = raw_g[old, tm - halo:tm, :]
        vhist[...] = raw_v[old, tm - halo:tm, :]

    @pl.when(slot == 0)
    def _():
        step(0, 1)

    @pl.when(slot == 1)
    def _():
        step(1, 0)


def _ffn_up(h, w_up, cw, cb, l, seq, tm=1024, tn=512):
    m, d = h.shape
    nt = D_FF // tn
    rt = m // tm
    n_tiles = nt * rt
    body = functools.partial(_ffn_up_kernel, n_tiles, rt, seq // tm, tm)
    cur = lambda s: jnp.minimum(s, n_tiles - 1)
    prev = lambda s: jnp.maximum(s - 1, 0)
    return pl.pallas_call(
        body,
        grid=(n_tiles + 1,),
        in_specs=[pl.BlockSpec((tm, d), lambda s: (cur(s) % rt, 0)),
                  pl.BlockSpec((None, d, tn), lambda s: (l, 0, cur(s) // rt)),
                  pl.BlockSpec((None, d, tn), lambda s: (l, 0, cur(s) // rt + nt)),
                  pl.BlockSpec((None, FFN_CONV, tn), lambda s: (l, 0, prev(s) // rt)),
                  pl.BlockSpec((None, FFN_CONV, tn), lambda s: (l, 0, prev(s) // rt + nt)),
                  pl.BlockSpec((None, 1, tn), lambda s: (l, 0, prev(s) // rt)),
                  pl.BlockSpec((None, 1, tn), lambda s: (l, 0, prev(s) // rt + nt))],
        out_specs=pl.BlockSpec((tm, tn), lambda s: (prev(s) % rt, prev(s) // rt)),
        out_shape=jax.ShapeDtypeStruct((m, D_FF), BF16),
        scratch_shapes=[pltpu.VMEM((2, tm, tn), F32), pltpu.VMEM((2, tm, tn), F32),
                        pltpu.VMEM((SUBLANES, tn), F32), pltpu.VMEM((SUBLANES, tn), F32),
                        pltpu.VMEM((d, tn), BF16), pltpu.VMEM((d, tn), BF16)],
        compiler_params=_params("arbitrary"),
        name="ffn_up_conv_gate",
    )(h, w_up, w_up, cw, cw, cb, cb)


def kernel(x, mix_norm, w_in, diff_qk_norm, diff_lambda, diff_subln, gla_gk_w2, gla_gk_b, gla_norm, conv_dw_w, conv_dw_b, conv_ln_w, conv_ln_b, ssd_conv_w, ssd_conv_b, ssd_dt_bias, ssd_a_log, ssd_d, ssd_norm, w_branch, w_gate, b_gate, w_out, ffn_norm, ffn_w_up, ffn_conv_w, ffn_conv_b, ffn_w_down):
    batch, seq, d = x.shape
    depth = w_in.shape[0]
    t = batch * seq
    xf = x.reshape(t, d)

    w2_pad = jnp.zeros((depth, U_SMALL, 256), F32).at[:, 8:24, :].set(gla_gk_w2).astype(BF16)
    pad8 = lambda v: jnp.pad(v, ((0, 0), (0, LANES - 8)))[:, None, :]
    row = lambda v: v[:, None, :]
    qk_gain = jnp.tile(diff_qk_norm, (1, 1, 2))
    dskip = jnp.repeat(ssd_d, 64, axis=-1)[:, None, :]
    w_in_t = jnp.swapaxes(w_in, 1, 2)

    for l in range(depth):
        lambda_init = 0.8 - 0.6 * math.exp(-0.3 * l)
        h, small, uq = _rmsnorm_small(xf, row(mix_norm), w_in_t, l)
        u = _in_proj(h, w_in_t, l)
        ya = _diff_attention(uq, u, qk_gain, diff_lambda, row(diff_subln), l, lambda_init, batch, seq)
        yb = _gla(u, small, w2_pad, row(gla_gk_b), row(gla_norm), l, batch, seq)
        yc = _conformer(u, conv_dw_w, row(conv_dw_b), row(conv_ln_w), row(conv_ln_b), l, batch, seq)
        yd = _ssd(u, small, ssd_conv_w, row(ssd_conv_b), pad8(ssd_dt_bias), pad8(ssd_a_log), dskip,
                  row(ssd_norm), l, batch, seq)
        merged = _merge(h, (ya, yb, yc, yd), w_gate, w_branch, b_gate, l)
        xf, h2 = _proj_res_norm(merged, w_out, l, xf, row(ffn_norm), name="out_proj_norm")
        act = _ffn_up(h2, ffn_w_up, ffn_conv_w, row(ffn_conv_b), l, seq)
        xf = _matmul_res(act, ffn_w_down, l, xf, tm=512, tn=512, name="ffn_down")
    return xf.reshape(batch, seq, d)
```
